```python
import math
import jax, jax.numpy as jnp
from jax import lax
import numpy as np

D_MODEL = 1024
BATCH = 8
SEQ = 2048
DEPTH = 4

HEAD_DIM = 64
A_Q_HEADS = 8
A_KV_HEADS = 2
A_WIDTH = A_Q_HEADS * HEAD_DIM
A_KV_WIDTH = A_KV_HEADS * HEAD_DIM
WINDOW = 128
BLOCK = 128
B_WIDTH = 256
B_BLOCKS = 4
B_BLOCK_DIM = B_WIDTH // B_BLOCKS
CONV_WIDTH = 4
LRU_C = 8.0
C_HEADS = 4
C_WIDTH = C_HEADS * HEAD_DIM
DECAY_RANK = 32
AICL_RANK = 32
VRES_RANK = 16
C_SHIFT_WIDTH = 3 * C_WIDTH + DECAY_RANK + AICL_RANK
GN_EPS = 64e-5

D_MIX = A_WIDTH + B_WIDTH + C_WIDTH
SPLIT_SIZES = (A_WIDTH, A_KV_WIDTH, A_KV_WIDTH, A_WIDTH,
               B_WIDTH, B_WIDTH,
               C_SHIFT_WIDTH, C_WIDTH)
D_IN = sum(SPLIT_SIZES)
LN_EPS = 1e-5
ALPHA = (2 * DEPTH) ** 0.25
BETA = (8 * DEPTH) ** -0.25

kernel_name = "hybrid_swa_rglru_rwkv7_deepnorm"


def _layer_norm(x, g, b):
    xf = x.astype(jnp.float32)
    mu = jnp.mean(xf, axis=-1, keepdims=True)
    var = jnp.mean(jnp.square(xf - mu), axis=-1, keepdims=True)
    return (xf - mu) * lax.rsqrt(var + LN_EPS) * g + b


def _swa_sinks(q, k, v, sinks):
    bsz, seq = q.shape[0], q.shape[1]
    nb = seq // BLOCK
    grp = A_Q_HEADS // A_KV_HEADS
    qb = q.astype(jnp.float32).reshape(bsz, nb, BLOCK, A_KV_HEADS, grp, HEAD_DIM) * (HEAD_DIM ** -0.5)

    def band(t):
        t = t.astype(jnp.float32).reshape(bsz, seq, A_KV_HEADS, HEAD_DIM)
        prev = jnp.pad(t, ((0, 0), (BLOCK, 0), (0, 0), (0, 0)))[:, :seq]
        prev = prev.reshape(bsz, nb, BLOCK, A_KV_HEADS, HEAD_DIM)
        cur = t.reshape(bsz, nb, BLOCK, A_KV_HEADS, HEAD_DIM)
        return jnp.concatenate([prev, cur], axis=2)

    kb, vb = band(k), band(v)
    s = jnp.einsum("bnqhgd,bnkhd->bnhgqk", qb, kb)
    qi = jnp.arange(BLOCK)[:, None]
    kj = jnp.arange(2 * BLOCK)[None, :]
    diff = qi + BLOCK - kj
    band_mask = (diff >= 0) & (diff < WINDOW)
    key_pos = jnp.arange(nb)[:, None] * BLOCK - BLOCK + jnp.arange(2 * BLOCK)[None, :]
    mask = band_mask[None] & (key_pos >= 0)[:, None, :]
    s = jnp.where(mask[None, :, None, None], s, -jnp.inf)
    sink = sinks.astype(jnp.float32).reshape(A_KV_HEADS, grp)[None, None, :, :, None, None]
    m = jnp.maximum(jnp.max(s, axis=-1, keepdims=True), sink)
    p = jnp.exp(s - m)
    denom = jnp.sum(p, axis=-1, keepdims=True) + jnp.exp(sink - m)
    o = jnp.einsum("bnhgqk,bnkhd->bnqhgd", p / denom, vb)
    return o.reshape(bsz, seq, A_WIDTH)


def _rg_lru(xb, conv_w, conv_b, wa, ba, wx, bx, lam):
    bsz, seq = xb.shape[0], xb.shape[1]
    xf = xb.astype(jnp.float32)
    xc = lax.conv_general_dilated(
        xf, conv_w.astype(jnp.float32).reshape(CONV_WIDTH, 1, B_WIDTH),
        window_strides=(1,), padding=[(CONV_WIDTH - 1, 0)],
        dimension_numbers=("NWC", "WIO", "NWC"), feature_group_count=B_WIDTH) + conv_b
    xh = xc.reshape(bsz, seq, B_BLOCKS, B_BLOCK_DIM)
    r = jax.nn.sigmoid(jnp.einsum("bsnd,nde->bsne", xh, wa).reshape(bsz, seq, B_WIDTH) + ba)
    i = jax.nn.sigmoid(jnp.einsum("bsnd,nde->bsne", xh, wx).reshape(bsz, seq, B_WIDTH) + bx)
    log_a = -LRU_C * r * jax.nn.softplus(-lam.astype(jnp.float32))
    a = jnp.exp(log_a)
    u = jnp.sqrt(-jnp.expm1(2.0 * log_a)) * (i * xc)

    def combine(e1, e2):
        a1, b1 = e1
        a2, b2 = e2
        return a1 * a2, a2 * b1 + b2

    _, h = lax.associative_scan(combine, (a, u), axis=1)
    return h


def _rwkv7(cs, mu, w0, w2, a0, a2, k_k, k_a, r_k, gn_w, gn_b, vres):
    bsz, seq = cs.shape[0], cs.shape[1]
    cs = cs.astype(jnp.float32)
    prev = jnp.pad(cs, ((0, 0), (1, 0), (0, 0)))[:, :seq]
    xs = cs + (prev - cs) * mu
    r, k, v, wl, al = jnp.split(xs, [C_WIDTH, 2 * C_WIDTH, 3 * C_WIDTH, 3 * C_WIDTH + DECAY_RANK], axis=-1)
    logw = -jax.nn.softplus(-(w0 + jnp.tanh(wl) @ w2)) - 0.5
    decay = jnp.exp(-jnp.exp(logw))
    a = jax.nn.sigmoid(a0 + al @ a2)
    v_own = v
    if vres is not None:
        v_first, v0, v1, v2 = vres
        v = v + (v_first - v) * jax.nn.sigmoid(v0 + (v @ v1) @ v2)

    def heads(t):
        return t.reshape(bsz, seq, C_HEADS, HEAD_DIM)

    kk = heads(k * k_k)
    kk = kk * lax.rsqrt(jnp.sum(jnp.square(kk), axis=-1, keepdims=True) + 1e-12)
    k = k * (1.0 + (a - 1.0) * k_a)
    rh, wh, kh, vh, ah = heads(r), heads(decay), heads(k), heads(v), heads(a)
    bh = kk * ah

    def step(state, inp):
        r_t, w_t, k_t, v_t, kk_t, b_t = inp
        sa = jnp.einsum("bhvk,bhk->bhv", state, -kk_t)
        state = (state * w_t[:, :, None, :] + sa[..., None] * b_t[:, :, None, :]
                 + v_t[..., None] * k_t[:, :, None, :])
        y_t = jnp.einsum("bhvk,bhk->bhv", state, r_t)
        return state, y_t

    tm = lambda t: jnp.swapaxes(t, 0, 1)
    s0 = jnp.zeros((bsz, C_HEADS, HEAD_DIM, HEAD_DIM), jnp.float32)
    _, y = lax.scan(step, s0, (tm(rh), tm(wh), tm(kh), tm(vh), tm(kk), tm(bh)))
    y = tm(y)
    ym = jnp.mean(y, axis=-1, keepdims=True)
    yv = jnp.mean(jnp.square(y - ym), axis=-1, keepdims=True)
    y = (y - ym) * lax.rsqrt(yv + GN_EPS) * gn_w.reshape(C_HEADS, HEAD_DIM) + gn_b.reshape(C_HEADS, HEAD_DIM)
    bonus = jnp.sum(rh * kh * r_k, axis=-1, keepdims=True) * vh
    return (y + bonus).reshape(bsz, seq, C_WIDTH), v_own


def setup_inputs(seed: int = 0) -> dict:
    key = jax.random.key(seed)
    ks = jax.random.split(key, 32)
    n = lambda i, shape: jax.random.normal(ks[i], shape, jnp.float32)
    u_lru = jax.random.uniform(ks[11], (DEPTH, B_WIDTH), jnp.float32, 0.9, 0.999)
    a_base = u_lru ** (1.0 / LRU_C)
    w0_base = jnp.linspace(-6.0, -1.0, C_WIDTH, dtype=jnp.float32)[None, :]
    return {
        "x": n(0, (BATCH, SEQ, D_MODEL)),
        "w_in": n(1, (DEPTH, D_MODEL, D_IN)) * D_MODEL ** -0.5,
        "w_out": n(2, (DEPTH, D_MIX, D_MODEL)) * (D_MIX ** -0.5) * BETA,
        "ln_g": 1.0 + 0.02 * n(3, (DEPTH, D_MODEL)),
        "ln_b": 0.02 * n(4, (DEPTH, D_MODEL)),
        "attn_sinks": 0.5 * n(5, (DEPTH, A_Q_HEADS)),
        "conv_w": n(6, (DEPTH, CONV_WIDTH, B_WIDTH)) * CONV_WIDTH ** -0.5,
        "conv_b": 0.02 * n(7, (DEPTH, B_WIDTH)),
        "lru_wa": n(8, (DEPTH, B_BLOCKS, B_BLOCK_DIM, B_BLOCK_DIM)) * B_BLOCK_DIM ** -0.5,
        "lru_ba": 0.02 * n(9, (DEPTH, B_WIDTH)),
        "lru_wx": n(10, (DEPTH, B_BLOCKS, B_BLOCK_DIM, B_BLOCK_DIM)) * B_BLOCK_DIM ** -0.5,
        "lru_bx": 0.02 * n(12, (DEPTH, B_WIDTH)),
        "lru_lambda": jnp.log(a_base) - jnp.log1p(-a_base),
        "rwkv_mu": jax.random.uniform(ks[13], (DEPTH, C_SHIFT_WIDTH), jnp.float32),
        "rwkv_w0": w0_base + 0.1 * n(14, (DEPTH, C_WIDTH)),
        "rwkv_w2": 0.1 * n(15, (DEPTH, DECAY_RANK, C_WIDTH)),
        "rwkv_a0": 0.1 * n(16, (DEPTH, C_WIDTH)),
        "rwkv_a2": 0.1 * n(17, (DEPTH, AICL_RANK, C_WIDTH)),
        "rwkv_kk": 0.85 + 0.02 * n(18, (DEPTH, C_WIDTH)),
        "rwkv_ka": 1.0 + 0.02 * n(19, (DEPTH, C_WIDTH)),
        "rwkv_rk": -0.04 + 0.02 * n(20, (DEPTH, C_HEADS, HEAD_DIM)),
        "rwkv_gn_w": 1.0 + 0.02 * n(21, (DEPTH, C_WIDTH)),
        "rwkv_gn_b": 0.02 * n(22, (DEPTH, C_WIDTH)),
        "rwkv_v0": 1.0 + 0.1 * n(23, (DEPTH - 1, C_WIDTH)),
        "rwkv_v1": n(24, (DEPTH - 1, C_WIDTH, VRES_RANK)) * C_WIDTH ** -0.5,
        "rwkv_v2": 0.1 * n(25, (DEPTH - 1, VRES_RANK, C_WIDTH)),
    }


def reference(x, w_in, w_out, ln_g, ln_b, attn_sinks, conv_w, conv_b, lru_wa, lru_ba,
              lru_wx, lru_bx, lru_lambda, rwkv_mu, rwkv_w0, rwkv_w2, rwkv_a0, rwkv_a2,
              rwkv_kk, rwkv_ka, rwkv_rk, rwkv_gn_w, rwkv_gn_b, rwkv_v0, rwkv_v1, rwkv_v2):
    split_idx = [int(c) for c in np.cumsum(SPLIT_SIZES)[:-1]]
    v_first = None
    for l in range(DEPTH):
        proj = jnp.einsum("bsd,de->bse", x, w_in[l])
        q, k, v, g_a, x_b, g_b, c_cols, g_c = jnp.split(proj, split_idx, axis=-1)
        y_a = _swa_sinks(q, k, v, attn_sinks[l])
        y_b = _rg_lru(x_b, conv_w[l], conv_b[l], lru_wa[l], lru_ba[l], lru_wx[l], lru_bx[l], lru_lambda[l])
        vres = None if l == 0 else (v_first, rwkv_v0[l - 1], rwkv_v1[l - 1], rwkv_v2[l - 1])
        y_c, v_own = _rwkv7(c_cols, rwkv_mu[l], rwkv_w0[l], rwkv_w2[l], rwkv_a0[l], rwkv_a2[l],
                            rwkv_kk[l], rwkv_ka[l], rwkv_rk[l], rwkv_gn_w[l], rwkv_gn_b[l], vres)
        if l == 0:
            v_first = v_own
        silu = lambda t: jax.nn.silu(t.astype(jnp.float32))
        y = jnp.concatenate([y_a * silu(g_a), y_b * silu(g_b), y_c * silu(g_c)], axis=-1)
        out = jnp.einsum("bse,ed->bsd", y, w_out[l])
        x = _layer_norm(ALPHA * x.astype(jnp.float32) + out, ln_g[l], ln_b[l]).astype(x.dtype)
    return x
```

```python
import functools
import math

import jax
import jax.numpy as jnp
from jax import lax
from jax.experimental import pallas as pl
from jax.experimental.pallas import tpu as pltpu

F32 = jnp.float32
BF16 = jnp.bfloat16

D_MODEL = 1024
HEAD_DIM = 64
A_Q_HEADS = 8
A_KV_HEADS = 2
A_WIDTH = A_Q_HEADS * HEAD_DIM
A_KV_WIDTH = A_KV_HEADS * HEAD_DIM
ATT_BLOCK = 128
B_WIDTH = 256
B_BLOCKS = 4
CONV_WIDTH = 4
LRU_C = 8.0
C_HEADS = 4
C_WIDTH = C_HEADS * HEAD_DIM
DECAY_RANK = 32
AICL_RANK = 32
VRES_RANK = 16
C_SHIFT_WIDTH = 3 * C_WIDTH + DECAY_RANK + AICL_RANK
GN_EPS = 64e-5
LN_EPS = 1e-5

LANES = 128
C_SHIFT_PAD = ((C_SHIFT_WIDTH + LANES - 1) // LANES) * LANES
GA_WIDTH = A_WIDTH + 2 * A_KV_WIDTH + A_WIDTH
GB_WIDTH = 2 * B_WIDTH
GC_WIDTH = C_SHIFT_PAD + C_WIDTH
D_IN_PAD = GA_WIDTH + GB_WIDTH + GC_WIDTH
D_MIX = A_WIDTH + B_WIDTH + C_WIDTH

CHUNK = 64
STACK = C_HEADS * CHUNK
LRU_TILE = 256
PROJ_TM = 512
OUT_TM = 512
VMEM_LIMIT = 56 * 1024 * 1024


def _mm(a, b):
    return jnp.dot(a.astype(BF16), b.astype(BF16), preferred_element_type=F32)


def _mm_nt(a, b):
    return lax.dot_general(a.astype(BF16), b.astype(BF16), (((1,), (1,)), ((), ())),
                           preferred_element_type=F32)


def _mm_tn(a, b):
    return lax.dot_general(a.astype(BF16), b.astype(BF16), (((0,), (0,)), ((), ())),
                           preferred_element_type=F32)


def _split_bf16(x, parts):
    out = []
    rem = x
    for _ in range(parts):
        hi = rem.astype(BF16)
        out.append(hi)
        rem = rem - hi.astype(F32)
    return out


def _mm_exact_rhs(x, m_bf16, parts):
    acc = None
    for p in _split_bf16(x, parts):
        t = jnp.dot(p, m_bf16, preferred_element_type=F32)
        acc = t if acc is None else acc + t
    return acc


def _mm_exact_lhs(m_bf16, x, parts):
    acc = None
    for p in _split_bf16(x, parts):
        t = jnp.dot(m_bf16, p, preferred_element_type=F32)
        acc = t if acc is None else acc + t
    return acc


def _sigmoid(x):
    return 1.0 / (1.0 + jnp.exp(-x))


def _silu(x):
    return x * _sigmoid(x)


def _softplus(x):
    return jnp.maximum(x, 0.0) + jnp.log(1.0 + jnp.exp(-jnp.abs(x)))


def _proj_kernel(x_ref, w_ref, oa_ref, ob_ref, oc_ref):
    x = x_ref[...].astype(BF16)
    oa_ref[...] = jnp.dot(x, w_ref[:, 0:GA_WIDTH], preferred_element_type=F32)
    ob_ref[...] = jnp.dot(x, w_ref[:, GA_WIDTH:GA_WIDTH + GB_WIDTH], preferred_element_type=F32)
    oc_ref[...] = jnp.dot(x, w_ref[:, GA_WIDTH + GB_WIDTH:D_IN_PAD], preferred_element_type=F32)


def _proj(x2, w_l):
    n = x2.shape[0]
    return pl.pallas_call(
        _proj_kernel,
        grid=(n // PROJ_TM,),
        in_specs=[pl.BlockSpec((PROJ_TM, D_MODEL), lambda i: (i, 0)),
                  pl.BlockSpec((D_MODEL, D_IN_PAD), lambda i: (0, 0))],
        out_specs=[pl.BlockSpec((PROJ_TM, GA_WIDTH), lambda i: (i, 0)),
                   pl.BlockSpec((PROJ_TM, GB_WIDTH), lambda i: (i, 0)),
                   pl.BlockSpec((PROJ_TM, GC_WIDTH), lambda i: (i, 0))],
        out_shape=[jax.ShapeDtypeStruct((n, GA_WIDTH), F32),
                   jax.ShapeDtypeStruct((n, GB_WIDTH), F32),
                   jax.ShapeDtypeStruct((n, GC_WIDTH), F32)],
        compiler_params=pltpu.CompilerParams(dimension_semantics=("arbitrary",),
                                             vmem_limit_bytes=VMEM_LIMIT),
        name="proj",
    )(x2, w_l)


def _attn_kernel(sink_ref, cur_ref, prev_ref, o_ref):
    n = pl.program_id(1)
    blk = ATT_BLOCK
    k_band = jnp.concatenate([prev_ref[0, :, 0:A_KV_WIDTH], cur_ref[0, :, A_WIDTH:A_WIDTH + A_KV_WIDTH]], axis=0)
    v_band = jnp.concatenate([prev_ref[0, :, A_KV_WIDTH:2 * A_KV_WIDTH],
                              cur_ref[0, :, A_WIDTH + A_KV_WIDTH:A_WIDTH + 2 * A_KV_WIDTH]], axis=0)
    lane = lax.broadcasted_iota(jnp.int32, (2 * blk, A_KV_WIDTH), 1)
    lo = lane < HEAD_DIM
    k_sw = pltpu.roll(k_band, HEAD_DIM, axis=1)
    v_sw = pltpu.roll(v_band, HEAD_DIM, axis=1)
    k_var = ((jnp.where(lo, k_band, 0.0).astype(BF16), jnp.where(lo, 0.0, k_sw).astype(BF16)),
             (jnp.where(lo, k_sw, 0.0).astype(BF16), jnp.where(lo, 0.0, k_band).astype(BF16)))
    v_var = ((jnp.where(lo, v_band, 0.0).astype(BF16), jnp.where(lo, 0.0, v_sw).astype(BF16)),
             (jnp.where(lo, v_sw, 0.0).astype(BF16), jnp.where(lo, 0.0, v_band).astype(BF16)))
    qi = lax.broadcasted_iota(jnp.int32, (blk, 2 * blk), 0)
    kj = lax.broadcasted_iota(jnp.int32, (blk, 2 * blk), 1)
    diff = qi + blk - kj
    key_pos = kj + (n - 1) * blk
    mask = (diff >= 0) & (diff < blk) & (key_pos >= 0)
    scale = HEAD_DIM ** -0.5
    grp = A_Q_HEADS // A_KV_HEADS
    for p in range(A_Q_HEADS // 2):
        h = (2 * p) // grp
        qp = cur_ref[0, :, p * LANES:(p + 1) * LANES].astype(BF16)
        acc = None
        for half in range(2):
            sink = sink_ref[2 * p + half]
            s = lax.dot_general(qp, k_var[h][half], (((1,), (1,)), ((), ())),
                                preferred_element_type=F32) * scale
            s = jnp.where(mask, s, -jnp.inf)
            m = jnp.maximum(jnp.max(s, axis=-1, keepdims=True), sink)
            e = jnp.exp(s - m)
            den = jnp.sum(e, axis=-1, keepdims=True) + jnp.exp(sink - m)
            o = jnp.dot(e.astype(BF16), v_var[h][half], preferred_element_type=F32) * (1.0 / den)
            acc = o if acc is None else acc + o
        g = cur_ref[0, :, A_WIDTH + 2 * A_KV_WIDTH + p * LANES:A_WIDTH + 2 * A_KV_WIDTH + (p + 1) * LANES]
        o_ref[0, :, p * LANES:(p + 1) * LANES] = (acc * _silu(g)).astype(BF16)


def _attn(pa, sinks_l):
    bsz, seq, _ = pa.shape
    nb = seq // ATT_BLOCK
    kv_blk = (A_WIDTH) // (2 * A_KV_WIDTH)
    return pl.pallas_call(
        _attn_kernel,
        grid=(bsz, nb),
        in_specs=[pl.BlockSpec(memory_space=pltpu.SMEM),
                  pl.BlockSpec((1, ATT_BLOCK, GA_WIDTH), lambda b, n: (b, n, 0)),
                  pl.BlockSpec((1, ATT_BLOCK, 2 * A_KV_WIDTH),
                               lambda b, n: (b, jnp.maximum(n - 1, 0), kv_blk))],
        out_specs=pl.BlockSpec((1, ATT_BLOCK, A_WIDTH), lambda b, n: (b, n, 0)),
        out_shape=jax.ShapeDtypeStruct((bsz, seq, A_WIDTH), BF16),
        compiler_params=pltpu.CompilerParams(dimension_semantics=("arbitrary", "arbitrary"),
                                             vmem_limit_bytes=VMEM_LIMIT),
        name="attn",
    )(sinks_l, pa, pa)


def _shift_rows(x, d, fill):
    n = x.shape[0]
    if d % 8 == 0:
        return jnp.concatenate([jnp.full((d, x.shape[1]), fill, x.dtype), x[:n - d]], axis=0)
    row = lax.broadcasted_iota(jnp.int32, x.shape, 0)
    return jnp.where(row < d, fill, pltpu.roll(x, d, axis=0))


def _lru_kernel(pb_ref, cw_ref, vec_ref, wa_ref, wx_ref, o_ref):
    seq = pb_ref.shape[1]
    tile = LRU_TILE
    conv_b = vec_ref[0:1, :]
    ba = vec_ref[1:2, :]
    bx = vec_ref[2:3, :]
    lam = vec_ref[3:4, :]
    neg_c_sp = -LRU_C * _softplus(-lam)

    def body(i, hc):
        t0 = pl.multiple_of(i * tile, tile)
        x = pb_ref[0, pl.ds(t0, tile), 0:B_WIDTH]
        tp = pl.multiple_of(jnp.maximum(t0 - 8, 0), 8)
        xp = pb_ref[0, pl.ds(tp, 8), 0:B_WIDTH]
        xp = jnp.where(i > 0, xp, 0.0)
        xe = jnp.concatenate([xp, x], axis=0)
        xc = cw_ref[CONV_WIDTH - 1:CONV_WIDTH, :] * x + conv_b
        for j in range(1, CONV_WIDTH):
            xs = pltpu.roll(xe, j, axis=0)[8:8 + tile]
            xc = xc + cw_ref[CONV_WIDTH - 1 - j:CONV_WIDTH - j, :] * xs
        r = _sigmoid(_mm(xc, wa_ref[...]) + ba)
        ig = _sigmoid(_mm(xc, wx_ref[...]) + bx)
        log_a = neg_c_sp * r
        a = jnp.exp(log_a)
        u = jnp.sqrt(1.0 - a * a) * (ig * xc)
        d = 1
        while d < tile:
            u = a * _shift_rows(u, d, 0.0) + u
            a = a * _shift_rows(a, d, 1.0)
            d *= 2
        h = u + a * hc
        g = pb_ref[0, pl.ds(t0, tile), B_WIDTH:2 * B_WIDTH]
        o_ref[0, pl.ds(t0, tile), :] = (h * _silu(g)).astype(BF16)
        return h[tile - 1:tile, :]

    lax.fori_loop(0, seq // tile, body, jnp.zeros((1, B_WIDTH), F32))


def _lru(pb, cw_l, vec_l, wa_l, wx_l):
    bsz, seq, _ = pb.shape
    full = lambda shape: pl.BlockSpec(shape, lambda b: (0,) * len(shape))
    return pl.pallas_call(
        _lru_kernel,
        grid=(bsz,),
        in_specs=[pl.BlockSpec((1, seq, GB_WIDTH), lambda b: (b, 0, 0)),
                  full((CONV_WIDTH, B_WIDTH)), full((4, B_WIDTH)),
                  full((B_WIDTH, B_WIDTH)), full((B_WIDTH, B_WIDTH))],
        out_specs=pl.BlockSpec((1, seq, B_WIDTH), lambda b: (b, 0, 0)),
        out_shape=jax.ShapeDtypeStruct((bsz, seq, B_WIDTH), BF16),
        compiler_params=pltpu.CompilerParams(dimension_semantics=("arbitrary",),
                                             vmem_limit_bytes=VMEM_LIMIT),
        name="lru",
    )(pb, cw_l, vec_l, wa_l, wx_l)


def _stack_heads(x, bd):
    return jnp.where(bd, jnp.concatenate([x] * C_HEADS, axis=0), 0.0)


def _rwkv_chunk(cs, prev_row, g, vfirst, h_in, mu, vec, w2p, a2p, v1p, v2p, ones_bd, tril_l,
                bd_f, sl_f, le_f, eye_f, use_vres):
    w0, a0, k_k, k_a, r_k, gn_w, gn_b, v0 = (vec[i:i + 1, :] for i in range(8))
    row = lax.broadcasted_iota(jnp.int32, cs.shape, 0)
    prev = jnp.where(row == 0, prev_row, pltpu.roll(cs, 1, axis=0))
    xs = cs + (prev - cs) * mu
    r = xs[:, 0:C_WIDTH]
    k = xs[:, C_WIDTH:2 * C_WIDTH]
    v = xs[:, 2 * C_WIDTH:3 * C_WIDTH]
    la = xs[:, 3 * C_WIDTH:C_SHIFT_PAD]
    logw = -_softplus(-(w0 + _mm(jnp.tanh(la), w2p))) - 0.5
    lw = -jnp.exp(logw)
    a = _sigmoid(a0 + _mm(la, a2p))
    v_own = v
    if use_vres:
        v = v + (vfirst - v) * _sigmoid(v0 + _mm(_mm(v, v1p), v2p))
    kk = k * k_k
    kap = kk * lax.rsqrt(_mm_exact_rhs(kk * kk, ones_bd, 2) + 1e-12)
    k2 = k * (1.0 + (a - 1.0) * k_a)
    b = kap * a

    cum = _mm_exact_lhs(tril_l, lw, 3)
    cum_end = cum[CHUNK - 1:CHUNK, :]
    e_in = jnp.exp(cum)
    e_ex = jnp.exp(cum - lw)
    e_neg = jnp.exp(-cum)
    e_end = jnp.exp(cum_end - cum)
    p_end = jnp.exp(cum_end)

    bd = bd_f > 0.0
    kt = _stack_heads(kap * e_ex, bd).astype(BF16)
    rt = _stack_heads(r * e_in, bd)
    bt = _stack_heads(b * e_neg, bd).astype(BF16)
    kq = _stack_heads(k2 * e_neg, bd).astype(BF16)
    bh = _stack_heads(b * e_end, bd).astype(BF16)
    kh = _stack_heads(k2 * e_end, bd).astype(BF16)
    vs = _stack_heads(v, bd).astype(BF16)

    rt16 = rt.astype(BF16)
    gram = _mm_nt(jnp.concatenate([kt, rt16], axis=0), jnp.concatenate([bt, kq], axis=0))
    sl = sl_f > 0.0
    le = le_f > 0.0
    a_ab = jnp.where(sl, gram[0:STACK, 0:STACK], 0.0)
    a_ak = jnp.where(sl, gram[0:STACK, STACK:2 * STACK], 0.0)
    a_rb = jnp.where(le, gram[STACK:2 * STACK, 0:STACK], 0.0)
    a_rk = jnp.where(le, gram[STACK:2 * STACK, STACK:2 * STACK], 0.0)

    pw = -a_ab
    t_inv = eye_f + pw
    pw = _mm(pw, pw)
    for _ in range(1, int(math.log2(CHUNK)) - 1):
        both = _mm(jnp.concatenate([t_inv, pw], axis=0), pw)
        t_inv = t_inv + both[0:STACK]
        pw = both[STACK:2 * STACK]
    t_inv = t_inv + _mm(t_inv, pw)

    av = _mm(jnp.concatenate([a_ak, a_rk], axis=0), vs)
    x1 = jnp.concatenate([kt, av[0:STACK].astype(BF16)], axis=1)
    tx = _mm(t_inv, x1)
    tx16 = tx.astype(BF16)
    ab = _mm(a_rb, tx16)
    bm = _mm_tn(bh, tx16)
    kv = _mm_tn(kh, vs)
    g_mat = eye_f * p_end - bm[:, 0:STACK]
    c_mat = kv - bm[:, STACK:2 * STACK]
    q_mat = rt - ab[:, 0:STACK]
    d_mat = av[STACK:2 * STACK] - ab[:, STACK:2 * STACK]

    y_st = _mm(q_mat, h_in) + d_mat
    h_out = _mm(g_mat, h_in) + c_mat
    y = y_st[0:CHUNK]
    for hh in range(1, C_HEADS):
        y = y + y_st[hh * CHUNK:(hh + 1) * CHUNK]

    inv_n = 1.0 / HEAD_DIM
    ym = _mm_exact_rhs(y, ones_bd, 3) * inv_n
    yc = y - ym
    yv = _mm_exact_rhs(yc * yc, ones_bd, 2) * inv_n
    yn = yc * lax.rsqrt(yv + GN_EPS) * gn_w + gn_b
    bonus = _mm_exact_rhs(r * k2 * r_k, ones_bd, 3) * v
    out = (yn + bonus) * _silu(g)
    return out, v_own, h_out


def _rwkv_kernel(use_vres, pc_ref, vf_ref, mu_ref, vec_ref, w2_ref, a2_ref, v1_ref, v2_ref,
                 ones_ref, tril_ref, bd_ref, sl_ref, le_ref, eye_ref, y_ref, vo_ref, h_ref):
    seq = pc_ref.shape[1]
    h_ref[...] = jnp.zeros_like(h_ref)

    def body(c, carry):
        t0 = pl.multiple_of(c * CHUNK, CHUNK)
        cs = pc_ref[0, pl.ds(t0, CHUNK), 0:C_SHIFT_PAD]
        tp = pl.multiple_of(jnp.maximum(t0 - 8, 0), 8)
        prev8 = pc_ref[0, pl.ds(tp, 8), 0:C_SHIFT_PAD]
        prev_row = jnp.where(c > 0, prev8[7:8, :], 0.0)
        g = pc_ref[0, pl.ds(t0, CHUNK), C_SHIFT_PAD:GC_WIDTH]
        vfirst = vf_ref[0, pl.ds(t0, CHUNK), :]
        out, v_own, h_out = _rwkv_chunk(
            cs, prev_row, g, vfirst, h_ref[...], mu_ref[...], vec_ref[...], w2_ref[...], a2_ref[...],
            v1_ref[...], v2_ref[...], ones_ref[...], tril_ref[...], bd_ref[...], sl_ref[...],
            le_ref[...], eye_ref[...], use_vres)
        h_ref[...] = h_out
        y_ref[0, pl.ds(t0, CHUNK), :] = out.astype(BF16)
        vo_ref[0, pl.ds(t0, CHUNK), :] = v_own
        return carry

    lax.fori_loop(0, seq // CHUNK, body, 0)


def _rwkv_consts():
    idx = jnp.arange(STACK)
    same = (idx[:, None] // CHUNK) == (idx[None, :] // CHUNK)
    tl = idx[:, None] % CHUNK
    sl_ = idx[None, :] % CHUNK
    bd = same.astype(F32)
    sl = (same & (sl_ < tl)).astype(F32)
    le = (same & (sl_ <= tl)).astype(F32)
    eye = jnp.eye(STACK, dtype=F32)
    ones_bd = same.astype(BF16)
    ci = jnp.arange(CHUNK)
    tril_l = (ci[None, :] <= ci[:, None]).astype(BF16)
    return ones_bd, tril_l, bd, sl, le, eye


def _rwkv(pc, vfirst, mu_l, vec_l, w2_l, a2_l, v1_l, v2_l, consts, use_vres):
    bsz, seq, _ = pc.shape
    full = lambda shape: pl.BlockSpec(shape, lambda b: (0,) * len(shape))
    sq = (STACK, STACK)
    return pl.pallas_call(
        functools.partial(_rwkv_kernel, use_vres),
        grid=(bsz,),
        in_specs=[pl.BlockSpec((1, seq, GC_WIDTH), lambda b: (b, 0, 0)),
                  pl.BlockSpec((1, seq, C_WIDTH), lambda b: (b, 0, 0)),
                  full((1, C_SHIFT_PAD)), full((8, C_WIDTH)),
                  full((LANES, C_WIDTH)), full((LANES, C_WIDTH)),
                  full((C_WIDTH, LANES)), full((LANES, C_WIDTH)),
                  full(sq), full((CHUNK, CHUNK)), full(sq), full(sq), full(sq), full(sq)],
        out_specs=[pl.BlockSpec((1, seq, C_WIDTH), lambda b: (b, 0, 0)),
                   pl.BlockSpec((1, seq, C_WIDTH), lambda b: (b, 0, 0))],
        out_shape=[jax.ShapeDtypeStruct((bsz, seq, C_WIDTH), BF16),
                   jax.ShapeDtypeStruct((bsz, seq, C_WIDTH), F32)],
        scratch_shapes=[pltpu.VMEM(sq, F32)],
        compiler_params=pltpu.CompilerParams(dimension_semantics=("arbitrary",),
                                             vmem_limit_bytes=VMEM_LIMIT),
        name="rwkv",
    )(pc, vfirst, mu_l, vec_l, w2_l, a2_l, v1_l, v2_l, *consts)


def _out_kernel(alpha, ya_ref, yb_ref, yc_ref, x_ref, w_ref, g_ref, b_ref, o_ref):
    acc = jnp.dot(ya_ref[...], w_ref[0:A_WIDTH, :], preferred_element_type=F32)
    acc = acc + jnp.dot(yb_ref[...], w_ref[A_WIDTH:A_WIDTH + B_WIDTH, :], preferred_element_type=F32)
    acc = acc + jnp.dot(yc_ref[...], w_ref[A_WIDTH + B_WIDTH:D_MIX, :], preferred_element_type=F32)
    z = alpha * x_ref[...] + acc
    mu = jnp.mean(z, axis=-1, keepdims=True)
    zc = z - mu
    var = jnp.mean(zc * zc, axis=-1, keepdims=True)
    o_ref[...] = zc * lax.rsqrt(var + LN_EPS) * g_ref[...] + b_ref[...]


def _out(ya, yb, yc, x2, w_l, g_l, b_l, alpha):
    n = x2.shape[0]
    row = lambda w: pl.BlockSpec((OUT_TM, w), lambda i: (i, 0))
    full = lambda shape: pl.BlockSpec(shape, lambda i: (0,) * len(shape))
    return pl.pallas_call(
        functools.partial(_out_kernel, alpha),
        grid=(n // OUT_TM,),
        in_specs=[row(A_WIDTH), row(B_WIDTH), row(C_WIDTH), row(D_MODEL),
                  full((D_MIX, D_MODEL)), full((1, D_MODEL)), full((1, D_MODEL))],
        out_specs=row(D_MODEL),
        out_shape=jax.ShapeDtypeStruct((n, D_MODEL), F32),
        compiler_params=pltpu.CompilerParams(dimension_semantics=("arbitrary",),
                                             vmem_limit_bytes=VMEM_LIMIT),
        name="outproj",
    )(ya, yb, yc, x2, w_l, g_l, b_l)


def _block_diag(w):
    nblk, d, _ = w.shape
    out = jnp.zeros((nblk * d, nblk * d), w.dtype)
    for i in range(nblk):
        out = out.at[i * d:(i + 1) * d, i * d:(i + 1) * d].set(w[i])
    return out


def kernel(x, w_in, w_out, ln_g, ln_b, attn_sinks, conv_w, conv_b, lru_wa, lru_ba, lru_wx, lru_bx,
           lru_lambda, rwkv_mu, rwkv_w0, rwkv_w2, rwkv_a0, rwkv_a2, rwkv_kk, rwkv_ka, rwkv_rk,
           rwkv_gn_w, rwkv_gn_b, rwkv_v0, rwkv_v1, rwkv_v2):
    bsz, seq, dm = x.shape
    depth = w_in.shape[0]
    alpha = (2 * depth) ** 0.25
    n = bsz * seq
    c_off = GA_WIDTH + GB_WIDTH
    pad_w = C_SHIFT_PAD - C_SHIFT_WIDTH
    consts = _rwkv_consts()

    x2 = x.reshape(n, dm)
    v_first = None
    for l in range(depth):
        wl = w_in[l]
        w_l = jnp.concatenate([wl[:, :c_off + C_SHIFT_WIDTH], jnp.zeros((dm, pad_w), wl.dtype),
                               wl[:, c_off + C_SHIFT_WIDTH:]], axis=1).astype(BF16)
        pa, pb, pc = _proj(x2, w_l)
        pa = pa.reshape(bsz, seq, GA_WIDTH)
        pb = pb.reshape(bsz, seq, GB_WIDTH)
        pc = pc.reshape(bsz, seq, GC_WIDTH)

        ya = _attn(pa, attn_sinks[l])

        lru_vec = jnp.stack([conv_b[l], lru_ba[l], lru_bx[l], lru_lambda[l]], axis=0)
        yb = _lru(pb, conv_w[l], lru_vec, _block_diag(lru_wa[l]).astype(BF16),
                  _block_diag(lru_wx[l]).astype(BF16))

        mu_l = jnp.concatenate([rwkv_mu[l], jnp.zeros((pad_w,), F32)])[None, :]
        use_vres = l > 0
        v0 = rwkv_v0[l - 1] if use_vres else jnp.zeros((C_WIDTH,), F32)
        vec_l = jnp.stack([rwkv_w0[l], rwkv_a0[l], rwkv_kk[l], rwkv_ka[l], rwkv_rk[l].reshape(-1),
                           rwkv_gn_w[l], rwkv_gn_b[l], v0], axis=0)
        w2_l = jnp.zeros((LANES, C_WIDTH), F32).at[0:DECAY_RANK].set(rwkv_w2[l]).astype(BF16)
        a2_l = jnp.zeros((LANES, C_WIDTH), F32).at[DECAY_RANK:DECAY_RANK + AICL_RANK].set(rwkv_a2[l]).astype(BF16)
        if use_vres:
            v1_l = jnp.zeros((C_WIDTH, LANES), F32).at[:, 0:VRES_RANK].set(rwkv_v1[l - 1]).astype(BF16)
            v2_l = jnp.zeros((LANES, C_WIDTH), F32).at[0:VRES_RANK].set(rwkv_v2[l - 1]).astype(BF16)
            vf = v_first
        else:
            v1_l = jnp.zeros((C_WIDTH, LANES), BF16)
            v2_l = jnp.zeros((LANES, C_WIDTH), BF16)
            vf = jnp.zeros((bsz, seq, C_WIDTH), F32)
        yc, v_own = _rwkv(pc, vf, mu_l, vec_l, w2_l, a2_l, v1_l, v2_l, consts, use_vres)
        if l == 0:
            v_first = v_own

        x2 = _out(ya.reshape(n, A_WIDTH), yb.reshape(n, B_WIDTH), yc.reshape(n, C_WIDTH), x2,
                  w_out[l].astype(BF16), ln_g[l][None, :], ln_b[l][None, :], alpha)
    return x2.reshape(bsz, seq, dm)
```

```python
import functools
import math

import jax
import jax.numpy as jnp
from jax import lax
from jax.experimental import pallas as pl
from jax.experimental.pallas import tpu as pltpu

F32 = jnp.float32
BF16 = jnp.bfloat16

D_MODEL = 1024
HEAD_DIM = 64
A_Q_HEADS = 8
A_KV_HEADS = 2
A_WIDTH = A_Q_HEADS * HEAD_DIM
A_KV_WIDTH = A_KV_HEADS * HEAD_DIM
ATT_BLOCK = 128
B_WIDTH = 256
B_BLOCKS = 4
CONV_WIDTH = 4
LRU_C = 8.0
C_HEADS = 4
C_WIDTH = C_HEADS * HEAD_DIM
DECAY_RANK = 32
AICL_RANK = 32
VRES_RANK = 16
C_SHIFT_WIDTH = 3 * C_WIDTH + DECAY_RANK + AICL_RANK
GN_EPS = 64e-5
LN_EPS = 1e-5

LANES = 128
C_SHIFT_PAD = ((C_SHIFT_WIDTH + LANES - 1) // LANES) * LANES
GA_WIDTH = A_WIDTH + 2 * A_KV_WIDTH + A_WIDTH
GB_WIDTH = 2 * B_WIDTH
GC_WIDTH = C_SHIFT_PAD + C_WIDTH
D_IN_PAD = GA_WIDTH + GB_WIDTH + GC_WIDTH
D_MIX = A_WIDTH + B_WIDTH + C_WIDTH

CHUNK = 64
STACK = C_HEADS * CHUNK
RWKV_GROUP = 4
LRU_TILE = 256
PROJ_TM = 512
OUT_TM = 512
VMEM_LIMIT = 56 * 1024 * 1024


def _mm(a, b):
    return jnp.dot(a.astype(BF16), b.astype(BF16), preferred_element_type=F32)


def _mm_nt(a, b):
    return lax.dot_general(a.astype(BF16), b.astype(BF16), (((1,), (1,)), ((), ())),
                           preferred_element_type=F32)


def _mm_tn(a, b):
    return lax.dot_general(a.astype(BF16), b.astype(BF16), (((0,), (0,)), ((), ())),
                           preferred_element_type=F32)


def _split_bf16(x, parts):
    out = []
    rem = x
    for _ in range(parts):
        hi = rem.astype(BF16)
        out.append(hi)
        rem = rem - hi.astype(F32)
    return out


def _mm_exact_rhs(x, m_bf16, parts):
    acc = None
    for p in _split_bf16(x, parts):
        t = jnp.dot(p, m_bf16, preferred_element_type=F32)
        acc = t if acc is None else acc + t
    return acc


def _mm_exact_lhs(m_bf16, x, parts):
    acc = None
    for p in _split_bf16(x, parts):
        t = jnp.dot(m_bf16, p, preferred_element_type=F32)
        acc = t if acc is None else acc + t
    return acc


def _sigmoid(x):
    return 1.0 / (1.0 + jnp.exp(-x))


def _silu(x):
    return x * _sigmoid(x)


def _softplus(x):
    return jnp.maximum(x, 0.0) + jnp.log(1.0 + jnp.exp(-jnp.abs(x)))


def _proj_kernel(x_ref, w_ref, oa_ref, ob_ref, oc_ref):
    x = x_ref[...].astype(BF16)
    oa_ref[...] = jnp.dot(x, w_ref[:, 0:GA_WIDTH], preferred_element_type=F32)
    ob_ref[...] = jnp.dot(x, w_ref[:, GA_WIDTH:GA_WIDTH + GB_WIDTH], preferred_element_type=F32)
    oc_ref[...] = jnp.dot(x, w_ref[:, GA_WIDTH + GB_WIDTH:D_IN_PAD], preferred_element_type=F32)


def _proj(x2, w_l):
    n = x2.shape[0]
    return pl.pallas_call(
        _proj_kernel,
        grid=(n // PROJ_TM,),
        in_specs=[pl.BlockSpec((PROJ_TM, D_MODEL), lambda i: (i, 0)),
                  pl.BlockSpec((D_MODEL, D_IN_PAD), lambda i: (0, 0))],
        out_specs=[pl.BlockSpec((PROJ_TM, GA_WIDTH), lambda i: (i, 0)),
                   pl.BlockSpec((PROJ_TM, GB_WIDTH), lambda i: (i, 0)),
                   pl.BlockSpec((PROJ_TM, GC_WIDTH), lambda i: (i, 0))],
        out_shape=[jax.ShapeDtypeStruct((n, GA_WIDTH), F32),
                   jax.ShapeDtypeStruct((n, GB_WIDTH), F32),
                   jax.ShapeDtypeStruct((n, GC_WIDTH), F32)],
        compiler_params=pltpu.CompilerParams(dimension_semantics=("arbitrary",),
                                             vmem_limit_bytes=VMEM_LIMIT),
        name="proj",
    )(x2, w_l)


def _attn_kernel(sink_ref, cur_ref, prev_ref, o_ref):
    n = pl.program_id(1)
    blk = ATT_BLOCK
    k_band = jnp.concatenate([prev_ref[0, :, 0:A_KV_WIDTH], cur_ref[0, :, A_WIDTH:A_WIDTH + A_KV_WIDTH]], axis=0)
    v_band = jnp.concatenate([prev_ref[0, :, A_KV_WIDTH:2 * A_KV_WIDTH],
                              cur_ref[0, :, A_WIDTH + A_KV_WIDTH:A_WIDTH + 2 * A_KV_WIDTH]], axis=0)
    lane = lax.broadcasted_iota(jnp.int32, (2 * blk, A_KV_WIDTH), 1)
    lo = lane < HEAD_DIM
    k_sw = pltpu.roll(k_band, HEAD_DIM, axis=1)
    v_sw = pltpu.roll(v_band, HEAD_DIM, axis=1)
    k_var = ((jnp.where(lo, k_band, 0.0).astype(BF16), jnp.where(lo, 0.0, k_sw).astype(BF16)),
             (jnp.where(lo, k_sw, 0.0).astype(BF16), jnp.where(lo, 0.0, k_band).astype(BF16)))
    v_var = ((jnp.where(lo, v_band, 0.0).astype(BF16), jnp.where(lo, 0.0, v_sw).astype(BF16)),
             (jnp.where(lo, v_sw, 0.0).astype(BF16), jnp.where(lo, 0.0, v_band).astype(BF16)))
    qi = lax.broadcasted_iota(jnp.int32, (blk, 2 * blk), 0)
    kj = lax.broadcasted_iota(jnp.int32, (blk, 2 * blk), 1)
    diff = qi + blk - kj
    key_pos = kj + (n - 1) * blk
    mask = (diff >= 0) & (diff < blk) & (key_pos >= 0)
    scale = HEAD_DIM ** -0.5
    grp = A_Q_HEADS // A_KV_HEADS
    for p in range(A_Q_HEADS // 2):
        h = (2 * p) // grp
        qp = cur_ref[0, :, p * LANES:(p + 1) * LANES].astype(BF16)
        acc = None
        for half in range(2):
            sink = sink_ref[2 * p + half]
            s = lax.dot_general(qp, k_var[h][half], (((1,), (1,)), ((), ())),
                                preferred_element_type=F32) * scale
            s = jnp.where(mask, s, -jnp.inf)
            m = jnp.maximum(jnp.max(s, axis=-1, keepdims=True), sink)
            e = jnp.exp(s - m)
            den = jnp.sum(e, axis=-1, keepdims=True) + jnp.exp(sink - m)
            o = jnp.dot(e.astype(BF16), v_var[h][half], preferred_element_type=F32) * (1.0 / den)
            acc = o if acc is None else acc + o
        g = cur_ref[0, :, A_WIDTH + 2 * A_KV_WIDTH + p * LANES:A_WIDTH + 2 * A_KV_WIDTH + (p + 1) * LANES]
        o_ref[0, :, p * LANES:(p + 1) * LANES] = (acc * _silu(g)).astype(BF16)


def _attn(pa, sinks_l):
    bsz, seq, _ = pa.shape
    nb = seq // ATT_BLOCK
    kv_blk = (A_WIDTH) // (2 * A_KV_WIDTH)
    return pl.pallas_call(
        _attn_kernel,
        grid=(bsz, nb),
        in_specs=[pl.BlockSpec(memory_space=pltpu.SMEM),
                  pl.BlockSpec((1, ATT_BLOCK, GA_WIDTH), lambda b, n: (b, n, 0)),
                  pl.BlockSpec((1, ATT_BLOCK, 2 * A_KV_WIDTH),
                               lambda b, n: (b, jnp.maximum(n - 1, 0), kv_blk))],
        out_specs=pl.BlockSpec((1, ATT_BLOCK, A_WIDTH), lambda b, n: (b, n, 0)),
        out_shape=jax.ShapeDtypeStruct((bsz, seq, A_WIDTH), BF16),
        compiler_params=pltpu.CompilerParams(dimension_semantics=("arbitrary", "arbitrary"),
                                             vmem_limit_bytes=VMEM_LIMIT),
        name="attn",
    )(sinks_l, pa, pa)


def _shift_rows(x, d, fill):
    n = x.shape[0]
    if d % 8 == 0:
        return jnp.concatenate([jnp.full((d, x.shape[1]), fill, x.dtype), x[:n - d]], axis=0)
    row = lax.broadcasted_iota(jnp.int32, x.shape, 0)
    return jnp.where(row < d, fill, pltpu.roll(x, d, axis=0))


def _lru_kernel(pb_ref, cw_ref, vec_ref, wa_ref, wx_ref, o_ref):
    seq = pb_ref.shape[1]
    tile = LRU_TILE
    conv_b = vec_ref[0:1, :]
    ba = vec_ref[1:2, :]
    bx = vec_ref[2:3, :]
    lam = vec_ref[3:4, :]
    neg_c_sp = -LRU_C * _softplus(-lam)

    def body(i, hc):
        t0 = pl.multiple_of(i * tile, tile)
        x = pb_ref[0, pl.ds(t0, tile), 0:B_WIDTH]
        tp = pl.multiple_of(jnp.maximum(t0 - 8, 0), 8)
        xp = pb_ref[0, pl.ds(tp, 8), 0:B_WIDTH]
        xp = jnp.where(i > 0, xp, 0.0)
        xe = jnp.concatenate([xp, x], axis=0)
        xc = cw_ref[CONV_WIDTH - 1:CONV_WIDTH, :] * x + conv_b
        for j in range(1, CONV_WIDTH):
            xs = pltpu.roll(xe, j, axis=0)[8:8 + tile]
            xc = xc + cw_ref[CONV_WIDTH - 1 - j:CONV_WIDTH - j, :] * xs
        r = _sigmoid(_mm(xc, wa_ref[...]) + ba)
        ig = _sigmoid(_mm(xc, wx_ref[...]) + bx)
        log_a = neg_c_sp * r
        a = jnp.exp(log_a)
        u = jnp.sqrt(1.0 - a * a) * (ig * xc)
        d = 1
        while d < tile:
            u = a * _shift_rows(u, d, 0.0) + u
            a = a * _shift_rows(a, d, 1.0)
            d *= 2
        h = u + a * hc
        g = pb_ref[0, pl.ds(t0, tile), B_WIDTH:2 * B_WIDTH]
        o_ref[0, pl.ds(t0, tile), :] = (h * _silu(g)).astype(BF16)
        return h[tile - 1:tile, :]

    lax.fori_loop(0, seq // tile, body, jnp.zeros((1, B_WIDTH), F32))


def _lru(pb, cw_l, vec_l, wa_l, wx_l):
    bsz, seq, _ = pb.shape
    full = lambda shape: pl.BlockSpec(shape, lambda b: (0,) * len(shape))
    return pl.pallas_call(
        _lru_kernel,
        grid=(bsz,),
        in_specs=[pl.BlockSpec((1, seq, GB_WIDTH), lambda b: (b, 0, 0)),
                  full((CONV_WIDTH, B_WIDTH)), full((4, B_WIDTH)),
                  full((B_WIDTH, B_WIDTH)), full((B_WIDTH, B_WIDTH))],
        out_specs=pl.BlockSpec((1, seq, B_WIDTH), lambda b: (b, 0, 0)),
        out_shape=jax.ShapeDtypeStruct((bsz, seq, B_WIDTH), BF16),
        compiler_params=pltpu.CompilerParams(dimension_semantics=("arbitrary",),
                                             vmem_limit_bytes=VMEM_LIMIT),
        name="lru",
    )(pb, cw_l, vec_l, wa_l, wx_l)


def _stack_heads(x, bd):
    return jnp.where(bd, jnp.concatenate([x] * C_HEADS, axis=0), 0.0).astype(BF16)


def _unstack_heads(x, bd):
    x = jnp.where(bd, x, 0.0)
    out = x[0:CHUNK]
    for h in range(1, C_HEADS):
        out = out + x[h * CHUNK:(h + 1) * CHUNK]
    return out


def _rwkv_pre(cs, prev_row, vfirst, mu, vec, w2p, a2p, v1p, v2p, ones_bd, tril_l, bd, sl, le, eye_w,
              use_vres):
    w0, a0, k_k, k_a, r_k, _, _, v0 = (vec[i:i + 1, :] for i in range(8))
    row = lax.broadcasted_iota(jnp.int32, cs.shape, 0)
    prev = jnp.where(row == 0, prev_row, pltpu.roll(cs, 1, axis=0))
    xs = cs + (prev - cs) * mu
    r = xs[:, 0:C_WIDTH]
    k = xs[:, C_WIDTH:2 * C_WIDTH]
    v = xs[:, 2 * C_WIDTH:3 * C_WIDTH]
    la = xs[:, 3 * C_WIDTH:C_SHIFT_PAD]
    yield
    logw =-_softplus(-(w0 + _mm(jnp.tanh(la), w2p))) - 0.5
    lw = -jnp.exp(logw)
    a = _sigmoid(a0 + _mm(la, a2p))
    v_own = v
    if use_vres:
        v = v + (vfirst - v) * _sigmoid(v0 + _mm(_mm(v, v1p), v2p))
    yield
    kk = k * k_k
    kap = kk * lax.rsqrt(_mm_exact_rhs(kk * kk, ones_bd, 2) + 1e-12)
    k2 = k * (1.0 + (a - 1.0) * k_a)
    b = kap * a

    yield
    cum = _mm_exact_lhs(tril_l, lw, 3)
    cum_end = cum[CHUNK - 1:CHUNK, :]
    e_in = jnp.exp(cum)
    e_ex = jnp.exp(cum - lw)
    e_neg = jnp.exp(-cum)
    e_end = jnp.exp(cum_end - cum)
    p_end = jnp.exp(cum_end)

    yield
    kt = kap * e_ex
    rt = r * e_in
    kt_st = _stack_heads(kt, bd)
    vs_st = _stack_heads(v, bd)
    gram = _mm_nt(jnp.concatenate([kt, rt], axis=0),
                  jnp.concatenate([_stack_heads(b * e_neg, bd), _stack_heads(k2 * e_neg, bd)], axis=0))
    a_ab = jnp.where(sl, gram[0:CHUNK, 0:STACK], 0.0)
    a_ak = jnp.where(sl, gram[0:CHUNK, STACK:2 * STACK], 0.0)
    a_rb = jnp.where(le, gram[CHUNK:2 * CHUNK, 0:STACK], 0.0)
    a_rk = jnp.where(le, gram[CHUNK:2 * CHUNK, STACK:2 * STACK], 0.0)

    yield
    pw = -a_ab
    t_w = eye_w + pw
    av = _mm(jnp.concatenate([a_ak, a_rk], axis=0), vs_st)
    pw = _mm(pw, _stack_heads(pw, bd))
    for _ in range(1, int(math.log2(CHUNK)) - 1):
        yield
        both = _mm(jnp.concatenate([t_w, pw], axis=0), _stack_heads(pw, bd))
        t_w = t_w + both[0:CHUNK]
        pw = both[CHUNK:2 * CHUNK]
    yield
    t_w = t_w + _mm(t_w, _stack_heads(pw, bd))
    yield
    tx = _mm(t_w, jnp.concatenate([kt_st, _stack_heads(av[0:CHUNK], bd)], axis=1))
    yield
    m1 = tx[:, 0:STACK]
    c1 = tx[:, STACK:2 * STACK]
    ab = _mm(a_rb, jnp.concatenate([_stack_heads(m1, bd), _stack_heads(c1, bd)], axis=1))
    q_n = rt - ab[:, 0:STACK]
    d_n = av[CHUNK:2 * CHUNK] - ab[:, STACK:2 * STACK]
    bh = b * e_end
    g_w = eye_w * p_end - _unstack_heads(_mm_tn(bh, m1), bd)
    c_w = _unstack_heads(_mm_tn(jnp.concatenate([k2 * e_end, bh], axis=0),
                                jnp.concatenate([v, -c1], axis=0)), bd)
    bonus_arg = r * k2 * r_k
    return q_n, d_n, g_w, c_w, bonus_arg, v, v_own


def _rwkv_post(y, bonus_arg, v, g, vec, ones_bd):
    gn_w = vec[5:6, :]
    gn_b = vec[6:7, :]
    inv_n = 1.0 / HEAD_DIM
    sums = _mm_exact_rhs(jnp.concatenate([y, bonus_arg], axis=0), ones_bd, 2)
    yc = y - sums[0:CHUNK] * inv_n
    yv = _mm_exact_rhs(yc * yc, ones_bd, 2) * inv_n
    yn = yc * lax.rsqrt(yv + GN_EPS) * gn_w + gn_b
    return (yn + sums[CHUNK:2 * CHUNK] * v) * _silu(g)


def _rwkv_kernel(use_vres, pc_ref, vf_ref, mu_ref, vec_ref, w2_ref, a2_ref, v1_ref, v2_ref,
                 ones_ref, tril_ref, bd_ref, sl_ref, le_ref, eye_ref, y_ref, vo_ref, h_ref):
    seq = pc_ref.shape[1]
    h_ref[...] = jnp.zeros_like(h_ref)

    def body(i, carry):
        bd = bd_ref[...] > 0.0
        sl = sl_ref[...] > 0.0
        le = le_ref[...] > 0.0
        gens = []
        prev_row = None
        for j in range(RWKV_GROUP):
            t0 = pl.multiple_of((i * RWKV_GROUP + j) * CHUNK, CHUNK)
            cs = pc_ref[0, pl.ds(t0, CHUNK), 0:C_SHIFT_PAD]
            if j == 0:
                tp = pl.multiple_of(jnp.maximum(t0 - 8, 0), 8)
                prev8 = pc_ref[0, pl.ds(tp, 8), 0:C_SHIFT_PAD]
                prev_row = jnp.where(i > 0, prev8[7:8, :], 0.0)
            vfirst = vf_ref[0, pl.ds(t0, CHUNK), :]
            gens.append(_rwkv_pre(cs, prev_row, vfirst, mu_ref[...], vec_ref[...], w2_ref[...],
                                  a2_ref[...], v1_ref[...], v2_ref[...], ones_ref[...], tril_ref[...],
                                  bd, sl, le, eye_ref[...], use_vres))
            prev_row = cs[CHUNK - 1:CHUNK, :]
        pre = [None] * RWKV_GROUP
        while any(p is None for p in pre):
            for j, gen in enumerate(gens):
                if pre[j] is None:
                    try:
                        next(gen)
                    except StopIteration as stop:
                        pre[j] = stop.value
        h_w = h_ref[...]
        for j in range(RWKV_GROUP):
            t0 = pl.multiple_of((i * RWKV_GROUP + j) * CHUNK, CHUNK)
            q_n, d_n, g_w, c_w, bonus_arg, v, v_own = pre[j]
            res = _mm(jnp.concatenate([q_n, g_w], axis=0), _stack_heads(h_w, bd))
            y = res[0:CHUNK] + d_n
            h_w = res[CHUNK:2 * CHUNK] + c_w
            g = pc_ref[0, pl.ds(t0, CHUNK), C_SHIFT_PAD:GC_WIDTH]
            out = _rwkv_post(y, bonus_arg, v, g, vec_ref[...], ones_ref[...])
            y_ref[0, pl.ds(t0, CHUNK), :] = out.astype(BF16)
            vo_ref[0, pl.ds(t0, CHUNK), :] = v_own
        h_ref[...] = h_w
        return carry

    lax.fori_loop(0, seq // (CHUNK * RWKV_GROUP), body, 0)


def _rwkv_consts():
    idx = jnp.arange(STACK)
    same = (idx[:, None] // CHUNK) == (idx[None, :] // CHUNK)
    bd = same.astype(F32)
    ones_bd = same.astype(BF16)
    ci = jnp.arange(CHUNK)
    tril_l = (ci[None, :] <= ci[:, None]).astype(BF16)
    s_loc = idx[None, :] % CHUNK
    sl = (s_loc < ci[:, None]).astype(F32)
    le = (s_loc <= ci[:, None]).astype(F32)
    eye_w = (s_loc == ci[:, None]).astype(F32)
    return ones_bd, tril_l, bd, sl, le, eye_w


def _rwkv(pc, vfirst, mu_l, vec_l, w2_l, a2_l, v1_l, v2_l, consts, use_vres):
    bsz, seq, _ = pc.shape
    full = lambda shape: pl.BlockSpec(shape, lambda b: (0,) * len(shape))
    sq = (STACK, STACK)
    wide = (CHUNK, STACK)
    return pl.pallas_call(
        functools.partial(_rwkv_kernel, use_vres),
        grid=(bsz,),
        in_specs=[pl.BlockSpec((1, seq, GC_WIDTH), lambda b: (b, 0, 0)),
                  pl.BlockSpec((1, seq, C_WIDTH), lambda b: (b, 0, 0)),
                  full((1, C_SHIFT_PAD)), full((8, C_WIDTH)),
                  full((LANES, C_WIDTH)), full((LANES, C_WIDTH)),
                  full((C_WIDTH, LANES)), full((LANES, C_WIDTH)),
                  full(sq), full((CHUNK, CHUNK)), full(sq), full(wide), full(wide), full(wide)],
        out_specs=[pl.BlockSpec((1, seq, C_WIDTH), lambda b: (b, 0, 0)),
                   pl.BlockSpec((1, seq, C_WIDTH), lambda b: (b, 0, 0))],
        out_shape=[jax.ShapeDtypeStruct((bsz, seq, C_WIDTH), BF16),
                   jax.ShapeDtypeStruct((bsz, seq, C_WIDTH), F32)],
        scratch_shapes=[pltpu.VMEM(wide, F32)],
        compiler_params=pltpu.CompilerParams(dimension_semantics=("arbitrary",),
                                             vmem_limit_bytes=VMEM_LIMIT),
        name="rwkv",
    )(pc, vfirst, mu_l, vec_l, w2_l, a2_l, v1_l, v2_l, *consts)


def _out_kernel(alpha, ya_ref, yb_ref, yc_ref, x_ref, w_ref, g_ref, b_ref, o_ref):
    acc = jnp.dot(ya_ref[...], w_ref[0:A_WIDTH, :], preferred_element_type=F32)
    acc = acc + jnp.dot(yb_ref[...], w_ref[A_WIDTH:A_WIDTH + B_WIDTH, :], preferred_element_type=F32)
    acc = acc + jnp.dot(yc_ref[...], w_ref[A_WIDTH + B_WIDTH:D_MIX, :], preferred_element_type=F32)
    z = alpha * x_ref[...] + acc
    mu = jnp.mean(z, axis=-1, keepdims=True)
    zc = z - mu
    var = jnp.mean(zc * zc, axis=-1, keepdims=True)
    o_ref[...] = zc * lax.rsqrt(var + LN_EPS) * g_ref[...] + b_ref[...]


def _out(ya, yb, yc, x2, w_l, g_l, b_l, alpha):
    n = x2.shape[0]
    row = lambda w: pl.BlockSpec((OUT_TM, w), lambda i: (i, 0))
    full = lambda shape: pl.BlockSpec(shape, lambda i: (0,) * len(shape))
    return pl.pallas_call(
        functools.partial(_out_kernel, alpha),
        grid=(n // OUT_TM,),
        in_specs=[row(A_WIDTH), row(B_WIDTH), row(C_WIDTH), row(D_MODEL),
                  full((D_MIX, D_MODEL)), full((1, D_MODEL)), full((1, D_MODEL))],
        out_specs=row(D_MODEL),
        out_shape=jax.ShapeDtypeStruct((n, D_MODEL), F32),
        compiler_params=pltpu.CompilerParams(dimension_semantics=("arbitrary",),
                                             vmem_limit_bytes=VMEM_LIMIT),
        name="outproj",
    )(ya, yb, yc, x2, w_l, g_l, b_l)


def _block_diag(w):
    nblk, d, _ = w.shape
    out = jnp.zeros((nblk * d, nblk * d), w.dtype)
    for i in range(nblk):
        out = out.at[i * d:(i + 1) * d, i * d:(i + 1) * d].set(w[i])
    return out


def kernel(x, w_in, w_out, ln_g, ln_b, attn_sinks, conv_w, conv_b, lru_wa, lru_ba, lru_wx, lru_bx,
           lru_lambda, rwkv_mu, rwkv_w0, rwkv_w2, rwkv_a0, rwkv_a2, rwkv_kk, rwkv_ka, rwkv_rk,
           rwkv_gn_w, rwkv_gn_b, rwkv_v0, rwkv_v1, rwkv_v2):
    bsz, seq, dm = x.shape
    depth = w_in.shape[0]
    alpha = (2 * depth) ** 0.25
    n = bsz * seq
    c_off = GA_WIDTH + GB_WIDTH
    pad_w = C_SHIFT_PAD - C_SHIFT_WIDTH
    consts = _rwkv_consts()

    x2 = x.reshape(n, dm)
    v_first = None
    for l in range(depth):
        wl = w_in[l]
        w_l = jnp.concatenate([wl[:, :c_off + C_SHIFT_WIDTH], jnp.zeros((dm, pad_w), wl.dtype),
                               wl[:, c_off + C_SHIFT_WIDTH:]], axis=1).astype(BF16)
        pa, pb, pc = _proj(x2, w_l)
        pa = pa.reshape(bsz, seq, GA_WIDTH)
        pb = pb.reshape(bsz, seq, GB_WIDTH)
        pc = pc.reshape(bsz, seq, GC_WIDTH)

        ya = _attn(pa, attn_sinks[l])

        lru_vec = jnp.stack([conv_b[l], lru_ba[l], lru_bx[l], lru_lambda[l]], axis=0)
        yb = _lru(pb, conv_w[l], lru_vec, _block_diag(lru_wa[l]).astype(BF16),
                  _block_diag(lru_wx[l]).astype(BF16))

        mu_l = jnp.concatenate([rwkv_mu[l], jnp.zeros((pad_w,), F32)])[None, :]
        use_vres = l > 0
        v0 = rwkv_v0[l - 1] if use_vres else jnp.zeros((C_WIDTH,), F32)
        vec_l = jnp.stack([rwkv_w0[l], rwkv_a0[l], rwkv_kk[l], rwkv_ka[l], rwkv_rk[l].reshape(-1),
                           rwkv_gn_w[l], rwkv_gn_b[l], v0], axis=0)
        w2_l = jnp.zeros((LANES, C_WIDTH), F32).at[0:DECAY_RANK].set(rwkv_w2[l]).astype(BF16)
        a2_l = jnp.zeros((LANES, C_WIDTH), F32).at[DECAY_RANK:DECAY_RANK + AICL_RANK].set(rwkv_a2[l]).astype(BF16)
        if use_vres:
            v1_l = jnp.zeros((C_WIDTH, LANES), F32).at[:, 0:VRES_RANK].set(rwkv_v1[l - 1]).astype(BF16)
            v2_l = jnp.zeros((LANES, C_WIDTH), F32).at[0:VRES_RANK].set(rwkv_v2[l - 1]).astype(BF16)
            vf = v_first
        else:
            v1_l = jnp.zeros((C_WIDTH, LANES), BF16)
            v2_l = jnp.zeros((LANES, C_WIDTH), BF16)
            vf = jnp.zeros((bsz, seq, C_WIDTH), F32)
        yc, v_own = _rwkv(pc, vf, mu_l, vec_l, w2_l, a2_l, v1_l, v2_l, consts, use_vres)
        if l == 0:
            v_first = v_own

        x2 = _out(ya.reshape(n, A_WIDTH), yb.reshape(n, B_WIDTH), yc.reshape(n, C_WIDTH), x2,
                  w_out[l].astype(BF16), ln_g[l][None, :], ln_b[l][None, :], alpha)
    return x2.reshape(bsz, seq, dm)
```

```python
import functools
import math

import jax
import jax.numpy as jnp
from jax import lax
from jax.experimental import pallas as pl
from jax.experimental.pallas import tpu as pltpu

F32 = jnp.float32
BF16 = jnp.bfloat16

D_MODEL = 1024
HEAD_DIM = 64
A_Q_HEADS = 8
A_KV_HEADS = 2
A_WIDTH = A_Q_HEADS * HEAD_DIM
A_KV_WIDTH = A_KV_HEADS * HEAD_DIM
ATT_BLOCK = 128
B_WIDTH = 256
B_BLOCKS = 4
CONV_WIDTH = 4
LRU_C = 8.0
C_HEADS = 4
C_WIDTH = C_HEADS * HEAD_DIM
DECAY_RANK = 32
AICL_RANK = 32
VRES_RANK = 16
C_SHIFT_WIDTH = 3 * C_WIDTH + DECAY_RANK + AICL_RANK
GN_EPS = 64e-5
LN_EPS = 1e-5
LOG2E = math.log2(math.e)

LANES = 128
SUBLANES = 8
C_SHIFT_PAD = ((C_SHIFT_WIDTH + LANES - 1) // LANES) * LANES
GA_WIDTH = A_WIDTH + 2 * A_KV_WIDTH + A_WIDTH
GB_WIDTH = 2 * B_WIDTH
GC_WIDTH = C_SHIFT_PAD + C_WIDTH
D_IN_PAD = GA_WIDTH + GB_WIDTH + GC_WIDTH
D_MIX = A_WIDTH + B_WIDTH + C_WIDTH

CHUNK = 64
STACK = C_HEADS * CHUNK
RWKV_GROUP = 4
ATT_TILE = 512
LRU_TILE = 256
PROJ_TM = 512
OUT_TM = 512
VMEM_LIMIT = 56 * 1024 * 1024


def _mm(a, b):
    return jnp.dot(a.astype(BF16), b.astype(BF16), preferred_element_type=F32)


def _mm_nt(a, b):
    return lax.dot_general(a.astype(BF16), b.astype(BF16), (((1,), (1,)), ((), ())),
                           preferred_element_type=F32)


def _mm_tn(a, b):
    return lax.dot_general(a.astype(BF16), b.astype(BF16), (((0,), (0,)), ((), ())),
                           preferred_element_type=F32)


def _split_bf16(x, parts):
    out = []
    rem = x
    for _ in range(parts):
        hi = rem.astype(BF16)
        out.append(hi)
        rem = rem - hi.astype(F32)
    return out


def _mm_exact_rhs(x, m_bf16, parts):
    n = x.shape[0]
    t = jnp.dot(jnp.concatenate(_split_bf16(x, parts), axis=0), m_bf16, preferred_element_type=F32)
    acc = t[0:n]
    for p in range(1, parts):
        acc = acc + t[p * n:(p + 1) * n]
    return acc


def _mm_exact_lhs(m_bf16, x, parts):
    n = x.shape[1]
    t = jnp.dot(m_bf16, jnp.concatenate(_split_bf16(x, parts), axis=1), preferred_element_type=F32)
    acc = t[:, 0:n]
    for p in range(1, parts):
        acc = acc + t[:, p * n:(p + 1) * n]
    return acc


def _sigmoid(x):
    return 1.0 / (1.0 + jnp.exp(-x))


def _silu(x):
    return x * _sigmoid(x)


def _softplus(x):
    return jnp.maximum(x, 0.0) + jnp.log(1.0 + jnp.exp(-jnp.abs(x)))


def _layer_spec(shape, layer):
    return pl.BlockSpec((None,) + shape, lambda *_: (layer,) + (0,) * len(shape))


def _const_spec(shape):
    return pl.BlockSpec(shape, lambda *_: (0,) * len(shape))


def _proj_kernel(x_ref, w_ref, oa_ref, ob_ref, oc_ref):
    x = x_ref[...].astype(BF16)
    oa_ref[...] = jnp.dot(x, w_ref[:, 0:GA_WIDTH], preferred_element_type=F32).astype(BF16)
    ob_ref[...] = jnp.dot(x, w_ref[:, GA_WIDTH:GA_WIDTH + GB_WIDTH], preferred_element_type=F32)
    oc_ref[...] = jnp.dot(x, w_ref[:, GA_WIDTH + GB_WIDTH:D_IN_PAD], preferred_element_type=F32)


def _proj(x2, w_all, layer):
    n = x2.shape[0]
    return pl.pallas_call(
        _proj_kernel,
        grid=(n // PROJ_TM,),
        in_specs=[pl.BlockSpec((PROJ_TM, D_MODEL), lambda i: (i, 0)),
                  _layer_spec((D_MODEL, D_IN_PAD), layer)],
        out_specs=[pl.BlockSpec((PROJ_TM, GA_WIDTH), lambda i: (i, 0)),
                   pl.BlockSpec((PROJ_TM, GB_WIDTH), lambda i: (i, 0)),
                   pl.BlockSpec((PROJ_TM, GC_WIDTH), lambda i: (i, 0))],
        out_shape=[jax.ShapeDtypeStruct((n, GA_WIDTH), BF16),
                   jax.ShapeDtypeStruct((n, GB_WIDTH), F32),
                   jax.ShapeDtypeStruct((n, GC_WIDTH), F32)],
        compiler_params=pltpu.CompilerParams(dimension_semantics=("arbitrary",),
                                             vmem_limit_bytes=VMEM_LIMIT),
        name="proj",
    )(x2, w_all)


def _attn_kernel(layer, sink_ref, cur_ref, prev_ref, o_ref):
    n = pl.program_id(1)
    blk = ATT_BLOCK
    nsub = ATT_TILE // blk
    grp = A_Q_HEADS // A_KV_HEADS
    lo = lax.broadcasted_iota(jnp.int32, (blk, A_KV_WIDTH), 1) < HEAD_DIM

    def variants(x16):
        x = x16.astype(F32)
        xs = pltpu.roll(x, HEAD_DIM, axis=1)
        return ((jnp.where(lo, x, 0.0).astype(BF16), jnp.where(lo, 0.0, xs).astype(BF16)),
                (jnp.where(lo, xs, 0.0).astype(BF16), jnp.where(lo, 0.0, x).astype(BF16)))

    k0, v0 = A_WIDTH, A_WIDTH + A_KV_WIDTH
    g0 = A_WIDTH + 2 * A_KV_WIDTH
    kblk = [variants(prev_ref[0, :, 0:A_KV_WIDTH])]
    vblk = [variants(prev_ref[0, :, A_KV_WIDTH:2 * A_KV_WIDTH])]
    for j in range(nsub):
        kblk.append(variants(cur_ref[0, j * blk:(j + 1) * blk, k0:k0 + A_KV_WIDTH]))
        vblk.append(variants(cur_ref[0, j * blk:(j + 1) * blk, v0:v0 + A_KV_WIDTH]))

    qi = lax.broadcasted_iota(jnp.int32, (blk, 2 * blk), 0)
    kj = lax.broadcasted_iota(jnp.int32, (blk, 2 * blk), 1)
    diff = qi + blk - kj
    band = (diff >= 0) & (diff < blk)
    first = band & (kj + (n * nsub - 1) * blk >= 0)

    for j in range(nsub):
        mask = first if j == 0 else band
        rows = slice(j * blk, (j + 1) * blk)
        scores = []
        for p in range(A_Q_HEADS // 2):
            h = (2 * p) // grp
            qp = cur_ref[0, rows, p * LANES:(p + 1) * LANES]
            for half in range(2):
                kb = jnp.concatenate([kblk[j][h][half], kblk[j + 1][h][half]], axis=0)
                scores.append(lax.dot_general(qp, kb, (((1,), (1,)), ((), ())),
                                              preferred_element_type=F32))
        probs, inv = [], []
        for idx, s in enumerate(scores):
            sink2 = sink_ref[layer, idx] * LOG2E
            s = jnp.where(mask, s, -jnp.inf)
            m = jnp.maximum(jnp.max(s, axis=-1, keepdims=True), sink2)
            e = jnp.exp2(s - m)
            den = jnp.sum(e, axis=-1, keepdims=True) + jnp.exp2(sink2 - m)
            probs.append(e.astype(BF16))
            inv.append(1.0 / den)
        for p in range(A_Q_HEADS // 2):
            h = (2 * p) // grp
            acc = None
            for half in range(2):
                vb = jnp.concatenate([vblk[j][h][half], vblk[j + 1][h][half]], axis=0)
                o = jnp.dot(probs[2 * p + half], vb, preferred_element_type=F32) * inv[2 * p + half]
                acc = o if acc is None else acc + o
            g = cur_ref[0, rows, g0 + p * LANES:g0 + (p + 1) * LANES].astype(F32)
            o_ref[0, rows, p * LANES:(p + 1) * LANES] = (acc * _silu(g)).astype(BF16)


def _attn(pa, sinks, layer):
    bsz, seq, _ = pa.shape
    nsub = ATT_TILE // ATT_BLOCK
    kv_blk = A_WIDTH // (2 * A_KV_WIDTH)
    return pl.pallas_call(
        functools.partial(_attn_kernel, layer),
        grid=(bsz, seq // ATT_TILE),
        in_specs=[pl.BlockSpec(memory_space=pltpu.SMEM),
                  pl.BlockSpec((1, ATT_TILE, GA_WIDTH), lambda b, n: (b, n, 0)),
                  pl.BlockSpec((1, ATT_BLOCK, 2 * A_KV_WIDTH),
                               lambda b, n: (b, jnp.maximum(n * nsub - 1, 0), kv_blk))],
        out_specs=pl.BlockSpec((1, ATT_TILE, A_WIDTH), lambda b, n: (b, n, 0)),
        out_shape=jax.ShapeDtypeStruct((bsz, seq, A_WIDTH), BF16),
        compiler_params=pltpu.CompilerParams(dimension_semantics=("arbitrary", "arbitrary"),
                                             vmem_limit_bytes=VMEM_LIMIT),
        name="attn",
    )(sinks, pa, pa)


def _lru_kernel(pb_ref, cw_ref, vec_ref, wa_ref, wx_ref, o_ref):
    seq = pb_ref.shape[1]
    tile = LRU_TILE
    ngrp = tile // SUBLANES
    conv_b = vec_ref[0:1, :]
    ba = vec_ref[1:2, :]
    bx = vec_ref[2:3, :]
    lam = vec_ref[3:4, :]
    neg_c_sp = -LRU_C * _softplus(-lam)

    def body(i, hc):
        t0 = pl.multiple_of(i * tile, tile)
        x = pb_ref[0, pl.ds(t0, tile), 0:B_WIDTH]
        tp = pl.multiple_of(jnp.maximum(t0 - SUBLANES, 0), SUBLANES)
        xp = pb_ref[0, pl.ds(tp, SUBLANES), 0:B_WIDTH]
        xp = jnp.where(i > 0, xp, 0.0)
        xe = jnp.concatenate([xp, x], axis=0)
        xc = cw_ref[CONV_WIDTH - 1:CONV_WIDTH, :] * x + conv_b
        for j in range(1, CONV_WIDTH):
            xs = pltpu.roll(xe, j, axis=0)[SUBLANES:SUBLANES + tile]
            xc = xc + cw_ref[CONV_WIDTH - 1 - j:CONV_WIDTH - j, :] * xs
        r = _sigmoid(_mm(xc, wa_ref[...]) + ba)
        ig = _sigmoid(_mm(xc, wx_ref[...]) + bx)
        a = jnp.exp(neg_c_sp * r)
        u = jnp.sqrt(1.0 - a * a) * (ig * xc)
        a3 = a.reshape(ngrp, SUBLANES, B_WIDTH)
        u3 = u.reshape(ngrp, SUBLANES, B_WIDTH)
        sub = lax.broadcasted_iota(jnp.int32, a3.shape, 1)
        d = 1
        while d < SUBLANES:
            keep = sub >= d
            u3 = a3 * jnp.where(keep, pltpu.roll(u3, d, axis=1), 0.0) + u3
            a3 = a3 * jnp.where(keep, pltpu.roll(a3, d, axis=1), 1.0)
            d *= 2
        hs = []
        carry = hc
        for gi in range(ngrp):
            hg = u3[gi] + a3[gi] * carry
            hs.append(hg)
            carry = hg[SUBLANES - 1:SUBLANES, :]
        h = jnp.concatenate(hs, axis=0)
        g = pb_ref[0, pl.ds(t0, tile), B_WIDTH:2 * B_WIDTH]
        o_ref[0, pl.ds(t0, tile), :] = (h * _silu(g)).astype(BF16)
        return carry

    lax.fori_loop(0, seq // tile, body, jnp.zeros((1, B_WIDTH), F32))


def _lru(pb, cw_all, vec_all, wa_all, wx_all, layer):
    bsz, seq, _ = pb.shape
    return pl.pallas_call(
        _lru_kernel,
        grid=(bsz,),
        in_specs=[pl.BlockSpec((1, seq, GB_WIDTH), lambda b: (b, 0, 0)),
                  _layer_spec((CONV_WIDTH, B_WIDTH), layer), _layer_spec((4, B_WIDTH), layer),
                  _layer_spec((B_WIDTH, B_WIDTH), layer), _layer_spec((B_WIDTH, B_WIDTH), layer)],
        out_specs=pl.BlockSpec((1, seq, B_WIDTH), lambda b: (b, 0, 0)),
        out_shape=jax.ShapeDtypeStruct((bsz, seq, B_WIDTH), BF16),
        compiler_params=pltpu.CompilerParams(dimension_semantics=("arbitrary",),
                                             vmem_limit_bytes=VMEM_LIMIT),
        name="lru",
    )(pb, cw_all, vec_all, wa_all, wx_all)


def _stack_heads(x, bd16):
    return jnp.concatenate([x.astype(BF16)] * C_HEADS, axis=0) * bd16


def _unstack_heads(x, bd):
    x = x * bd
    out = x[0:CHUNK]
    for h in range(1, C_HEADS):
        out = out + x[h * CHUNK:(h + 1) * CHUNK]
    return out


def _rwkv_pre(cs, prev_row, vfirst, mu, vec, w2p, a2p, v1p, v2p, ones_bd, tril_l, bd, bd32, sl, le,
              eye_w):
    w0, a0, k_k, k_a, r_k, _, _, v0 = (vec[i:i + 1, :] for i in range(8))
    row = lax.broadcasted_iota(jnp.int32, cs.shape, 0)
    prev = jnp.where(row == 0, prev_row, pltpu.roll(cs, 1, axis=0))
    xs = cs + (prev - cs) * mu
    r = xs[:, 0:C_WIDTH]
    k = xs[:, C_WIDTH:2 * C_WIDTH]
    v = xs[:, 2 * C_WIDTH:3 * C_WIDTH]
    la = xs[:, 3 * C_WIDTH:C_SHIFT_PAD]
    yield
    logw = -_softplus(-(w0 + _mm(jnp.tanh(la), w2p))) - 0.5
    lw = -jnp.exp(logw)
    a = _sigmoid(a0 + _mm(la, a2p))
    v_own = v
    if vfirst is not None:
        v = v + (vfirst - v) * _sigmoid(v0 + _mm(_mm(v, v1p), v2p))
    yield
    kk = k * k_k
    kap = kk * lax.rsqrt(_mm_exact_rhs(kk * kk, ones_bd, 2) + 1e-12)
    k2 = k * (1.0 + (a - 1.0) * k_a)
    b = kap * a

    yield
    cum = _mm_exact_lhs(tril_l, lw, 3)
    cum_end = cum[CHUNK - 1:CHUNK, :]
    e_in = jnp.exp(cum)
    e_ex = jnp.exp(cum - lw)
    e_neg = jnp.exp(-cum)
    e_end = jnp.exp(cum_end - cum)
    p_end = jnp.exp(cum_end)

    yield
    kt = kap * e_ex
    rt = r * e_in
    kt_st = _stack_heads(kt, bd)
    vs_st = _stack_heads(v, bd)
    gram = _mm_nt(jnp.concatenate([kt, rt], axis=0),
                  jnp.concatenate([_stack_heads(b * e_neg, bd), _stack_heads(k2 * e_neg, bd)], axis=0))
    a_ab = jnp.where(sl, gram[0:CHUNK, 0:STACK], 0.0)
    a_ak = jnp.where(sl, gram[0:CHUNK, STACK:2 * STACK], 0.0)
    a_rb = jnp.where(le, gram[CHUNK:2 * CHUNK, 0:STACK], 0.0)
    a_rk = jnp.where(le, gram[CHUNK:2 * CHUNK, STACK:2 * STACK], 0.0)

    yield
    pw = -a_ab
    t_w = eye_w + pw
    av = _mm(jnp.concatenate([a_ak, a_rk], axis=0), vs_st)
    pw = _mm(pw, _stack_heads(pw, bd))
    for _ in range(1, int(math.log2(CHUNK)) - 1):
        yield
        both = _mm(jnp.concatenate([t_w, pw], axis=0), _stack_heads(pw, bd))
        t_w = t_w + both[0:CHUNK]
        pw = both[CHUNK:2 * CHUNK]
    yield
    t_w = t_w + _mm(t_w, _stack_heads(pw, bd))
    yield
    tx = _mm(t_w, jnp.concatenate([kt_st, _stack_heads(av[0:CHUNK], bd)], axis=1))
    yield
    m1 = tx[:, 0:STACK]
    c1 = tx[:, STACK:2 * STACK]
    ab = _mm(a_rb, jnp.concatenate([_stack_heads(m1, bd), _stack_heads(c1, bd)], axis=1))
    q_n = rt - ab[:, 0:STACK]
    d_n = av[CHUNK:2 * CHUNK] - ab[:, STACK:2 * STACK]
    bh = b * e_end
    g_w = eye_w * p_end - _unstack_heads(_mm_tn(bh, m1), bd32)
    c_w = _unstack_heads(_mm_tn(jnp.concatenate([k2 * e_end, bh], axis=0),
                                jnp.concatenate([v, -c1], axis=0)), bd32)
    bonus_arg = r * k2 * r_k
    return q_n, d_n, g_w, c_w, bonus_arg, v, v_own


RWKV_SAVED = 6


def _rwkv_kernel(use_vres, *refs):
    if use_vres:
        (pc_ref, vf_ref, mu_ref, vec_ref, w2_ref, a2_ref, v1_ref, v2_ref, tril_ref, bd_ref, bdf_ref,
         sl_ref, le_ref, eye_ref, y_ref, h_ref, pre_ref) = refs
        vo_ref = None
    else:
        (pc_ref, mu_ref, vec_ref, w2_ref, a2_ref, v1_ref, v2_ref, tril_ref, bd_ref, bdf_ref,
         sl_ref, le_ref, eye_ref, y_ref, vo_ref, h_ref, pre_ref) = refs
        vf_ref = None
    seq = pc_ref.shape[1]
    ngroups = seq // (CHUNK * RWKV_GROUP)
    h_ref[...] = jnp.zeros_like(h_ref)

    def chunk_start(gi, j):
        return pl.multiple_of((gi * RWKV_GROUP + j) * CHUNK, CHUNK)

    def pre_generators(gi, first_group):
        bd = bd_ref[...]
        sl = sl_ref[...] > 0.0
        le = le_ref[...] > 0.0
        gens = []
        prev_row = None
        for j in range(RWKV_GROUP):
            t0 = chunk_start(gi, j)
            cs = pc_ref[0, pl.ds(t0, CHUNK), 0:C_SHIFT_PAD]
            if j == 0:
                if first_group:
                    prev_row = jnp.zeros((1, C_SHIFT_PAD), F32)
                else:
                    tp = pl.multiple_of(t0 - SUBLANES, SUBLANES)
                    prev_row = pc_ref[0, pl.ds(tp, SUBLANES), 0:C_SHIFT_PAD][SUBLANES - 1:SUBLANES, :]
            vfirst = vf_ref[0, pl.ds(t0, CHUNK), :] if use_vres else None
            gens.append(_rwkv_pre(cs, prev_row, vfirst, mu_ref[...], vec_ref[...], w2_ref[...],
                                  a2_ref[...], v1_ref[...], v2_ref[...], bd, tril_ref[...],
                                  bd, bdf_ref[...], sl, le, eye_ref[...]))
            prev_row = cs[CHUNK - 1:CHUNK, :]
        return gens

    def chain_tasks(gi):
        ys = [None] * RWKV_GROUP
        stats = [None] * RWKV_GROUP
        state = {}

        def step(j):
            def run():
                h_w = h_ref[...] if j == 0 else state["h"]
                q_n, d_n, g_w, c_w = (pre_ref[j * RWKV_SAVED + k] for k in range(4))
                res = _mm(jnp.concatenate([q_n, g_w], axis=0), _stack_heads(h_w, bd_ref[...]))
                ys[j] = res[0:CHUNK] + d_n
                state["h"] = res[CHUNK:2 * CHUNK] + c_w
                if j == RWKV_GROUP - 1:
                    h_ref[...] = state["h"]
            return run

        def sums():
            for j in range(RWKV_GROUP):
                s = _mm_exact_rhs(jnp.concatenate([ys[j], pre_ref[j * RWKV_SAVED + 4]], axis=0),
                                  bd_ref[...], 2)
                yc = ys[j] - s[0:CHUNK] * (1.0 / HEAD_DIM)
                stats[j] = (yc, s[CHUNK:2 * CHUNK])

        def variance():
            for j in range(RWKV_GROUP):
                yc, bonus = stats[j]
                yv = _mm_exact_rhs(yc * yc, bd_ref[...], 2) * (1.0 / HEAD_DIM)
                stats[j] = (yc, bonus, yv)

        def finish():
            gn_w = vec_ref[5:6, :]
            gn_b = vec_ref[6:7, :]
            for j in range(RWKV_GROUP):
                t0 = chunk_start(gi, j)
                yc, bonus, yv = stats[j]
                yn = yc * lax.rsqrt(yv + GN_EPS) * gn_w + gn_b
                g = pc_ref[0, pl.ds(t0, CHUNK), C_SHIFT_PAD:GC_WIDTH]
                out = (yn + bonus * pre_ref[j * RWKV_SAVED + 5]) * _silu(g)
                y_ref[0, pl.ds(t0, CHUNK), :] = out.astype(BF16)

        return [step(j) for j in range(RWKV_GROUP)] + [sums, variance, finish]

    def run_group(gi, first_group, tasks):
        gens = pre_generators(gi, first_group)
        tasks = list(tasks)
        pre = [None] * RWKV_GROUP
        sweep = 0
        while any(p is None for p in pre):
            for j, gen in enumerate(gens):
                if pre[j] is None:
                    try:
                        next(gen)
                    except StopIteration as stop:
                        pre[j] = stop.value
            sweep += 1
            if tasks and sweep >= 2:
                tasks.pop(0)()
        for t in tasks:
            t()
        for j in range(RWKV_GROUP):
            for k in range(RWKV_SAVED):
                pre_ref[j * RWKV_SAVED + k] = pre[j][k]
            if not use_vres:
                vo_ref[0, pl.ds(chunk_start(gi, j), CHUNK), :] = pre[j][RWKV_SAVED]

    run_group(0, True, [])

    def body(i, carry):
        run_group(i, False, chain_tasks(i - 1))
        return carry

    lax.fori_loop(1, ngroups, body, 0)
    for t in chain_tasks(ngroups - 1):
        t()


def _rwkv_consts():
    idx = jnp.arange(STACK)
    same = (idx[:, None] // CHUNK) == (idx[None, :] // CHUNK)
    ci = jnp.arange(CHUNK)
    tril_l = (ci[None, :] <= ci[:, None]).astype(BF16)
    s_loc = idx[None, :] % CHUNK
    sl = (s_loc < ci[:, None]).astype(F32)
    le = (s_loc <= ci[:, None]).astype(F32)
    eye_w = (s_loc == ci[:, None]).astype(F32)
    return tril_l, same.astype(BF16), same.astype(F32), sl, le, eye_w


def _rwkv(pc, vfirst, mu_all, vec_all, w2_all, a2_all, v1_all, v2_all, consts, layer):
    bsz, seq, _ = pc.shape
    use_vres = vfirst is not None
    sq = (STACK, STACK)
    wide = (CHUNK, STACK)
    seq_spec = lambda w: pl.BlockSpec((1, seq, w), lambda b: (b, 0, 0))
    in_specs = [seq_spec(GC_WIDTH)] + ([seq_spec(C_WIDTH)] if use_vres else []) + [
        _layer_spec((1, C_SHIFT_PAD), layer), _layer_spec((8, C_WIDTH), layer),
        _layer_spec((LANES, C_WIDTH), layer), _layer_spec((LANES, C_WIDTH), layer),
        _layer_spec((C_WIDTH, LANES), layer), _layer_spec((LANES, C_WIDTH), layer),
        _const_spec((CHUNK, CHUNK)), _const_spec(sq), _const_spec(sq),
        _const_spec(wide), _const_spec(wide), _const_spec(wide)]
    out_specs = [seq_spec(C_WIDTH)]
    out_shape = [jax.ShapeDtypeStruct((bsz, seq, C_WIDTH), BF16)]
    if not use_vres:
        out_specs.append(seq_spec(C_WIDTH))
        out_shape.append(jax.ShapeDtypeStruct((bsz, seq, C_WIDTH), F32))
    args = (pc,) + ((vfirst,) if use_vres else ()) + (mu_all, vec_all, w2_all, a2_all, v1_all, v2_all)
    outs = pl.pallas_call(
        functools.partial(_rwkv_kernel, use_vres),
        grid=(bsz,),
        in_specs=in_specs,
        out_specs=out_specs,
        out_shape=out_shape,
        scratch_shapes=[pltpu.VMEM(wide, F32), pltpu.VMEM((RWKV_GROUP * RWKV_SAVED,) + wide, F32)],
        compiler_params=pltpu.CompilerParams(dimension_semantics=("arbitrary",),
                                             vmem_limit_bytes=VMEM_LIMIT),
        name="rwkv",
    )(*args, *consts)
    return (outs[0], None) if use_vres else (outs[0], outs[1])


def _out_kernel(alpha, ya_ref, yb_ref, yc_ref, x_ref, w_ref, ln_ref, o_ref):
    acc = jnp.dot(ya_ref[...], w_ref[0:A_WIDTH, :], preferred_element_type=F32)
    acc = acc + jnp.dot(yb_ref[...], w_ref[A_WIDTH:A_WIDTH + B_WIDTH, :], preferred_element_type=F32)
    acc = acc + jnp.dot(yc_ref[...], w_ref[A_WIDTH + B_WIDTH:D_MIX, :], preferred_element_type=F32)
    z = alpha * x_ref[...] + acc
    mu = jnp.mean(z, axis=-1, keepdims=True)
    zc = z - mu
    var = jnp.mean(zc * zc, axis=-1, keepdims=True)
    o_ref[...] = zc * lax.rsqrt(var + LN_EPS) * ln_ref[0:1, :] + ln_ref[1:2, :]


def _out(ya, yb, yc, x2, w_all, ln_all, alpha, layer):
    n = x2.shape[0]
    row = lambda w: pl.BlockSpec((OUT_TM, w), lambda i: (i, 0))
    return pl.pallas_call(
        functools.partial(_out_kernel, alpha),
        grid=(n // OUT_TM,),
        in_specs=[row(A_WIDTH), row(B_WIDTH), row(C_WIDTH), row(D_MODEL),
                  _layer_spec((D_MIX, D_MODEL), layer), _layer_spec((2, D_MODEL), layer)],
        out_specs=row(D_MODEL),
        out_shape=jax.ShapeDtypeStruct((n, D_MODEL), F32),
        compiler_params=pltpu.CompilerParams(dimension_semantics=("arbitrary",),
                                             vmem_limit_bytes=VMEM_LIMIT),
        name="outproj",
    )(ya, yb, yc, x2, w_all, ln_all)


def _block_diag_all(w):
    depth, nblk, d, _ = w.shape
    eye = jnp.eye(nblk, dtype=w.dtype)
    return (w[:, :, :, None, :] * eye[None, :, None, :, None]).reshape(depth, nblk * d, nblk * d)


def kernel(x, w_in, w_out, ln_g, ln_b, attn_sinks, conv_w, conv_b, lru_wa, lru_ba, lru_wx, lru_bx,
           lru_lambda, rwkv_mu, rwkv_w0, rwkv_w2, rwkv_a0, rwkv_a2, rwkv_kk, rwkv_ka, rwkv_rk,
           rwkv_gn_w, rwkv_gn_b, rwkv_v0, rwkv_v1, rwkv_v2):
    bsz, seq, dm = x.shape
    depth = w_in.shape[0]
    alpha = (2 * depth) ** 0.25
    n = bsz * seq
    c_end = GA_WIDTH + GB_WIDTH + C_SHIFT_WIDTH
    pad_w = C_SHIFT_PAD - C_SHIFT_WIDTH

    q_scale = jnp.concatenate([jnp.full((A_WIDTH,), HEAD_DIM ** -0.5 * LOG2E, F32),
                               jnp.ones((w_in.shape[2] - A_WIDTH,), F32)])
    w_sc = w_in * q_scale
    w_in_p = jnp.concatenate([w_sc[:, :, :c_end], jnp.zeros((depth, dm, pad_w), F32),
                              w_sc[:, :, c_end:]], axis=2).astype(BF16)
    w_out_b = w_out.astype(BF16)
    ln_all = jnp.stack([ln_g, ln_b], axis=1)
    lru_vec = jnp.stack([conv_b, lru_ba, lru_bx, lru_lambda], axis=1)
    wa_bd = _block_diag_all(lru_wa).astype(BF16)
    wx_bd = _block_diag_all(lru_wx).astype(BF16)
    mu_all = jnp.pad(rwkv_mu, ((0, 0), (0, pad_w)))[:, None, :]
    v0_all = jnp.pad(rwkv_v0, ((1, 0), (0, 0)))
    vec_all = jnp.stack([rwkv_w0, rwkv_a0, rwkv_kk, rwkv_ka, rwkv_rk.reshape(depth, C_WIDTH),
                         rwkv_gn_w, rwkv_gn_b, v0_all], axis=1)
    w2_all = jnp.pad(rwkv_w2, ((0, 0), (0, LANES - DECAY_RANK), (0, 0))).astype(BF16)
    a2_all = jnp.pad(rwkv_a2, ((0, 0), (DECAY_RANK, LANES - DECAY_RANK - AICL_RANK), (0, 0))).astype(BF16)
    v1_all = jnp.pad(rwkv_v1, ((1, 0), (0, 0), (0, LANES - VRES_RANK))).astype(BF16)
    v2_all = jnp.pad(rwkv_v2, ((1, 0), (0, LANES - VRES_RANK), (0, 0))).astype(BF16)
    consts = _rwkv_consts()

    x2 = x.reshape(n, dm)
    v_first = None
    for l in range(depth):
        pa, pb, pc = _proj(x2, w_in_p, l)
        ya = _attn(pa.reshape(bsz, seq, GA_WIDTH), attn_sinks, l)
        yb = _lru(pb.reshape(bsz, seq, GB_WIDTH), conv_w, lru_vec, wa_bd, wx_bd, l)
        yc, v_own = _rwkv(pc.reshape(bsz, seq, GC_WIDTH), v_first, mu_all, vec_all, w2_all, a2_all,
                          v1_all, v2_all, consts, l)
        if l == 0:
            v_first = v_own
        x2 = _out(ya.reshape(n, A_WIDTH), yb.reshape(n, B_WIDTH), yc.reshape(n, C_WIDTH), x2,
                  w_out_b, ln_all, alpha, l)
    return x2.reshape(bsz, seq, dm)
```

```python
import functools
import math

import jax
import jax.numpy as jnp
from jax import lax
from jax.experimental import pallas as pl
from jax.experimental.pallas import tpu as pltpu

F32 = jnp.float32
BF16 = jnp.bfloat16

D_MODEL = 1024
HEAD_DIM = 64
A_Q_HEADS = 8
A_KV_HEADS = 2
A_WIDTH = A_Q_HEADS * HEAD_DIM
A_KV_WIDTH = A_KV_HEADS * HEAD_DIM
ATT_BLOCK = 128
B_WIDTH = 256
B_BLOCKS = 4
CONV_WIDTH = 4
LRU_C = 8.0
C_HEADS = 4
C_WIDTH = C_HEADS * HEAD_DIM
DECAY_RANK = 32
AICL_RANK = 32
VRES_RANK = 16
C_SHIFT_WIDTH = 3 * C_WIDTH + DECAY_RANK + AICL_RANK
GN_EPS = 64e-5
LN_EPS = 1e-5
LOG2E = math.log2(math.e)

LANES = 128
SUBLANES = 8
C_SHIFT_PAD = ((C_SHIFT_WIDTH + LANES - 1) // LANES) * LANES
GA_WIDTH = A_WIDTH + 2 * A_KV_WIDTH + A_WIDTH
GB_WIDTH = 2 * B_WIDTH
GC_WIDTH = C_SHIFT_WIDTH + C_WIDTH
D_IN_PAD = GA_WIDTH + GB_WIDTH + GC_WIDTH
D_MIX = A_WIDTH + B_WIDTH + C_WIDTH

CHUNK = 64
STACK = C_HEADS * CHUNK
RWKV_GROUP = 8
ATT_TILE = 512
LRU_TILE = 256
PROJ_TM = 512
OUT_TM = 1024
OUT_SUB = 256
VMEM_LIMIT = 56 * 1024 * 1024


def _mm(a, b):
    return jnp.dot(a.astype(BF16), b.astype(BF16), preferred_element_type=F32)


def _mm_nt(a, b):
    return lax.dot_general(a.astype(BF16), b.astype(BF16), (((1,), (1,)), ((), ())),
                           preferred_element_type=F32)


def _mm_tn(a, b):
    return lax.dot_general(a.astype(BF16), b.astype(BF16), (((0,), (0,)), ((), ())),
                           preferred_element_type=F32)


def _split_bf16(x, parts):
    out = []
    rem = x
    for _ in range(parts):
        hi = rem.astype(BF16)
        out.append(hi)
        rem = rem - hi.astype(F32)
    return out


def _mm_exact_rhs(x, m_bf16, parts):
    n = x.shape[0]
    t = jnp.dot(jnp.concatenate(_split_bf16(x, parts), axis=0), m_bf16, preferred_element_type=F32)
    acc = t[0:n]
    for p in range(1, parts):
        acc = acc + t[p * n:(p + 1) * n]
    return acc


def _mm_exact_lhs(m_bf16, x, parts):
    n = x.shape[1]
    t = jnp.dot(m_bf16, jnp.concatenate(_split_bf16(x, parts), axis=1), preferred_element_type=F32)
    acc = t[:, 0:n]
    for p in range(1, parts):
        acc = acc + t[:, p * n:(p + 1) * n]
    return acc


def _sigmoid(x):
    return 1.0 / (1.0 + jnp.exp(-x))


def _silu(x):
    return x * _sigmoid(x)


def _softplus(x):
    return jnp.maximum(x, 0.0) + jnp.log(1.0 + jnp.exp(-jnp.abs(x)))


def _layer_spec(shape, layer):
    return pl.BlockSpec((None,) + shape, lambda *_: (layer,) + (0,) * len(shape))


def _const_spec(shape):
    return pl.BlockSpec(shape, lambda *_: (0,) * len(shape))


def _proj_kernel(x_ref, w_ref, oa_ref, ob_ref, oc_ref):
    x = x_ref[...].astype(BF16)
    oa_ref[...] = jnp.dot(x, w_ref[:, 0:GA_WIDTH], preferred_element_type=F32).astype(BF16)
    ob_ref[...] = jnp.dot(x, w_ref[:, GA_WIDTH:GA_WIDTH + GB_WIDTH], preferred_element_type=F32)
    oc_ref[...] = jnp.dot(x, w_ref[:, GA_WIDTH + GB_WIDTH:D_IN_PAD], preferred_element_type=F32)


def _proj(x2, w_all, layer):
    n = x2.shape[0]
    return pl.pallas_call(
        _proj_kernel,
        grid=(n // PROJ_TM,),
        in_specs=[pl.BlockSpec((PROJ_TM, D_MODEL), lambda i: (i, 0)),
                  _layer_spec((D_MODEL, D_IN_PAD), layer)],
        out_specs=[pl.BlockSpec((PROJ_TM, GA_WIDTH), lambda i: (i, 0)),
                   pl.BlockSpec((PROJ_TM, GB_WIDTH), lambda i: (i, 0)),
                   pl.BlockSpec((PROJ_TM, GC_WIDTH), lambda i: (i, 0))],
        out_shape=[jax.ShapeDtypeStruct((n, GA_WIDTH), BF16),
                   jax.ShapeDtypeStruct((n, GB_WIDTH), F32),
                   jax.ShapeDtypeStruct((n, GC_WIDTH), F32)],
        compiler_params=pltpu.CompilerParams(dimension_semantics=("arbitrary",),
                                             vmem_limit_bytes=VMEM_LIMIT),
        name="proj",
    )(x2, w_all)


def _attn_kernel(layer, sink_ref, cur_ref, prev_ref, o_ref):
    n = pl.program_id(1)
    blk = ATT_BLOCK
    nsub = ATT_TILE // blk
    grp = A_Q_HEADS // A_KV_HEADS
    lo = lax.broadcasted_iota(jnp.int32, (blk, A_KV_WIDTH), 1) < HEAD_DIM

    def variants(x16):
        x = x16.astype(F32)
        xs = pltpu.roll(x, HEAD_DIM, axis=1)
        return ((jnp.where(lo, x, 0.0).astype(BF16), jnp.where(lo, 0.0, xs).astype(BF16)),
                (jnp.where(lo, xs, 0.0).astype(BF16), jnp.where(lo, 0.0, x).astype(BF16)))

    k0, v0 = A_WIDTH, A_WIDTH + A_KV_WIDTH
    g0 = A_WIDTH + 2 * A_KV_WIDTH
    kblk = [variants(prev_ref[0, :, 0:A_KV_WIDTH])]
    vblk = [variants(prev_ref[0, :, A_KV_WIDTH:2 * A_KV_WIDTH])]
    for j in range(nsub):
        kblk.append(variants(cur_ref[0, j * blk:(j + 1) * blk, k0:k0 + A_KV_WIDTH]))
        vblk.append(variants(cur_ref[0, j * blk:(j + 1) * blk, v0:v0 + A_KV_WIDTH]))

    qi = lax.broadcasted_iota(jnp.int32, (blk, 2 * blk), 0)
    kj = lax.broadcasted_iota(jnp.int32, (blk, 2 * blk), 1)
    diff = qi + blk - kj
    band = (diff >= 0) & (diff < blk)
    first = band & (kj + (n * nsub - 1) * blk >= 0)

    for j in range(nsub):
        mask = first if j == 0 else band
        rows = slice(j * blk, (j + 1) * blk)
        scores = []
        for p in range(A_Q_HEADS // 2):
            h = (2 * p) // grp
            qp = cur_ref[0, rows, p * LANES:(p + 1) * LANES]
            for half in range(2):
                kb = jnp.concatenate([kblk[j][h][half], kblk[j + 1][h][half]], axis=0)
                scores.append(lax.dot_general(qp, kb, (((1,), (1,)), ((), ())),
                                              preferred_element_type=F32))
        probs, inv = [], []
        for idx, s in enumerate(scores):
            sink2 = sink_ref[layer, idx] * LOG2E
            s = jnp.where(mask, s, -jnp.inf)
            m = jnp.maximum(jnp.max(s, axis=-1, keepdims=True), sink2)
            e = jnp.exp2(s - m)
            den = jnp.sum(e, axis=-1, keepdims=True) + jnp.exp2(sink2 - m)
            probs.append(e.astype(BF16))
            inv.append(1.0 / den)
        for p in range(A_Q_HEADS // 2):
            h = (2 * p) // grp
            acc = None
            for half in range(2):
                vb = jnp.concatenate([vblk[j][h][half], vblk[j + 1][h][half]], axis=0)
                o = jnp.dot(probs[2 * p + half], vb, preferred_element_type=F32) * inv[2 * p + half]
                acc = o if acc is None else acc + o
            g = cur_ref[0, rows, g0 + p * LANES:g0 + (p + 1) * LANES].astype(F32)
            o_ref[0, rows, p * LANES:(p + 1) * LANES] = (acc * _silu(g)).astype(BF16)


def _attn(pa, sinks, layer):
    bsz, seq, _ = pa.shape
    nsub = ATT_TILE // ATT_BLOCK
    kv_blk = A_WIDTH // (2 * A_KV_WIDTH)
    return pl.pallas_call(
        functools.partial(_attn_kernel, layer),
        grid=(bsz, seq // ATT_TILE),
        in_specs=[pl.BlockSpec(memory_space=pltpu.SMEM),
                  pl.BlockSpec((1, ATT_TILE, GA_WIDTH), lambda b, n: (b, n, 0)),
                  pl.BlockSpec((1, ATT_BLOCK, 2 * A_KV_WIDTH),
                               lambda b, n: (b, jnp.maximum(n * nsub - 1, 0), kv_blk))],
        out_specs=pl.BlockSpec((1, ATT_TILE, A_WIDTH), lambda b, n: (b, n, 0)),
        out_shape=jax.ShapeDtypeStruct((bsz, seq, A_WIDTH), BF16),
        compiler_params=pltpu.CompilerParams(dimension_semantics=("arbitrary", "arbitrary"),
                                             vmem_limit_bytes=VMEM_LIMIT),
        name="attn",
    )(sinks, pa, pa)


def _lru_kernel(pb_ref, cw_ref, vec_ref, wa_ref, wx_ref, o_ref):
    seq = pb_ref.shape[1]
    tile = LRU_TILE
    ngrp = tile // SUBLANES
    conv_b = vec_ref[0:1, :]
    ba = vec_ref[1:2, :]
    bx = vec_ref[2:3, :]
    lam = vec_ref[3:4, :]
    neg_c_sp = -LRU_C * _softplus(-lam)

    def body(i, hc):
        t0 = pl.multiple_of(i * tile, tile)
        x = pb_ref[0, pl.ds(t0, tile), 0:B_WIDTH]
        tp = pl.multiple_of(jnp.maximum(t0 - SUBLANES, 0), SUBLANES)
        xp = pb_ref[0, pl.ds(tp, SUBLANES), 0:B_WIDTH]
        xp = jnp.where(i > 0, xp, 0.0)
        xe = jnp.concatenate([xp, x], axis=0)
        xc = cw_ref[CONV_WIDTH - 1:CONV_WIDTH, :] * x + conv_b
        for j in range(1, CONV_WIDTH):
            xs = pltpu.roll(xe, j, axis=0)[SUBLANES:SUBLANES + tile]
            xc = xc + cw_ref[CONV_WIDTH - 1 - j:CONV_WIDTH - j, :] * xs
        r = _sigmoid(_mm(xc, wa_ref[...]) + ba)
        ig = _sigmoid(_mm(xc, wx_ref[...]) + bx)
        a = jnp.exp(neg_c_sp * r)
        u = jnp.sqrt(1.0 - a * a) * (ig * xc)
        a3 = a.reshape(ngrp, SUBLANES, B_WIDTH)
        u3 = u.reshape(ngrp, SUBLANES, B_WIDTH)
        sub = lax.broadcasted_iota(jnp.int32, a3.shape, 1)
        d = 1
        while d < SUBLANES:
            keep = sub >= d
            u3 = a3 * jnp.where(keep, pltpu.roll(u3, d, axis=1), 0.0) + u3
            a3 = a3 * jnp.where(keep, pltpu.roll(a3, d, axis=1), 1.0)
            d *= 2
        hs = []
        carry = hc
        for gi in range(ngrp):
            hg = u3[gi] + a3[gi] * carry
            hs.append(hg)
            carry = hg[SUBLANES - 1:SUBLANES, :]
        h = jnp.concatenate(hs, axis=0)
        g = pb_ref[0, pl.ds(t0, tile), B_WIDTH:2 * B_WIDTH]
        o_ref[0, pl.ds(t0, tile), :] = (h * _silu(g)).astype(BF16)
        return carry

    lax.fori_loop(0, seq // tile, body, jnp.zeros((1, B_WIDTH), F32))


def _lru(pb, cw_all, vec_all, wa_all, wx_all, layer):
    bsz, seq, _ = pb.shape
    return pl.pallas_call(
        _lru_kernel,
        grid=(bsz,),
        in_specs=[pl.BlockSpec((1, seq, GB_WIDTH), lambda b: (b, 0, 0)),
                  _layer_spec((CONV_WIDTH, B_WIDTH), layer), _layer_spec((4, B_WIDTH), layer),
                  _layer_spec((B_WIDTH, B_WIDTH), layer), _layer_spec((B_WIDTH, B_WIDTH), layer)],
        out_specs=pl.BlockSpec((1, seq, B_WIDTH), lambda b: (b, 0, 0)),
        out_shape=jax.ShapeDtypeStruct((bsz, seq, B_WIDTH), BF16),
        compiler_params=pltpu.CompilerParams(dimension_semantics=("arbitrary",),
                                             vmem_limit_bytes=VMEM_LIMIT),
        name="lru",
    )(pb, cw_all, vec_all, wa_all, wx_all)


def _stack_heads(x, bd16):
    return jnp.concatenate([x.astype(BF16)] * C_HEADS, axis=0) * bd16


def _unstack_heads(x, bd):
    x = x * bd
    out = x[0:CHUNK]
    for h in range(1, C_HEADS):
        out = out + x[h * CHUNK:(h + 1) * CHUNK]
    return out


def _rwkv_pre(cs, prev_row, vfirst, mu, vec, w2p, a2p, v1p, v2p, ones_bd, tril_l, bd, bd32, sl, le,
              eye_w):
    w0, a0, k_k, k_a, r_k, _, _, v0 = (vec[i:i + 1, :] for i in range(8))
    row = lax.broadcasted_iota(jnp.int32, cs.shape, 0)
    prev = jnp.where(row == 0, prev_row, pltpu.roll(cs, 1, axis=0))
    xs = cs + (prev - cs) * mu
    r = xs[:, 0:C_WIDTH]
    k = xs[:, C_WIDTH:2 * C_WIDTH]
    v = xs[:, 2 * C_WIDTH:3 * C_WIDTH]
    la = xs[:, 3 * C_WIDTH:C_SHIFT_PAD]
    yield
    logw = -_softplus(-(w0 + _mm(jnp.tanh(la), w2p))) - 0.5
    lw = -LOG2E * jnp.exp(logw)
    a = _sigmoid(a0 + _mm(la, a2p))
    v_own = v
    if vfirst is not None:
        v = v + (vfirst - v) * _sigmoid(v0 + _mm(_mm(v, v1p), v2p))
    yield
    kk = k * k_k
    kap = kk * lax.rsqrt(_mm_exact_rhs(kk * kk, ones_bd, 2) + 1e-12)
    k2 = k * (1.0 + (a - 1.0) * k_a)
    b = kap * a

    yield
    cum = _mm_exact_lhs(tril_l, lw, 3)
    cum_end = cum[CHUNK - 1:CHUNK, :]
    e_in = jnp.exp2(cum)
    e_ex = jnp.exp2(cum - lw)
    e_neg = jnp.exp2(-cum)
    e_end = jnp.exp2(cum_end - cum)
    p_end = jnp.exp2(cum_end)

    yield
    kt = kap * e_ex
    rt = r * e_in
    kt_st = _stack_heads(kt, bd)
    vs_st = _stack_heads(v, bd)
    gram = _mm_nt(jnp.concatenate([kt, rt], axis=0),
                  jnp.concatenate([_stack_heads(b * e_neg, bd), _stack_heads(k2 * e_neg, bd)], axis=0))
    a_ab = jnp.where(sl, gram[0:CHUNK, 0:STACK], 0.0)
    a_ak = jnp.where(sl, gram[0:CHUNK, STACK:2 * STACK], 0.0)
    a_rb = jnp.where(le, gram[CHUNK:2 * CHUNK, 0:STACK], 0.0)
    a_rk = jnp.where(le, gram[CHUNK:2 * CHUNK, STACK:2 * STACK], 0.0)

    yield
    pw = -a_ab
    t_w = eye_w + pw
    av = _mm(jnp.concatenate([a_ak, a_rk], axis=0), vs_st)
    pw = _mm(pw, _stack_heads(pw, bd))
    for _ in range(1, int(math.log2(CHUNK)) - 1):
        yield
        both = _mm(jnp.concatenate([t_w, pw], axis=0), _stack_heads(pw, bd))
        t_w = t_w + both[0:CHUNK]
        pw = both[CHUNK:2 * CHUNK]
    yield
    t_w = t_w + _mm(t_w, _stack_heads(pw, bd))
    yield
    tx = _mm(t_w, jnp.concatenate([kt_st, _stack_heads(av[0:CHUNK], bd)], axis=1))
    yield
    m1 = tx[:, 0:STACK]
    c1 = tx[:, STACK:2 * STACK]
    ab = _mm(a_rb, jnp.concatenate([_stack_heads(m1, bd), _stack_heads(c1, bd)], axis=1))
    q_n = rt - ab[:, 0:STACK]
    d_n = av[CHUNK:2 * CHUNK] - ab[:, STACK:2 * STACK]
    bh = b * e_end
    g_w = eye_w * p_end - _unstack_heads(_mm_tn(bh, m1), bd32)
    c_w = _unstack_heads(_mm_tn(jnp.concatenate([k2 * e_end, bh], axis=0),
                                jnp.concatenate([v, -c1], axis=0)), bd32)
    bonus_arg = r * k2 * r_k
    return q_n, d_n, g_w, c_w, bonus_arg, v, v_own


RWKV_SAVED = 6


def _rwkv_kernel(use_vres, *refs):
    if use_vres:
        (pc_ref, vf_ref, mu_ref, vec_ref, w2_ref, a2_ref, v1_ref, v2_ref, tril_ref, bd_ref, bdf_ref,
         sl_ref, le_ref, eye_ref, y_ref, h_ref, pre_ref) = refs
        vo_ref = None
    else:
        (pc_ref, mu_ref, vec_ref, w2_ref, a2_ref, v1_ref, v2_ref, tril_ref, bd_ref, bdf_ref,
         sl_ref, le_ref, eye_ref, y_ref, vo_ref, h_ref, pre_ref) = refs
        vf_ref = None
    seq = pc_ref.shape[1]
    ngroups = seq // (CHUNK * RWKV_GROUP)
    h_ref[...] = jnp.zeros_like(h_ref)

    def chunk_start(gi, j):
        return pl.multiple_of((gi * RWKV_GROUP + j) * CHUNK, CHUNK)

    def pre_generators(gi, first_group):
        bd = bd_ref[...]
        sl = sl_ref[...] > 0.0
        le = le_ref[...] > 0.0
        gens = []
        prev_row = None
        for j in range(RWKV_GROUP):
            t0 = chunk_start(gi, j)
            cs = pc_ref[0, pl.ds(t0, CHUNK), 0:C_SHIFT_PAD]
            if j == 0:
                if first_group:
                    prev_row = jnp.zeros((1, C_SHIFT_PAD), F32)
                else:
                    tp = pl.multiple_of(t0 - SUBLANES, SUBLANES)
                    prev_row = pc_ref[0, pl.ds(tp, SUBLANES), 0:C_SHIFT_PAD][SUBLANES - 1:SUBLANES, :]
            vfirst = vf_ref[0, pl.ds(t0, CHUNK), :] if use_vres else None
            gens.append(_rwkv_pre(cs, prev_row, vfirst, mu_ref[...], vec_ref[...], w2_ref[...],
                                  a2_ref[...], v1_ref[...], v2_ref[...], bd, tril_ref[...],
                                  bd, bdf_ref[...], sl, le, eye_ref[...]))
            prev_row = cs[CHUNK - 1:CHUNK, :]
        return gens

    def chain_tasks(gi):
        ys = [None] * RWKV_GROUP
        stats = [None] * RWKV_GROUP
        state = {}

        def step(j):
            def run():
                h_w = h_ref[...] if j == 0 else state["h"]
                q_n, d_n, g_w, c_w = (pre_ref[j * RWKV_SAVED + k] for k in range(4))
                res = _mm(jnp.concatenate([q_n, g_w], axis=0), _stack_heads(h_w, bd_ref[...]))
                ys[j] = res[0:CHUNK] + d_n
                state["h"] = res[CHUNK:2 * CHUNK] + c_w
                if j == RWKV_GROUP - 1:
                    h_ref[...] = state["h"]
            return run

        def sums():
            for j in range(RWKV_GROUP):
                s = _mm_exact_rhs(jnp.concatenate([ys[j], pre_ref[j * RWKV_SAVED + 4]], axis=0),
                                  bd_ref[...], 2)
                yc = ys[j] - s[0:CHUNK] * (1.0 / HEAD_DIM)
                stats[j] = (yc, s[CHUNK:2 * CHUNK])

        def variance():
            for j in range(RWKV_GROUP):
                yc, bonus = stats[j]
                yv = _mm_exact_rhs(yc * yc, bd_ref[...], 2) * (1.0 / HEAD_DIM)
                stats[j] = (yc, bonus, yv)

        def finish():
            gn_w = vec_ref[5:6, :]
            gn_b = vec_ref[6:7, :]
            for j in range(RWKV_GROUP):
                t0 = chunk_start(gi, j)
                yc, bonus, yv = stats[j]
                yn = yc * lax.rsqrt(yv + GN_EPS) * gn_w + gn_b
                g = pc_ref[0, pl.ds(t0, CHUNK), C_SHIFT_WIDTH:GC_WIDTH]
                out = (yn + bonus * pre_ref[j * RWKV_SAVED + 5]) * _silu(g)
                y_ref[0, pl.ds(t0, CHUNK), :] = out.astype(BF16)

        return [step(j) for j in range(RWKV_GROUP)] + [sums, variance, finish]

    def run_group(gi, first_group, tasks):
        gens = pre_generators(gi, first_group)
        tasks = list(tasks)
        pre = [None] * RWKV_GROUP
        sweep = 0
        while any(p is None for p in pre):
            for j, gen in enumerate(gens):
                if pre[j] is None:
                    try:
                        next(gen)
                    except StopIteration as stop:
                        pre[j] = stop.value
            sweep += 1
            if tasks and sweep >= 2:
                tasks.pop(0)()
        for t in tasks:
            t()
        for j in range(RWKV_GROUP):
            for k in range(RWKV_SAVED):
                pre_ref[j * RWKV_SAVED + k] = pre[j][k]
            if not use_vres:
                vo_ref[0, pl.ds(chunk_start(gi, j), CHUNK), :] = pre[j][RWKV_SAVED]

    run_group(0, True, [])

    def body(i, carry):
        run_group(i, False, chain_tasks(i - 1))
        return carry

    lax.fori_loop(1, ngroups, body, 0)
    for t in chain_tasks(ngroups - 1):
        t()


def _rwkv_consts():
    idx = jnp.arange(STACK)
    same = (idx[:, None] // CHUNK) == (idx[None, :] // CHUNK)
    ci = jnp.arange(CHUNK)
    tril_l = (ci[None, :] <= ci[:, None]).astype(BF16)
    s_loc = idx[None, :] % CHUNK
    sl = (s_loc < ci[:, None]).astype(F32)
    le = (s_loc <= ci[:, None]).astype(F32)
    eye_w = (s_loc == ci[:, None]).astype(F32)
    return tril_l, same.astype(BF16), same.astype(F32), sl, le, eye_w


def _rwkv(pc, vfirst, mu_all, vec_all, w2_all, a2_all, v1_all, v2_all, consts, layer):
    bsz, seq, _ = pc.shape
    use_vres = vfirst is not None
    sq = (STACK, STACK)
    wide = (CHUNK, STACK)
    seq_spec = lambda w: pl.BlockSpec((1, seq, w), lambda b: (b, 0, 0))
    in_specs = [seq_spec(GC_WIDTH)] + ([seq_spec(C_WIDTH)] if use_vres else []) + [
        _layer_spec((1, C_SHIFT_PAD), layer), _layer_spec((8, C_WIDTH), layer),
        _layer_spec((LANES, C_WIDTH), layer), _layer_spec((LANES, C_WIDTH), layer),
        _layer_spec((C_WIDTH, LANES), layer), _layer_spec((LANES, C_WIDTH), layer),
        _const_spec((CHUNK, CHUNK)), _const_spec(sq), _const_spec(sq),
        _const_spec(wide), _const_spec(wide), _const_spec(wide)]
    out_specs = [seq_spec(C_WIDTH)]
    out_shape = [jax.ShapeDtypeStruct((bsz, seq, C_WIDTH), BF16)]
    if not use_vres:
        out_specs.append(seq_spec(C_WIDTH))
        out_shape.append(jax.ShapeDtypeStruct((bsz, seq, C_WIDTH), F32))
    args = (pc,) + ((vfirst,) if use_vres else ()) + (mu_all, vec_all, w2_all, a2_all, v1_all, v2_all)
    outs = pl.pallas_call(
        functools.partial(_rwkv_kernel, use_vres),
        grid=(bsz,),
        in_specs=in_specs,
        out_specs=out_specs,
        out_shape=out_shape,
        scratch_shapes=[pltpu.VMEM(wide, F32), pltpu.VMEM((RWKV_GROUP * RWKV_SAVED,) + wide, F32)],
        compiler_params=pltpu.CompilerParams(dimension_semantics=("arbitrary",),
                                             vmem_limit_bytes=VMEM_LIMIT),
        name="rwkv",
    )(*args, *consts)
    return (outs[0], None) if use_vres else (outs[0], outs[1])


def _out_kernel(alpha, ya_ref, yb_ref, yc_ref, x_ref, w_ref, ln_ref, o_ref):
    def project(rows):
        acc = jnp.dot(ya_ref[rows, :], w_ref[0:A_WIDTH, :], preferred_element_type=F32)
        acc = acc + jnp.dot(yb_ref[rows, :], w_ref[A_WIDTH:A_WIDTH + B_WIDTH, :],
                            preferred_element_type=F32)
        return acc + jnp.dot(yc_ref[rows, :], w_ref[A_WIDTH + B_WIDTH:D_MIX, :],
                             preferred_element_type=F32)

    def norm(rows, acc):
        z = alpha * x_ref[rows, :] + acc
        mu = jnp.mean(z, axis=-1, keepdims=True)
        zc = z - mu
        var = jnp.mean(zc * zc, axis=-1, keepdims=True)
        o_ref[rows, :] = zc * lax.rsqrt(var + LN_EPS) * ln_ref[0:1, :] + ln_ref[1:2, :]

    subs = [slice(s * OUT_SUB, (s + 1) * OUT_SUB) for s in range(OUT_TM // OUT_SUB)]
    acc_prev = project(subs[0])
    for s in range(1, len(subs)):
        acc_next = project(subs[s])
        norm(subs[s - 1], acc_prev)
        acc_prev = acc_next
    norm(subs[-1], acc_prev)


def _out(ya, yb, yc, x2, w_all, ln_all, alpha, layer):
    n = x2.shape[0]
    row = lambda w: pl.BlockSpec((OUT_TM, w), lambda i: (i, 0))
    return pl.pallas_call(
        functools.partial(_out_kernel, alpha),
        grid=(n // OUT_TM,),
        in_specs=[row(A_WIDTH), row(B_WIDTH), row(C_WIDTH), row(D_MODEL),
                  _layer_spec((D_MIX, D_MODEL), layer), _layer_spec((2, D_MODEL), layer)],
        out_specs=row(D_MODEL),
        out_shape=jax.ShapeDtypeStruct((n, D_MODEL), F32),
        compiler_params=pltpu.CompilerParams(dimension_semantics=("arbitrary",),
                                             vmem_limit_bytes=VMEM_LIMIT),
        name="outproj",
    )(ya, yb, yc, x2, w_all, ln_all)


def _block_diag_all(w):
    depth, nblk, d, _ = w.shape
    eye = jnp.eye(nblk, dtype=w.dtype)
    return (w[:, :, :, None, :] * eye[None, :, None, :, None]).reshape(depth, nblk * d, nblk * d)


def kernel(x, w_in, w_out, ln_g, ln_b, attn_sinks, conv_w, conv_b, lru_wa, lru_ba, lru_wx, lru_bx,
           lru_lambda, rwkv_mu, rwkv_w0, rwkv_w2, rwkv_a0, rwkv_a2, rwkv_kk, rwkv_ka, rwkv_rk,
           rwkv_gn_w, rwkv_gn_b, rwkv_v0, rwkv_v1, rwkv_v2):
    bsz, seq, dm = x.shape
    depth = w_in.shape[0]
    alpha = (2 * depth) ** 0.25
    n = bsz * seq
    pad_w = C_SHIFT_PAD - C_SHIFT_WIDTH

    q_scale = jnp.concatenate([jnp.full((A_WIDTH,), HEAD_DIM ** -0.5 * LOG2E, F32),
                               jnp.ones((w_in.shape[2] - A_WIDTH,), F32)])
    w_in_p = (w_in * q_scale).astype(BF16)
    w_out_b = w_out.astype(BF16)
    ln_all = jnp.stack([ln_g, ln_b], axis=1)
    lru_vec = jnp.stack([conv_b, lru_ba, lru_bx, lru_lambda], axis=1)
    wa_bd = _block_diag_all(lru_wa).astype(BF16)
    wx_bd = _block_diag_all(lru_wx).astype(BF16)
    mu_all = jnp.pad(rwkv_mu, ((0, 0), (0, pad_w)))[:, None, :]
    v0_all = jnp.pad(rwkv_v0, ((1, 0), (0, 0)))
    vec_all = jnp.stack([rwkv_w0, rwkv_a0, rwkv_kk, rwkv_ka, rwkv_rk.reshape(depth, C_WIDTH),
                         rwkv_gn_w, rwkv_gn_b, v0_all], axis=1)
    w2_all = jnp.pad(rwkv_w2, ((0, 0), (0, LANES - DECAY_RANK), (0, 0))).astype(BF16)
    a2_all = jnp.pad(rwkv_a2, ((0, 0), (DECAY_RANK, LANES - DECAY_RANK - AICL_RANK), (0, 0))).astype(BF16)
    v1_all = jnp.pad(rwkv_v1, ((1, 0), (0, 0), (0, LANES - VRES_RANK))).astype(BF16)
    v2_all = jnp.pad(rwkv_v2, ((1, 0), (0, LANES - VRES_RANK), (0, 0))).astype(BF16)
    consts = _rwkv_consts()

    x2 = x.reshape(n, dm)
    v_first = None
    for l in range(depth):
        pa, pb, pc = _proj(x2, w_in_p, l)
        ya = _attn(pa.reshape(bsz, seq, GA_WIDTH), attn_sinks, l)
        yb = _lru(pb.reshape(bsz, seq, GB_WIDTH), conv_w, lru_vec, wa_bd, wx_bd, l)
        yc, v_own = _rwkv(pc.reshape(bsz, seq, GC_WIDTH), v_first, mu_all, vec_all, w2_all, a2_all,
                          v1_all, v2_all, consts, l)
        if l == 0:
            v_first = v_own
        x2 = _out(ya.reshape(n, A_WIDTH), yb.reshape(n, B_WIDTH), yc.reshape(n, C_WIDTH), x2,
                  w_out_b, ln_all, alpha, l)
    return x2.reshape(bsz, seq, dm)
```

```python
import functools
import math

import jax
import jax.numpy as jnp
from jax import lax
from jax.experimental import pallas as pl
from jax.experimental.pallas import tpu as pltpu

F32 = jnp.float32
BF16 = jnp.bfloat16

D_MODEL = 1024
HEAD_DIM = 64
A_Q_HEADS = 8
A_KV_HEADS = 2
A_WIDTH = A_Q_HEADS * HEAD_DIM
A_KV_WIDTH = A_KV_HEADS * HEAD_DIM
ATT_BLOCK = 128
B_WIDTH = 256
B_BLOCKS = 4
CONV_WIDTH = 4
LRU_C = 8.0
C_HEADS = 4
C_WIDTH = C_HEADS * HEAD_DIM
DECAY_RANK = 32
AICL_RANK = 32
VRES_RANK = 16
C_SHIFT_WIDTH = 3 * C_WIDTH + DECAY_RANK + AICL_RANK
GN_EPS = 64e-5
LN_EPS = 1e-5
LOG2E = math.log2(math.e)

LANES = 128
SUBLANES = 8
C_SHIFT_PAD = ((C_SHIFT_WIDTH + LANES - 1) // LANES) * LANES
GA_WIDTH = A_WIDTH + 2 * A_KV_WIDTH + A_WIDTH
GB_WIDTH = 2 * B_WIDTH
GC_WIDTH = C_SHIFT_WIDTH + C_WIDTH
D_IN_PAD = GA_WIDTH + GB_WIDTH + GC_WIDTH
D_MIX = A_WIDTH + B_WIDTH + C_WIDTH

CHUNK = 64
STACK = C_HEADS * CHUNK
RWKV_GROUP = 8
ATT_TILE = 512
LRU_TILE = 256
PROJ_TM = 512
OUT_TM = 1024
OUT_SUB = 256
VMEM_LIMIT = 56 * 1024 * 1024


def _mm(a, b):
    return jnp.dot(a.astype(BF16), b.astype(BF16), preferred_element_type=F32)


def _mm_nt(a, b):
    return lax.dot_general(a.astype(BF16), b.astype(BF16), (((1,), (1,)), ((), ())),
                           preferred_element_type=F32)


def _split_bf16(x, parts):
    out = []
    rem = x
    for _ in range(parts):
        hi = rem.astype(BF16)
        out.append(hi)
        rem = rem - hi.astype(F32)
    return out


def _mm_exact_rhs(x, m_bf16, parts):
    n = x.shape[0]
    t = jnp.dot(jnp.concatenate(_split_bf16(x, parts), axis=0), m_bf16, preferred_element_type=F32)
    acc = t[0:n]
    for p in range(1, parts):
        acc = acc + t[p * n:(p + 1) * n]
    return acc


def _mm_exact_lhs(m_bf16, x, parts):
    n = x.shape[1]
    t = jnp.dot(m_bf16, jnp.concatenate(_split_bf16(x, parts), axis=1), preferred_element_type=F32)
    acc = t[:, 0:n]
    for p in range(1, parts):
        acc = acc + t[:, p * n:(p + 1) * n]
    return acc


def _sigmoid(x):
    return 1.0 / (1.0 + jnp.exp(-x))


def _silu(x):
    return x * _sigmoid(x)


def _softplus(x):
    return jnp.maximum(x, 0.0) + jnp.log(1.0 + jnp.exp(-jnp.abs(x)))


def _layer_spec(shape, layer):
    return pl.BlockSpec((None,) + shape, lambda *_: (layer,) + (0,) * len(shape))


def _const_spec(shape):
    return pl.BlockSpec(shape, lambda *_: (0,) * len(shape))


def _proj_kernel(x_ref, w_ref, oa_ref, ob_ref, oc_ref):
    x = x_ref[...].astype(BF16)
    nt = (((1,), (1,)), ((), ()))
    oa_ref[...] = lax.dot_general(x, w_ref[0:GA_WIDTH, :], nt, preferred_element_type=F32).astype(BF16)
    ob_ref[...] = lax.dot_general(x, w_ref[GA_WIDTH:GA_WIDTH + GB_WIDTH, :], nt,
                                  preferred_element_type=F32)
    oc_ref[...] = lax.dot_general(x, w_ref[GA_WIDTH + GB_WIDTH:D_IN_PAD, :], nt,
                                  preferred_element_type=F32)


def _proj(x2, w_all, layer):
    n = x2.shape[0]
    return pl.pallas_call(
        _proj_kernel,
        grid=(n // PROJ_TM,),
        in_specs=[pl.BlockSpec((PROJ_TM, D_MODEL), lambda i: (i, 0)),
                  _layer_spec((D_IN_PAD, D_MODEL), layer)],
        out_specs=[pl.BlockSpec((PROJ_TM, GA_WIDTH), lambda i: (i, 0)),
                   pl.BlockSpec((PROJ_TM, GB_WIDTH), lambda i: (i, 0)),
                   pl.BlockSpec((PROJ_TM, GC_WIDTH), lambda i: (i, 0))],
        out_shape=[jax.ShapeDtypeStruct((n, GA_WIDTH), BF16),
                   jax.ShapeDtypeStruct((n, GB_WIDTH), F32),
                   jax.ShapeDtypeStruct((n, GC_WIDTH), F32)],
        compiler_params=pltpu.CompilerParams(dimension_semantics=("arbitrary",),
                                             vmem_limit_bytes=VMEM_LIMIT),
        name="proj",
    )(x2, w_all)


def _attn_kernel(layer, sink_ref, cur_ref, prev_ref, o_ref):
    n = pl.program_id(1)
    blk = ATT_BLOCK
    nsub = ATT_TILE // blk
    grp = A_Q_HEADS // A_KV_HEADS
    lo = lax.broadcasted_iota(jnp.int32, (blk, A_KV_WIDTH), 1) < HEAD_DIM

    def variants(x16):
        x = x16.astype(F32)
        xs = pltpu.roll(x, HEAD_DIM, axis=1)
        return ((jnp.where(lo, x, 0.0).astype(BF16), jnp.where(lo, 0.0, xs).astype(BF16)),
                (jnp.where(lo, xs, 0.0).astype(BF16), jnp.where(lo, 0.0, x).astype(BF16)))

    k0, v0 = A_WIDTH, A_WIDTH + A_KV_WIDTH
    g0 = A_WIDTH + 2 * A_KV_WIDTH
    kblk = [variants(prev_ref[0, :, 0:A_KV_WIDTH])]
    vblk = [variants(prev_ref[0, :, A_KV_WIDTH:2 * A_KV_WIDTH])]
    for j in range(nsub):
        kblk.append(variants(cur_ref[0, j * blk:(j + 1) * blk, k0:k0 + A_KV_WIDTH]))
        vblk.append(variants(cur_ref[0, j * blk:(j + 1) * blk, v0:v0 + A_KV_WIDTH]))

    qi = lax.broadcasted_iota(jnp.int32, (blk, 2 * blk), 0)
    kj = lax.broadcasted_iota(jnp.int32, (blk, 2 * blk), 1)
    diff = qi + blk - kj
    band = (diff >= 0) & (diff < blk)
    first = band & (kj + (n * nsub - 1) * blk >= 0)

    for j in range(nsub):
        mask = first if j == 0 else band
        rows = slice(j * blk, (j + 1) * blk)
        scores = []
        for p in range(A_Q_HEADS // 2):
            h = (2 * p) // grp
            qp = cur_ref[0, rows, p * LANES:(p + 1) * LANES]
            for half in range(2):
                kb = jnp.concatenate([kblk[j][h][half], kblk[j + 1][h][half]], axis=0)
                scores.append(lax.dot_general(qp, kb, (((1,), (1,)), ((), ())),
                                              preferred_element_type=F32))
        probs, inv = [], []
        for idx, s in enumerate(scores):
            sink2 = sink_ref[layer, idx] * LOG2E
            s = jnp.where(mask, s, -jnp.inf)
            m = jnp.maximum(jnp.max(s, axis=-1, keepdims=True), sink2)
            e = jnp.exp2(s - m)
            den = jnp.sum(e, axis=-1, keepdims=True) + jnp.exp2(sink2 - m)
            probs.append(e.astype(BF16))
            inv.append(1.0 / den)
        for p in range(A_Q_HEADS // 2):
            h = (2 * p) // grp
            acc = None
            for half in range(2):
                vb = jnp.concatenate([vblk[j][h][half], vblk[j + 1][h][half]], axis=0)
                o = jnp.dot(probs[2 * p + half], vb, preferred_element_type=F32) * inv[2 * p + half]
                acc = o if acc is None else acc + o
            g = cur_ref[0, rows, g0 + p * LANES:g0 + (p + 1) * LANES].astype(F32)
            o_ref[0, rows, p * LANES:(p + 1) * LANES] = (acc * _silu(g)).astype(BF16)


def _attn(pa, sinks, layer):
    bsz, seq, _ = pa.shape
    nsub = ATT_TILE // ATT_BLOCK
    kv_blk = A_WIDTH // (2 * A_KV_WIDTH)
    return pl.pallas_call(
        functools.partial(_attn_kernel, layer),
        grid=(bsz, seq // ATT_TILE),
        in_specs=[pl.BlockSpec(memory_space=pltpu.SMEM),
                  pl.BlockSpec((1, ATT_TILE, GA_WIDTH), lambda b, n: (b, n, 0)),
                  pl.BlockSpec((1, ATT_BLOCK, 2 * A_KV_WIDTH),
                               lambda b, n: (b, jnp.maximum(n * nsub - 1, 0), kv_blk))],
        out_specs=pl.BlockSpec((1, ATT_TILE, A_WIDTH), lambda b, n: (b, n, 0)),
        out_shape=jax.ShapeDtypeStruct((bsz, seq, A_WIDTH), BF16),
        compiler_params=pltpu.CompilerParams(dimension_semantics=("arbitrary", "arbitrary"),
                                             vmem_limit_bytes=VMEM_LIMIT),
        name="attn",
    )(sinks, pa, pa)


def _lru_kernel(pb_ref, cw_ref, vec_ref, wa_ref, wx_ref, o_ref):
    seq = pb_ref.shape[1]
    tile = LRU_TILE
    ngrp = tile // SUBLANES
    conv_b = vec_ref[0:1, :]
    ba = vec_ref[1:2, :]
    bx = vec_ref[2:3, :]
    lam = vec_ref[3:4, :]
    neg_c_sp = -LRU_C * _softplus(-lam)

    def body(i, hc):
        t0 = pl.multiple_of(i * tile, tile)
        x = pb_ref[0, pl.ds(t0, tile), 0:B_WIDTH]
        tp = pl.multiple_of(jnp.maximum(t0 - SUBLANES, 0), SUBLANES)
        xp = pb_ref[0, pl.ds(tp, SUBLANES), 0:B_WIDTH]
        xp = jnp.where(i > 0, xp, 0.0)
        xe = jnp.concatenate([xp, x], axis=0)
        xc = cw_ref[CONV_WIDTH - 1:CONV_WIDTH, :] * x + conv_b
        for j in range(1, CONV_WIDTH):
            xs = pltpu.roll(xe, j, axis=0)[SUBLANES:SUBLANES + tile]
            xc = xc + cw_ref[CONV_WIDTH - 1 - j:CONV_WIDTH - j, :] * xs
        r = _sigmoid(_mm(xc, wa_ref[...]) + ba)
        ig = _sigmoid(_mm(xc, wx_ref[...]) + bx)
        a = jnp.exp(neg_c_sp * r)
        u = jnp.sqrt(1.0 - a * a) * (ig * xc)
        a3 = a.reshape(ngrp, SUBLANES, B_WIDTH)
        u3 = u.reshape(ngrp, SUBLANES, B_WIDTH)
        sub = lax.broadcasted_iota(jnp.int32, a3.shape, 1)
        d = 1
        while d < SUBLANES:
            keep = sub >= d
            u3 = a3 * jnp.where(keep, pltpu.roll(u3, d, axis=1), 0.0) + u3
            a3 = a3 * jnp.where(keep, pltpu.roll(a3, d, axis=1), 1.0)
            d *= 2
        hs = []
        carry = hc
        for gi in range(ngrp):
            hg = u3[gi] + a3[gi] * carry
            hs.append(hg)
            carry = hg[SUBLANES - 1:SUBLANES, :]
        h = jnp.concatenate(hs, axis=0)
        g = pb_ref[0, pl.ds(t0, tile), B_WIDTH:2 * B_WIDTH]
        o_ref[0, pl.ds(t0, tile), :] = (h * _silu(g)).astype(BF16)
        return carry

    lax.fori_loop(0, seq // tile, body, jnp.zeros((1, B_WIDTH), F32))


def _lru(pb, cw_all, vec_all, wa_all, wx_all, layer):
    bsz, seq, _ = pb.shape
    return pl.pallas_call(
        _lru_kernel,
        grid=(bsz,),
        in_specs=[pl.BlockSpec((1, seq, GB_WIDTH), lambda b: (b, 0, 0)),
                  _layer_spec((CONV_WIDTH, B_WIDTH), layer), _layer_spec((4, B_WIDTH), layer),
                  _layer_spec((B_WIDTH, B_WIDTH), layer), _layer_spec((B_WIDTH, B_WIDTH), layer)],
        out_specs=pl.BlockSpec((1, seq, B_WIDTH), lambda b: (b, 0, 0)),
        out_shape=jax.ShapeDtypeStruct((bsz, seq, B_WIDTH), BF16),
        compiler_params=pltpu.CompilerParams(dimension_semantics=("arbitrary",),
                                             vmem_limit_bytes=VMEM_LIMIT),
        name="lru",
    )(pb, cw_all, vec_all, wa_all, wx_all)


def _stack_heads(x):
    x16 = x.astype(BF16)
    head = lax.broadcasted_iota(jnp.int32, x16.shape, 1) // HEAD_DIM
    return jnp.concatenate([jnp.where(head == h, x16, jnp.zeros_like(x16)) for h in range(C_HEADS)],
                           axis=0)


def _rwkv_pre(cs, prev, vfirst, mu, vec, w2p, a2p, v1p, v2p, ones_bd, tril_l, sl, le, eye_w):
    w0, a0, k_k, k_a, r_k, _, _, v0 = (vec[i:i + 1, :] for i in range(8))
    xs = cs + (prev - cs) * mu
    r = xs[:, 0:C_WIDTH]
    k = xs[:, C_WIDTH:2 * C_WIDTH]
    v = xs[:, 2 * C_WIDTH:3 * C_WIDTH]
    la = xs[:, 3 * C_WIDTH:C_SHIFT_PAD]
    yield
    logw = -_softplus(-(w0 + _mm(jnp.tanh(la), w2p))) - 0.5
    lw = -LOG2E * jnp.exp(logw)
    a = _sigmoid(a0 + _mm(la, a2p))
    v_own = v
    if vfirst is not None:
        v = v + (vfirst - v) * _sigmoid(v0 + _mm(_mm(v, v1p), v2p))
    yield
    kk = k * k_k
    kap = kk * lax.rsqrt(_mm_exact_rhs(kk * kk, ones_bd, 2) + 1e-12)
    k2 = k * (1.0 + (a - 1.0) * k_a)
    b = kap * a

    yield
    cum = _mm_exact_lhs(tril_l, lw, 3)
    cum_end = cum[CHUNK - 1:CHUNK, :]
    e_in = jnp.exp2(cum)
    e_ex = jnp.exp2(cum - lw)
    e_neg = jnp.exp2(-cum)
    e_end = jnp.exp2(cum_end - cum)
    p_end = jnp.exp2(cum_end)

    yield
    kt = kap * e_ex
    rt = r * e_in
    kt_st = _stack_heads(kt)
    vs_st = _stack_heads(v)
    gram = _mm_nt(jnp.concatenate([kt, rt], axis=0),
                  jnp.concatenate([_stack_heads(b * e_neg), _stack_heads(k2 * e_neg)], axis=0))
    a_ab = jnp.where(sl, gram[0:CHUNK, 0:STACK], 0.0)
    a_ak = jnp.where(sl, gram[0:CHUNK, STACK:2 * STACK], 0.0)
    a_rb = jnp.where(le, gram[CHUNK:2 * CHUNK, 0:STACK], 0.0)
    a_rk = jnp.where(le, gram[CHUNK:2 * CHUNK, STACK:2 * STACK], 0.0)

    yield
    pw = -a_ab
    t_w = eye_w + pw
    av = _mm(jnp.concatenate([a_ak, a_rk], axis=0), vs_st)
    pw = _mm(pw, _stack_heads(pw))
    for _ in range(1, int(math.log2(CHUNK)) - 1):
        yield
        both = _mm(jnp.concatenate([t_w, pw], axis=0), _stack_heads(pw))
        t_w = t_w + both[0:CHUNK]
        pw = both[CHUNK:2 * CHUNK]
    yield
    t_w = t_w + _mm(t_w, _stack_heads(pw))
    yield
    tx = _mm(t_w, jnp.concatenate([kt_st, _stack_heads(av[0:CHUNK])], axis=1))
    tr = _mm_nt(eye_w, jnp.concatenate([_stack_heads(b * e_end), _stack_heads(k2 * e_end)],
                                       axis=0))
    yield
    mc_st = jnp.concatenate([_stack_heads(tx[:, 0:STACK]), _stack_heads(tx[:, STACK:2 * STACK])],
                            axis=1)
    ab = _mm(jnp.concatenate([a_rb, tr[:, 0:STACK]], axis=0), mc_st)
    kv = _mm(tr[:, STACK:2 * STACK], vs_st)
    q_n = rt - ab[0:CHUNK, 0:STACK]
    d_n = av[CHUNK:2 * CHUNK] - ab[0:CHUNK, STACK:2 * STACK]
    g_w = eye_w * p_end - ab[CHUNK:2 * CHUNK, 0:STACK]
    c_w = kv - ab[CHUNK:2 * CHUNK, STACK:2 * STACK]
    bonus_arg = r * k2 * r_k
    return q_n, d_n, g_w, c_w, bonus_arg, v, v_own


RWKV_SAVED = 6


def _rwkv_kernel(use_vres, *refs):
    if use_vres:
        (pc_ref, vf_ref, mu_ref, vec_ref, w2_ref, a2_ref, v1_ref, v2_ref, tril_ref, bd_ref,
         sl_ref, le_ref, eye_ref, y_ref, h_ref, pre_ref) = refs
        vo_ref = None
    else:
        (pc_ref, mu_ref, vec_ref, w2_ref, a2_ref, v1_ref, v2_ref, tril_ref, bd_ref,
         sl_ref, le_ref, eye_ref, y_ref, vo_ref, h_ref, pre_ref) = refs
        vf_ref = None
    seq = pc_ref.shape[1]
    ngroups = seq // (CHUNK * RWKV_GROUP)
    h_ref[...] = jnp.zeros_like(h_ref)

    def chunk_start(gi, j):
        return pl.multiple_of((gi * RWKV_GROUP + j) * CHUNK, CHUNK)

    def pre_generators(gi, first_group):
        sl = sl_ref[...] > 0.0
        le = le_ref[...] > 0.0
        gens = []
        for j in range(RWKV_GROUP):
            t0 = chunk_start(gi, j)
            cs = pc_ref[0, pl.ds(t0, CHUNK), 0:C_SHIFT_PAD]
            if first_group and j == 0:
                row = lax.broadcasted_iota(jnp.int32, cs.shape, 0)
                prev = jnp.where(row == 0, 0.0, pltpu.roll(cs, 1, axis=0))
            else:
                tp = pl.multiple_of(t0 - SUBLANES, SUBLANES)
                ext = pc_ref[0, pl.ds(tp, CHUNK + SUBLANES), 0:C_SHIFT_PAD]
                prev = pltpu.roll(ext, 1, axis=0)[SUBLANES:SUBLANES + CHUNK]
            vfirst = vf_ref[0, pl.ds(t0, CHUNK), :] if use_vres else None
            gens.append(_rwkv_pre(cs, prev, vfirst, mu_ref[...], vec_ref[...], w2_ref[...],
                                  a2_ref[...], v1_ref[...], v2_ref[...], bd_ref[...], tril_ref[...],
                                  sl, le, eye_ref[...]))
        return gens

    def chain_tasks(gi):
        ys = [None] * RWKV_GROUP
        stats = [None] * RWKV_GROUP
        state = {}

        def step(j):
            def run():
                h_w = h_ref[...] if j == 0 else state["h"]
                q_n, d_n, g_w, c_w = (pre_ref[j * RWKV_SAVED + k] for k in range(4))
                res = _mm(jnp.concatenate([q_n, g_w], axis=0), _stack_heads(h_w))
                ys[j] = res[0:CHUNK] + d_n
                state["h"] = res[CHUNK:2 * CHUNK] + c_w
                if j == RWKV_GROUP - 1:
                    h_ref[...] = state["h"]
            return run

        def sums():
            for j in range(RWKV_GROUP):
                s = _mm_exact_rhs(jnp.concatenate([ys[j], pre_ref[j * RWKV_SAVED + 4]], axis=0),
                                  bd_ref[...], 2)
                yc = ys[j] - s[0:CHUNK] * (1.0 / HEAD_DIM)
                stats[j] = (yc, s[CHUNK:2 * CHUNK])

        def variance():
            for j in range(RWKV_GROUP):
                yc, bonus = stats[j]
                yv = _mm_exact_rhs(yc * yc, bd_ref[...], 2) * (1.0 / HEAD_DIM)
                stats[j] = (yc, bonus, yv)

        def finish():
            gn_w = vec_ref[5:6, :]
            gn_b = vec_ref[6:7, :]
            for j in range(RWKV_GROUP):
                t0 = chunk_start(gi, j)
                yc, bonus, yv = stats[j]
                yn = yc * lax.rsqrt(yv + GN_EPS) * gn_w + gn_b
                g = pc_ref[0, pl.ds(t0, CHUNK), C_SHIFT_WIDTH:GC_WIDTH]
                out = (yn + bonus * pre_ref[j * RWKV_SAVED + 5]) * _silu(g)
                y_ref[0, pl.ds(t0, CHUNK), :] = out.astype(BF16)

        return [step(j) for j in range(RWKV_GROUP)] + [sums, variance, finish]

    def run_group(gi, first_group, tasks):
        gens = pre_generators(gi, first_group)
        tasks = list(tasks)
        pre = [None] * RWKV_GROUP
        sweep = 0
        while any(p is None for p in pre):
            for j, gen in enumerate(gens):
                if pre[j] is None:
                    try:
                        next(gen)
                    except StopIteration as stop:
                        pre[j] = stop.value
            sweep += 1
            if tasks and sweep >= 2:
                tasks.pop(0)()
        for t in tasks:
            t()
        for j in range(RWKV_GROUP):
            for k in range(RWKV_SAVED):
                pre_ref[j * RWKV_SAVED + k] = pre[j][k]
            if not use_vres:
                vo_ref[0, pl.ds(chunk_start(gi, j), CHUNK), :] = pre[j][RWKV_SAVED]

    run_group(0, True, [])

    def body(i, carry):
        run_group(i, False, chain_tasks(i - 1))
        return carry

    lax.fori_loop(1, ngroups, body, 0)
    for t in chain_tasks(ngroups - 1):
        t()


def _rwkv_consts():
    idx = jnp.arange(STACK)
    same = (idx[:, None] // CHUNK) == (idx[None, :] // CHUNK)
    ci = jnp.arange(CHUNK)
    tril_l = (ci[None, :] <= ci[:, None]).astype(BF16)
    s_loc = idx[None, :] % CHUNK
    sl = (s_loc < ci[:, None]).astype(F32)
    le = (s_loc <= ci[:, None]).astype(F32)
    eye_w = (s_loc == ci[:, None]).astype(F32)
    return tril_l, same.astype(BF16), sl, le, eye_w


def _rwkv(pc, vfirst, mu_all, vec_all, w2_all, a2_all, v1_all, v2_all, consts, layer):
    bsz, seq, _ = pc.shape
    use_vres = vfirst is not None
    sq = (STACK, STACK)
    wide = (CHUNK, STACK)
    seq_spec = lambda w: pl.BlockSpec((1, seq, w), lambda b: (b, 0, 0))
    in_specs = [seq_spec(GC_WIDTH)] + ([seq_spec(C_WIDTH)] if use_vres else []) + [
        _layer_spec((1, C_SHIFT_PAD), layer), _layer_spec((8, C_WIDTH), layer),
        _layer_spec((LANES, C_WIDTH), layer), _layer_spec((LANES, C_WIDTH), layer),
        _layer_spec((C_WIDTH, LANES), layer), _layer_spec((LANES, C_WIDTH), layer),
        _const_spec((CHUNK, CHUNK)), _const_spec(sq),
        _const_spec(wide), _const_spec(wide), _const_spec(wide)]
    out_specs = [seq_spec(C_WIDTH)]
    out_shape = [jax.ShapeDtypeStruct((bsz, seq, C_WIDTH), BF16)]
    if not use_vres:
        out_specs.append(seq_spec(C_WIDTH))
        out_shape.append(jax.ShapeDtypeStruct((bsz, seq, C_WIDTH), F32))
    args = (pc,) + ((vfirst,) if use_vres else ()) + (mu_all, vec_all, w2_all, a2_all, v1_all, v2_all)
    outs = pl.pallas_call(
        functools.partial(_rwkv_kernel, use_vres),
        grid=(bsz,),
        in_specs=in_specs,
        out_specs=out_specs,
        out_shape=out_shape,
        scratch_shapes=[pltpu.VMEM(wide, F32), pltpu.VMEM((RWKV_GROUP * RWKV_SAVED,) + wide, F32)],
        compiler_params=pltpu.CompilerParams(dimension_semantics=("arbitrary",),
                                             vmem_limit_bytes=VMEM_LIMIT),
        name="rwkv",
    )(*args, *consts)
    return (outs[0], None) if use_vres else (outs[0], outs[1])


def _out_kernel(alpha, ya_ref, yb_ref, yc_ref, x_ref, w_ref, ln_ref, o_ref):
    def project(rows):
        acc = jnp.dot(ya_ref[rows, :], w_ref[0:A_WIDTH, :], preferred_element_type=F32)
        acc = acc + jnp.dot(yb_ref[rows, :], w_ref[A_WIDTH:A_WIDTH + B_WIDTH, :],
                            preferred_element_type=F32)
        return acc + jnp.dot(yc_ref[rows, :], w_ref[A_WIDTH + B_WIDTH:D_MIX, :],
                             preferred_element_type=F32)

    def norm(rows, acc):
        z = alpha * x_ref[rows, :] + acc
        mu = jnp.mean(z, axis=-1, keepdims=True)
        zc = z - mu
        var = jnp.mean(zc * zc, axis=-1, keepdims=True)
        o_ref[rows, :] = zc * lax.rsqrt(var + LN_EPS) * ln_ref[0:1, :] + ln_ref[1:2, :]

    subs = [slice(s * OUT_SUB, (s + 1) * OUT_SUB) for s in range(OUT_TM // OUT_SUB)]
    acc_prev = project(subs[0])
    for s in range(1, len(subs)):
        acc_next = project(subs[s])
        norm(subs[s - 1], acc_prev)
        acc_prev = acc_next
    norm(subs[-1], acc_prev)


def _out(ya, yb, yc, x2, w_all, ln_all, alpha, layer):
    n = x2.shape[0]
    row = lambda w: pl.BlockSpec((OUT_TM, w), lambda i: (i, 0))
    return pl.pallas_call(
        functools.partial(_out_kernel, alpha),
        grid=(n // OUT_TM,),
        in_specs=[row(A_WIDTH), row(B_WIDTH), row(C_WIDTH), row(D_MODEL),
                  _layer_spec((D_MIX, D_MODEL), layer), _layer_spec((2, D_MODEL), layer)],
        out_specs=row(D_MODEL),
        out_shape=jax.ShapeDtypeStruct((n, D_MODEL), F32),
        compiler_params=pltpu.CompilerParams(dimension_semantics=("arbitrary",),
                                             vmem_limit_bytes=VMEM_LIMIT),
        name="outproj",
    )(ya, yb, yc, x2, w_all, ln_all)


def _block_diag_all(w):
    depth, nblk, d, _ = w.shape
    eye = jnp.eye(nblk, dtype=w.dtype)
    return (w[:, :, :, None, :] * eye[None, :, None, :, None]).reshape(depth, nblk * d, nblk * d)


def kernel(x, w_in, w_out, ln_g, ln_b, attn_sinks, conv_w, conv_b, lru_wa, lru_ba, lru_wx, lru_bx,
           lru_lambda, rwkv_mu, rwkv_w0, rwkv_w2, rwkv_a0, rwkv_a2, rwkv_kk, rwkv_ka, rwkv_rk,
           rwkv_gn_w, rwkv_gn_b, rwkv_v0, rwkv_v1, rwkv_v2):
    bsz, seq, dm = x.shape
    depth = w_in.shape[0]
    alpha = (2 * depth) ** 0.25
    n = bsz * seq
    pad_w = C_SHIFT_PAD - C_SHIFT_WIDTH

    q_scale = jnp.concatenate([jnp.full((A_WIDTH,), HEAD_DIM ** -0.5 * LOG2E, F32),
                               jnp.ones((w_in.shape[2] - A_WIDTH,), F32)])
    w_in_p = (jnp.swapaxes(w_in, 1, 2) * q_scale[None, :, None]).astype(BF16)
    w_out_b = w_out.astype(BF16)
    ln_all = jnp.stack([ln_g, ln_b], axis=1)
    lru_vec = jnp.stack([conv_b, lru_ba, lru_bx, lru_lambda], axis=1)
    wa_bd = _block_diag_all(lru_wa).astype(BF16)
    wx_bd = _block_diag_all(lru_wx).astype(BF16)
    mu_all = jnp.pad(rwkv_mu, ((0, 0), (0, pad_w)))[:, None, :]
    v0_all = jnp.pad(rwkv_v0, ((1, 0), (0, 0)))
    vec_all = jnp.stack([rwkv_w0, rwkv_a0, rwkv_kk, rwkv_ka, rwkv_rk.reshape(depth, C_WIDTH),
                         rwkv_gn_w, rwkv_gn_b, v0_all], axis=1)
    w2_all = jnp.pad(rwkv_w2, ((0, 0), (0, LANES - DECAY_RANK), (0, 0))).astype(BF16)
    a2_all = jnp.pad(rwkv_a2, ((0, 0), (DECAY_RANK, LANES - DECAY_RANK - AICL_RANK), (0, 0))).astype(BF16)
    v1_all = jnp.pad(rwkv_v1, ((1, 0), (0, 0), (0, LANES - VRES_RANK))).astype(BF16)
    v2_all = jnp.pad(rwkv_v2, ((1, 0), (0, LANES - VRES_RANK), (0, 0))).astype(BF16)
    consts = _rwkv_consts()

    x2 = x.reshape(n, dm)
    v_first = None
    for l in range(depth):
        pa, pb, pc = _proj(x2, w_in_p, l)
        ya = _attn(pa.reshape(bsz, seq, GA_WIDTH), attn_sinks, l)
        yb = _lru(pb.reshape(bsz, seq, GB_WIDTH), conv_w, lru_vec, wa_bd, wx_bd, l)
        yc, v_own = _rwkv(pc.reshape(bsz, seq, GC_WIDTH), v_first, mu_all, vec_all, w2_all, a2_all,
                          v1_all, v2_all, consts, l)
        if l == 0:
            v_first = v_own
        x2 = _out(ya.reshape(n, A_WIDTH), yb.reshape(n, B_WIDTH), yc.reshape(n, C_WIDTH), x2,
                  w_out_b, ln_all, alpha, l)
    return x2.reshape(bsz, seq, dm)
```

```python
import functools
import math

import jax
import jax.numpy as jnp
from jax import lax
from jax.experimental import pallas as pl
from jax.experimental.pallas import tpu as pltpu

F32 = jnp.float32
BF16 = jnp.bfloat16

D_MODEL = 1024
HEAD_DIM = 64
A_Q_HEADS = 8
A_KV_HEADS = 2
A_WIDTH = A_Q_HEADS * HEAD_DIM
A_KV_WIDTH = A_KV_HEADS * HEAD_DIM
ATT_BLOCK = 128
B_WIDTH = 256
B_BLOCKS = 4
CONV_WIDTH = 4
LRU_C = 8.0
C_HEADS = 4
C_WIDTH = C_HEADS * HEAD_DIM
DECAY_RANK = 32
AICL_RANK = 32
VRES_RANK = 16
C_SHIFT_WIDTH = 3 * C_WIDTH + DECAY_RANK + AICL_RANK
GN_EPS = 64e-5
LN_EPS = 1e-5
LOG2E = math.log2(math.e)

LANES = 128
SUBLANES = 8
C_SHIFT_PAD = ((C_SHIFT_WIDTH + LANES - 1) // LANES) * LANES
GA_WIDTH = A_WIDTH + 2 * A_KV_WIDTH + A_WIDTH
GB_WIDTH = 2 * B_WIDTH
GC_WIDTH = C_SHIFT_WIDTH + C_WIDTH
D_IN_PAD = GA_WIDTH + GB_WIDTH + GC_WIDTH
D_MIX = A_WIDTH + B_WIDTH + C_WIDTH

CHUNK = 64
STACK = C_HEADS * CHUNK
RWKV_GROUP = 8
ATT_TILE = 512
LRU_TILE = 256
PROJ_TM = 512
OUT_TM = 1024
OUT_SUB = 256
VMEM_LIMIT = 56 * 1024 * 1024


def _mm(a, b):
    return jnp.dot(a.astype(BF16), b.astype(BF16), preferred_element_type=F32)


def _mm_nt(a, b):
    return lax.dot_general(a.astype(BF16), b.astype(BF16), (((1,), (1,)), ((), ())),
                           preferred_element_type=F32)


def _mm_tn(a, b):
    return lax.dot_general(a.astype(BF16), b.astype(BF16), (((0,), (0,)), ((), ())),
                           preferred_element_type=F32)


def _split_bf16(x, parts):
    out = []
    rem = x
    for _ in range(parts):
        hi = rem.astype(BF16)
        out.append(hi)
        rem = rem - hi.astype(F32)
    return out


def _mm_exact_lhs(m_bf16, x, parts):
    n = x.shape[1]
    t = jnp.dot(m_bf16, jnp.concatenate(_split_bf16(x, parts), axis=1), preferred_element_type=F32)
    acc = t[:, 0:n]
    for p in range(1, parts):
        acc = acc + t[:, p * n:(p + 1) * n]
    return acc


def _sigmoid(x):
    return 1.0 / (1.0 + jnp.exp2(x * (-LOG2E)))


def _silu(x):
    return x * _sigmoid(x)


def _softplus(x):
    return jnp.maximum(x, 0.0) + jnp.log(1.0 + jnp.exp(-jnp.abs(x)))


def _layer_spec(shape, layer):
    return pl.BlockSpec((None,) + shape, lambda *_: (layer,) + (0,) * len(shape))


def _const_spec(shape):
    return pl.BlockSpec(shape, lambda *_: (0,) * len(shape))


def _proj_kernel(x_ref, w_ref, oa_ref, ob_ref, oc_ref):
    x = x_ref[...].astype(BF16)
    nt = (((1,), (1,)), ((), ()))
    oa_ref[...] = lax.dot_general(x, w_ref[0:GA_WIDTH, :], nt, preferred_element_type=F32).astype(BF16)
    ob_ref[...] = lax.dot_general(x, w_ref[GA_WIDTH:GA_WIDTH + GB_WIDTH, :], nt,
                                  preferred_element_type=F32)
    oc_ref[...] = lax.dot_general(x, w_ref[GA_WIDTH + GB_WIDTH:D_IN_PAD, :], nt,
                                  preferred_element_type=F32)


def _proj(x2, w_all, layer):
    n = x2.shape[0]
    return pl.pallas_call(
        _proj_kernel,
        grid=(n // PROJ_TM,),
        in_specs=[pl.BlockSpec((PROJ_TM, D_MODEL), lambda i: (i, 0)),
                  _layer_spec((D_IN_PAD, D_MODEL), layer)],
        out_specs=[pl.BlockSpec((PROJ_TM, GA_WIDTH), lambda i: (i, 0)),
                   pl.BlockSpec((PROJ_TM, GB_WIDTH), lambda i: (i, 0)),
                   pl.BlockSpec((PROJ_TM, GC_WIDTH), lambda i: (i, 0))],
        out_shape=[jax.ShapeDtypeStruct((n, GA_WIDTH), BF16),
                   jax.ShapeDtypeStruct((n, GB_WIDTH), F32),
                   jax.ShapeDtypeStruct((n, GC_WIDTH), F32)],
        compiler_params=pltpu.CompilerParams(dimension_semantics=("arbitrary",),
                                             vmem_limit_bytes=VMEM_LIMIT),
        name="proj",
    )(x2, w_all)


def _attn_kernel(layer, sink_ref, cur_ref, prev_ref, o_ref):
    n = pl.program_id(1)
    blk = ATT_BLOCK
    nsub = ATT_TILE // blk
    grp = A_Q_HEADS // A_KV_HEADS
    lo = lax.broadcasted_iota(jnp.int32, (blk, A_KV_WIDTH), 1) < HEAD_DIM

    def variants(x16):
        x = x16.astype(F32)
        xs = pltpu.roll(x, HEAD_DIM, axis=1)
        return ((jnp.where(lo, x, 0.0).astype(BF16), jnp.where(lo, 0.0, xs).astype(BF16)),
                (jnp.where(lo, xs, 0.0).astype(BF16), jnp.where(lo, 0.0, x).astype(BF16)))

    k0, v0 = A_WIDTH, A_WIDTH + A_KV_WIDTH
    g0 = A_WIDTH + 2 * A_KV_WIDTH
    kblk = [variants(prev_ref[0, :, 0:A_KV_WIDTH])]
    vblk = [variants(prev_ref[0, :, A_KV_WIDTH:2 * A_KV_WIDTH])]
    for j in range(nsub):
        kblk.append(variants(cur_ref[0, j * blk:(j + 1) * blk, k0:k0 + A_KV_WIDTH]))
        vblk.append(variants(cur_ref[0, j * blk:(j + 1) * blk, v0:v0 + A_KV_WIDTH]))

    qi = lax.broadcasted_iota(jnp.int32, (blk, 2 * blk), 0)
    kj = lax.broadcasted_iota(jnp.int32, (blk, 2 * blk), 1)
    diff = qi + blk - kj
    band = (diff >= 0) & (diff < blk)
    first = band & (kj + (n * nsub - 1) * blk >= 0)

    for j in range(nsub):
        mask = first if j == 0 else band
        rows = slice(j * blk, (j + 1) * blk)
        scores = []
        for p in range(A_Q_HEADS // 2):
            h = (2 * p) // grp
            qp = cur_ref[0, rows, p * LANES:(p + 1) * LANES]
            for half in range(2):
                kb = jnp.concatenate([kblk[j][h][half], kblk[j + 1][h][half]], axis=0)
                scores.append(lax.dot_general(qp, kb, (((1,), (1,)), ((), ())),
                                              preferred_element_type=F32))
        probs, inv = [], []
        for idx, s in enumerate(scores):
            sink2 = sink_ref[layer, idx] * LOG2E
            s = jnp.where(mask, s, -jnp.inf)
            m = jnp.maximum(jnp.max(s, axis=-1, keepdims=True), sink2)
            e = jnp.exp2(s - m)
            den = jnp.sum(e, axis=-1, keepdims=True) + jnp.exp2(sink2 - m)
            probs.append(e.astype(BF16))
            inv.append(1.0 / den)
        for p in range(A_Q_HEADS // 2):
            h = (2 * p) // grp
            acc = None
            for half in range(2):
                vb = jnp.concatenate([vblk[j][h][half], vblk[j + 1][h][half]], axis=0)
                o = jnp.dot(probs[2 * p + half], vb, preferred_element_type=F32) * inv[2 * p + half]
                acc = o if acc is None else acc + o
            g = cur_ref[0, rows, g0 + p * LANES:g0 + (p + 1) * LANES].astype(F32)
            o_ref[0, rows, p * LANES:(p + 1) * LANES] = (acc * _silu(g)).astype(BF16)


def _attn(pa, sinks, layer):
    bsz, seq, _ = pa.shape
    nsub = ATT_TILE // ATT_BLOCK
    kv_blk = A_WIDTH // (2 * A_KV_WIDTH)
    return pl.pallas_call(
        functools.partial(_attn_kernel, layer),
        grid=(bsz, seq // ATT_TILE),
        in_specs=[pl.BlockSpec(memory_space=pltpu.SMEM),
                  pl.BlockSpec((1, ATT_TILE, GA_WIDTH), lambda b, n: (b, n, 0)),
                  pl.BlockSpec((1, ATT_BLOCK, 2 * A_KV_WIDTH),
                               lambda b, n: (b, jnp.maximum(n * nsub - 1, 0), kv_blk))],
        out_specs=pl.BlockSpec((1, ATT_TILE, A_WIDTH), lambda b, n: (b, n, 0)),
        out_shape=jax.ShapeDtypeStruct((bsz, seq, A_WIDTH), BF16),
        compiler_params=pltpu.CompilerParams(dimension_semantics=("arbitrary", "arbitrary"),
                                             vmem_limit_bytes=VMEM_LIMIT),
        name="attn",
    )(sinks, pa, pa)


def _lru_kernel(pb_ref, cw_ref, vec_ref, wa_ref, wx_ref, o_ref):
    seq = pb_ref.shape[1]
    tile = LRU_TILE
    ngrp = tile // SUBLANES
    conv_b = vec_ref[0:1, :]
    ba = vec_ref[1:2, :]
    bx = vec_ref[2:3, :]
    lam = vec_ref[3:4, :]
    neg_c_sp = -LRU_C * _softplus(-lam)

    def body(i, hc):
        t0 = pl.multiple_of(i * tile, tile)
        x = pb_ref[0, pl.ds(t0, tile), 0:B_WIDTH]
        tp = pl.multiple_of(jnp.maximum(t0 - SUBLANES, 0), SUBLANES)
        xp = pb_ref[0, pl.ds(tp, SUBLANES), 0:B_WIDTH]
        xp = jnp.where(i > 0, xp, 0.0)
        xe = jnp.concatenate([xp, x], axis=0)
        xc = cw_ref[CONV_WIDTH - 1:CONV_WIDTH, :] * x + conv_b
        for j in range(1, CONV_WIDTH):
            xs = pltpu.roll(xe, j, axis=0)[SUBLANES:SUBLANES + tile]
            xc = xc + cw_ref[CONV_WIDTH - 1 - j:CONV_WIDTH - j, :] * xs
        r = _sigmoid(_mm(xc, wa_ref[...]) + ba)
        ig = _sigmoid(_mm(xc, wx_ref[...]) + bx)
        a = jnp.exp(neg_c_sp * r)
        u = jnp.sqrt(1.0 - a * a) * (ig * xc)
        a3 = a.reshape(ngrp, SUBLANES, B_WIDTH)
        u3 = u.reshape(ngrp, SUBLANES, B_WIDTH)
        sub = lax.broadcasted_iota(jnp.int32, a3.shape, 1)
        d = 1
        while d < SUBLANES:
            keep = sub >= d
            u3 = a3 * jnp.where(keep, pltpu.roll(u3, d, axis=1), 0.0) + u3
            a3 = a3 * jnp.where(keep, pltpu.roll(a3, d, axis=1), 1.0)
            d *= 2
        hs = []
        carry = hc
        for gi in range(ngrp):
            hg = u3[gi] + a3[gi] * carry
            hs.append(hg)
            carry = hg[SUBLANES - 1:SUBLANES, :]
        h = jnp.concatenate(hs, axis=0)
        g = pb_ref[0, pl.ds(t0, tile), B_WIDTH:2 * B_WIDTH]
        o_ref[0, pl.ds(t0, tile), :] = (h * _silu(g)).astype(BF16)
        return carry

    lax.fori_loop(0, seq // tile, body, jnp.zeros((1, B_WIDTH), F32))


def _lru(pb, cw_all, vec_all, wa_all, wx_all, layer):
    bsz, seq, _ = pb.shape
    return pl.pallas_call(
        _lru_kernel,
        grid=(bsz,),
        in_specs=[pl.BlockSpec((1, seq, GB_WIDTH), lambda b: (b, 0, 0)),
                  _layer_spec((CONV_WIDTH, B_WIDTH), layer), _layer_spec((4, B_WIDTH), layer),
                  _layer_spec((B_WIDTH, B_WIDTH), layer), _layer_spec((B_WIDTH, B_WIDTH), layer)],
        out_specs=pl.BlockSpec((1, seq, B_WIDTH), lambda b: (b, 0, 0)),
        out_shape=jax.ShapeDtypeStruct((bsz, seq, B_WIDTH), BF16),
        compiler_params=pltpu.CompilerParams(dimension_semantics=("arbitrary",),
                                             vmem_limit_bytes=VMEM_LIMIT),
        name="lru",
    )(pb, cw_all, vec_all, wa_all, wx_all)


def _stack_heads(x):
    x16 = x.astype(BF16)
    head = lax.broadcasted_iota(jnp.int32, x16.shape, 1) // HEAD_DIM
    return jnp.concatenate([jnp.where(head == h, x16, jnp.zeros_like(x16)) for h in range(C_HEADS)],
                           axis=0)


def _unstack_heads(x):
    head = lax.broadcasted_iota(jnp.int32, (CHUNK, STACK), 1) // HEAD_DIM
    out = x[(C_HEADS - 1) * CHUNK:C_HEADS * CHUNK]
    for h in range(C_HEADS - 2, -1, -1):
        out = jnp.where(head == h, x[h * CHUNK:(h + 1) * CHUNK], out)
    return out


def _rwkv_pre(cs, prev, vfirst, mu, vec, w2p, a2p, v1p, v2p, ones_bd, tril_l, sl, le, eye_w):
    w0, a0, k_k, k_a, r_k, _, _, v0 = (vec[i:i + 1, :] for i in range(8))
    xs = cs + (prev - cs) * mu
    r = xs[:, 0:C_WIDTH]
    k = xs[:, C_WIDTH:2 * C_WIDTH]
    v = xs[:, 2 * C_WIDTH:3 * C_WIDTH]
    la = xs[:, 3 * C_WIDTH:C_SHIFT_PAD]
    yield
    lw = (-LOG2E * math.exp(-0.5)) * _sigmoid(w0 + _mm(jnp.tanh(la), w2p))
    a = _sigmoid(a0 + _mm(la, a2p))
    v_own = v
    if vfirst is not None:
        v = v + (vfirst - v) * _sigmoid(v0 + _mm(_mm(v, v1p), v2p))
    yield
    kk = k * k_k
    kap = kk * lax.rsqrt(_mm(kk * kk, ones_bd) + 1e-12)
    k2 = k * (1.0 + (a - 1.0) * k_a)
    b = kap * a

    yield
    cum = _mm_exact_lhs(tril_l, lw, 2)
    cum_end = cum[CHUNK - 1:CHUNK, :]
    e_in = jnp.exp2(cum)
    e_ex = jnp.exp2(cum - lw)
    e_neg = jnp.exp2(-cum)
    e_end = jnp.exp2(cum_end - cum)
    p_end = jnp.exp2(cum_end)

    yield
    kt = kap * e_ex
    rt = r * e_in
    kt_st = _stack_heads(kt)
    vs_st = _stack_heads(v)
    gram = _mm_nt(jnp.concatenate([kt, rt], axis=0),
                  jnp.concatenate([_stack_heads(b * e_neg), _stack_heads(k2 * e_neg)], axis=0))
    a_ab = jnp.where(sl, gram[0:CHUNK, 0:STACK], 0.0)
    a_ak = jnp.where(sl, gram[0:CHUNK, STACK:2 * STACK], 0.0)
    a_rb = jnp.where(le, gram[CHUNK:2 * CHUNK, 0:STACK], 0.0)
    a_rk = jnp.where(le, gram[CHUNK:2 * CHUNK, STACK:2 * STACK], 0.0)

    yield
    pw = -a_ab
    t_w = eye_w + pw
    av = _mm(jnp.concatenate([a_ak, a_rk], axis=0), vs_st)
    pw = _mm(pw, _stack_heads(pw))
    for _ in range(1, int(math.log2(CHUNK)) - 1):
        yield
        both = _mm(jnp.concatenate([t_w, pw], axis=0), _stack_heads(pw))
        t_w = t_w + both[0:CHUNK]
        pw = both[CHUNK:2 * CHUNK]
    yield
    t_w = t_w + _mm(t_w, _stack_heads(pw))
    yield
    tx = _mm(t_w, jnp.concatenate([kt_st, _stack_heads(av[0:CHUNK])], axis=1))
    yield
    m1 = tx[:, 0:STACK]
    c1 = tx[:, STACK:2 * STACK]
    ab = _mm(a_rb, jnp.concatenate([_stack_heads(m1), _stack_heads(c1)], axis=1))
    q_n = rt - ab[:, 0:STACK]
    d_n = av[CHUNK:2 * CHUNK] - ab[:, STACK:2 * STACK]
    bh = b * e_end
    g_w = eye_w * p_end - _unstack_heads(_mm_tn(bh, m1))
    c_w = _unstack_heads(_mm_tn(jnp.concatenate([k2 * e_end, bh], axis=0),
                                jnp.concatenate([v, -c1], axis=0)))
    bonus_arg = r * k2 * r_k
    return q_n, d_n, g_w, c_w, bonus_arg, v, v_own


RWKV_SAVED = 6


def _rwkv_kernel(use_vres, *refs):
    if use_vres:
        (pc_ref, vf_ref, mu_ref, vec_ref, w2_ref, a2_ref, v1_ref, v2_ref, tril_ref, bd_ref,
         sl_ref, le_ref, eye_ref, y_ref, h_ref, pre_ref) = refs
        vo_ref = None
    else:
        (pc_ref, mu_ref, vec_ref, w2_ref, a2_ref, v1_ref, v2_ref, tril_ref, bd_ref,
         sl_ref, le_ref, eye_ref, y_ref, vo_ref, h_ref, pre_ref) = refs
        vf_ref = None
    seq = pc_ref.shape[1]
    ngroups = seq // (CHUNK * RWKV_GROUP)
    h_ref[...] = jnp.zeros_like(h_ref)

    def chunk_start(gi, j):
        return pl.multiple_of((gi * RWKV_GROUP + j) * CHUNK, CHUNK)

    def pre_generators(gi, first_group):
        sl = sl_ref[...] > 0.0
        le = le_ref[...] > 0.0
        gens = []
        for j in range(RWKV_GROUP):
            t0 = chunk_start(gi, j)
            cs = pc_ref[0, pl.ds(t0, CHUNK), 0:C_SHIFT_PAD]
            if first_group and j == 0:
                row = lax.broadcasted_iota(jnp.int32, cs.shape, 0)
                prev = jnp.where(row == 0, 0.0, pltpu.roll(cs, 1, axis=0))
            else:
                tp = pl.multiple_of(t0 - SUBLANES, SUBLANES)
                ext = pc_ref[0, pl.ds(tp, CHUNK + SUBLANES), 0:C_SHIFT_PAD]
                prev = pltpu.roll(ext, 1, axis=0)[SUBLANES:SUBLANES + CHUNK]
            vfirst = vf_ref[0, pl.ds(t0, CHUNK), :] if use_vres else None
            gens.append(_rwkv_pre(cs, prev, vfirst, mu_ref[...], vec_ref[...], w2_ref[...],
                                  a2_ref[...], v1_ref[...], v2_ref[...], bd_ref[...], tril_ref[...],
                                  sl, le, eye_ref[...]))
        return gens

    def chain_tasks(gi):
        ys = [None] * RWKV_GROUP
        stats = [None] * RWKV_GROUP
        state = {}

        def step(j):
            def run():
                h_w = h_ref[...] if j == 0 else state["h"]
                q_n, d_n, g_w, c_w = (pre_ref[j * RWKV_SAVED + k] for k in range(4))
                res = _mm(jnp.concatenate([q_n, g_w], axis=0), _stack_heads(h_w))
                ys[j] = res[0:CHUNK] + d_n
                state["h"] = res[CHUNK:2 * CHUNK] + c_w
                if j == RWKV_GROUP - 1:
                    h_ref[...] = state["h"]
            return run

        def sums():
            for j in range(RWKV_GROUP):
                parts = _split_bf16(ys[j], 2) + [pre_ref[j * RWKV_SAVED + 4].astype(BF16)]
                s = jnp.dot(jnp.concatenate(parts, axis=0), bd_ref[...], preferred_element_type=F32)
                yc = ys[j] - (s[0:CHUNK] + s[CHUNK:2 * CHUNK]) * (1.0 / HEAD_DIM)
                stats[j] = (yc, s[2 * CHUNK:3 * CHUNK])

        def variance():
            for j in range(RWKV_GROUP):
                yc, bonus = stats[j]
                yv = _mm(yc * yc, bd_ref[...]) * (1.0 / HEAD_DIM)
                stats[j] = (yc, bonus, yv)

        def finish():
            gn_w = vec_ref[5:6, :]
            gn_b = vec_ref[6:7, :]
            for j in range(RWKV_GROUP):
                t0 = chunk_start(gi, j)
                yc, bonus, yv = stats[j]
                yn = yc * lax.rsqrt(yv + GN_EPS) * gn_w + gn_b
                g = pc_ref[0, pl.ds(t0, CHUNK), C_SHIFT_WIDTH:GC_WIDTH]
                out = (yn + bonus * pre_ref[j * RWKV_SAVED + 5]) * _silu(g)
                y_ref[0, pl.ds(t0, CHUNK), :] = out.astype(BF16)

        return [step(j) for j in range(RWKV_GROUP)] + [sums, variance, finish]

    def run_group(gi, first_group, tasks):
        gens = pre_generators(gi, first_group)
        tasks = list(tasks)
        pre = [None] * RWKV_GROUP
        sweep = 0
        while any(p is None for p in pre):
            for j, gen in enumerate(gens):
                if pre[j] is None:
                    try:
                        next(gen)
                    except StopIteration as stop:
                        pre[j] = stop.value
            sweep += 1
            if tasks and sweep >= 2:
                tasks.pop(0)()
        for t in tasks:
            t()
        for j in range(RWKV_GROUP):
            for k in range(RWKV_SAVED):
                pre_ref[j * RWKV_SAVED + k] = pre[j][k]
            if not use_vres:
                vo_ref[0, pl.ds(chunk_start(gi, j), CHUNK), :] = pre[j][RWKV_SAVED]

    run_group(0, True, [])

    def body(i, carry):
        run_group(i, False, chain_tasks(i - 1))
        return carry

    lax.fori_loop(1, ngroups, body, 0)
    for t in chain_tasks(ngroups - 1):
        t()


def _rwkv_consts():
    idx = jnp.arange(STACK)
    same = (idx[:, None] // CHUNK) == (idx[None, :] // CHUNK)
    ci = jnp.arange(CHUNK)
    tril_l = (ci[None, :] <= ci[:, None]).astype(BF16)
    s_loc = idx[None, :] % CHUNK
    sl = (s_loc < ci[:, None]).astype(F32)
    le = (s_loc <= ci[:, None]).astype(F32)
    eye_w = (s_loc == ci[:, None]).astype(F32)
    return tril_l, same.astype(BF16), sl, le, eye_w


def _rwkv(pc, vfirst, mu_all, vec_all, w2_all, a2_all, v1_all, v2_all, consts, layer):
    bsz, seq, _ = pc.shape
    use_vres = vfirst is not None
    sq = (STACK, STACK)
    wide = (CHUNK, STACK)
    seq_spec = lambda w: pl.BlockSpec((1, seq, w), lambda b: (b, 0, 0))
    in_specs = [seq_spec(GC_WIDTH)] + ([seq_spec(C_WIDTH)] if use_vres else []) + [
        _layer_spec((1, C_SHIFT_PAD), layer), _layer_spec((8, C_WIDTH), layer),
        _layer_spec((LANES, C_WIDTH), layer), _layer_spec((LANES, C_WIDTH), layer),
        _layer_spec((C_WIDTH, LANES), layer), _layer_spec((LANES, C_WIDTH), layer),
        _const_spec((CHUNK, CHUNK)), _const_spec(sq),
        _const_spec(wide), _const_spec(wide), _const_spec(wide)]
    out_specs = [seq_spec(C_WIDTH)]
    out_shape = [jax.ShapeDtypeStruct((bsz, seq, C_WIDTH), BF16)]
    if not use_vres:
        out_specs.append(seq_spec(C_WIDTH))
        out_shape.append(jax.ShapeDtypeStruct((bsz, seq, C_WIDTH), F32))
    args = (pc,) + ((vfirst,) if use_vres else ()) + (mu_all, vec_all, w2_all, a2_all, v1_all, v2_all)
    outs = pl.pallas_call(
        functools.partial(_rwkv_kernel, use_vres),
        grid=(bsz,),
        in_specs=in_specs,
        out_specs=out_specs,
        out_shape=out_shape,
        scratch_shapes=[pltpu.VMEM(wide, F32), pltpu.VMEM((RWKV_GROUP * RWKV_SAVED,) + wide, F32)],
        compiler_params=pltpu.CompilerParams(dimension_semantics=("arbitrary",),
                                             vmem_limit_bytes=VMEM_LIMIT),
        name="rwkv",
    )(*args, *consts)
    return (outs[0], None) if use_vres else (outs[0], outs[1])


def _out_kernel(alpha, ya_ref, yb_ref, yc_ref, x_ref, w_ref, ln_ref, o_ref):
    def project(rows):
        acc = jnp.dot(ya_ref[rows, :], w_ref[0:A_WIDTH, :], preferred_element_type=F32)
        acc = acc + jnp.dot(yb_ref[rows, :], w_ref[A_WIDTH:A_WIDTH + B_WIDTH, :],
                            preferred_element_type=F32)
        return acc + jnp.dot(yc_ref[rows, :], w_ref[A_WIDTH + B_WIDTH:D_MIX, :],
                             preferred_element_type=F32)

    def norm(rows, acc):
        z = alpha * x_ref[rows, :] + acc
        mu = jnp.mean(z, axis=-1, keepdims=True)
        zc = z - mu
        var = jnp.mean(zc * zc, axis=-1, keepdims=True)
        o_ref[rows, :] = zc * lax.rsqrt(var + LN_EPS) * ln_ref[0:1, :] + ln_ref[1:2, :]

    subs = [slice(s * OUT_SUB, (s + 1) * OUT_SUB) for s in range(OUT_TM // OUT_SUB)]
    acc_prev = project(subs[0])
    for s in range(1, len(subs)):
        acc_next = project(subs[s])
        norm(subs[s - 1], acc_prev)
        acc_prev = acc_next
    norm(subs[-1], acc_prev)


def _out(ya, yb, yc, x2, w_all, ln_all, alpha, layer):
    n = x2.shape[0]
    row = lambda w: pl.BlockSpec((OUT_TM, w), lambda i: (i, 0))
    return pl.pallas_call(
        functools.partial(_out_kernel, alpha),
        grid=(n // OUT_TM,),
        in_specs=[row(A_WIDTH), row(B_WIDTH), row(C_WIDTH), row(D_MODEL),
                  _layer_spec((D_MIX, D_MODEL), layer), _layer_spec((2, D_MODEL), layer)],
        out_specs=row(D_MODEL),
        out_shape=jax.ShapeDtypeStruct((n, D_MODEL), F32),
        compiler_params=pltpu.CompilerParams(dimension_semantics=("arbitrary",),
                                             vmem_limit_bytes=VMEM_LIMIT),
        name="outproj",
    )(ya, yb, yc, x2, w_all, ln_all)


def _block_diag_all(w):
    depth, nblk, d, _ = w.shape
    eye = jnp.eye(nblk, dtype=w.dtype)
    return (w[:, :, :, None, :] * eye[None, :, None, :, None]).reshape(depth, nblk * d, nblk * d)


def kernel(x, w_in, w_out, ln_g, ln_b, attn_sinks, conv_w, conv_b, lru_wa, lru_ba, lru_wx, lru_bx,
           lru_lambda, rwkv_mu, rwkv_w0, rwkv_w2, rwkv_a0, rwkv_a2, rwkv_kk, rwkv_ka, rwkv_rk,
           rwkv_gn_w, rwkv_gn_b, rwkv_v0, rwkv_v1, rwkv_v2):
    bsz, seq, dm = x.shape
    depth = w_in.shape[0]
    alpha = (2 * depth) ** 0.25
    n = bsz * seq
    pad_w = C_SHIFT_PAD - C_SHIFT_WIDTH

    q_scale = jnp.concatenate([jnp.full((A_WIDTH,), HEAD_DIM ** -0.5 * LOG2E, F32),
                               jnp.ones((w_in.shape[2] - A_WIDTH,), F32)])
    w_in_p = (jnp.swapaxes(w_in, 1, 2) * q_scale[None, :, None]).astype(BF16)
    w_out_b = w_out.astype(BF16)
    ln_all = jnp.stack([ln_g, ln_b], axis=1)
    lru_vec = jnp.stack([conv_b, lru_ba, lru_bx, lru_lambda], axis=1)
    wa_bd = _block_diag_all(lru_wa).astype(BF16)
    wx_bd = _block_diag_all(lru_wx).astype(BF16)
    mu_all = jnp.pad(rwkv_mu, ((0, 0), (0, pad_w)))[:, None, :]
    v0_all = jnp.pad(rwkv_v0, ((1, 0), (0, 0)))
    vec_all = jnp.stack([rwkv_w0, rwkv_a0, rwkv_kk, rwkv_ka, rwkv_rk.reshape(depth, C_WIDTH),
                         rwkv_gn_w, rwkv_gn_b, v0_all], axis=1)
    w2_all = jnp.pad(rwkv_w2, ((0, 0), (0, LANES - DECAY_RANK), (0, 0))).astype(BF16)
    a2_all = jnp.pad(rwkv_a2, ((0, 0), (DECAY_RANK, LANES - DECAY_RANK - AICL_RANK), (0, 0))).astype(BF16)
    v1_all = jnp.pad(rwkv_v1, ((1, 0), (0, 0), (0, LANES - VRES_RANK))).astype(BF16)
    v2_all = jnp.pad(rwkv_v2, ((1, 0), (0, LANES - VRES_RANK), (0, 0))).astype(BF16)
    consts = _rwkv_consts()

    x2 = x.reshape(n, dm)
    v_first = None
    for l in range(depth):
        pa, pb, pc = _proj(x2, w_in_p, l)
        ya = _attn(pa.reshape(bsz, seq, GA_WIDTH), attn_sinks, l)
        yb = _lru(pb.reshape(bsz, seq, GB_WIDTH), conv_w, lru_vec, wa_bd, wx_bd, l)
        yc, v_own = _rwkv(pc.reshape(bsz, seq, GC_WIDTH), v_first, mu_all, vec_all, w2_all, a2_all,
                          v1_all, v2_all, consts, l)
        if l == 0:
            v_first = v_own
        x2 = _out(ya.reshape(n, A_WIDTH), yb.reshape(n, B_WIDTH), yc.reshape(n, C_WIDTH), x2,
                  w_out_b, ln_all, alpha, l)
    return x2.reshape(bsz, seq, dm)
```

```python
import functools
import math

import jax
import jax.numpy as jnp
from jax import lax
from jax.experimental import pallas as pl
from jax.experimental.pallas import tpu as pltpu

F32 = jnp.float32
BF16 = jnp.bfloat16

D_MODEL = 1024
HEAD_DIM = 64
A_Q_HEADS = 8
A_KV_HEADS = 2
A_WIDTH = A_Q_HEADS * HEAD_DIM
A_KV_WIDTH = A_KV_HEADS * HEAD_DIM
ATT_BLOCK = 128
B_WIDTH = 256
B_BLOCKS = 4
CONV_WIDTH = 4
LRU_C = 8.0
C_HEADS = 4
C_WIDTH = C_HEADS * HEAD_DIM
DECAY_RANK = 32
AICL_RANK = 32
VRES_RANK = 16
C_SHIFT_WIDTH = 3 * C_WIDTH + DECAY_RANK + AICL_RANK
GN_EPS = 64e-5
LN_EPS = 1e-5
LOG2E = math.log2(math.e)

LANES = 128
SUBLANES = 8
C_SHIFT_PAD = ((C_SHIFT_WIDTH + LANES - 1) // LANES) * LANES
GA_WIDTH = A_WIDTH + 2 * A_KV_WIDTH + A_WIDTH
GB_WIDTH = 2 * B_WIDTH
GC_WIDTH = C_SHIFT_WIDTH + C_WIDTH
D_IN_PAD = GA_WIDTH + GB_WIDTH + GC_WIDTH
D_MIX = A_WIDTH + B_WIDTH + C_WIDTH

CHUNK = 64
STACK = C_HEADS * CHUNK
RWKV_GROUP = 8
ATT_TILE = 1024
LRU_TILE = 256
PROJ_TM = 512
OUT_TM = 1024
FUSED_TM = 512
OUT_SUB = 256
VMEM_LIMIT = 56 * 1024 * 1024


def _mm(a, b):
    return jnp.dot(a.astype(BF16), b.astype(BF16), preferred_element_type=F32)


def _mm_nt(a, b):
    return lax.dot_general(a.astype(BF16), b.astype(BF16), (((1,), (1,)), ((), ())),
                           preferred_element_type=F32)


def _mm_tn(a, b):
    return lax.dot_general(a.astype(BF16), b.astype(BF16), (((0,), (0,)), ((), ())),
                           preferred_element_type=F32)


def _split_bf16(x, parts):
    out = []
    rem = x
    for _ in range(parts):
        hi = rem.astype(BF16)
        out.append(hi)
        rem = rem - hi.astype(F32)
    return out


def _mm_exact_lhs(m_bf16, x, parts):
    n = x.shape[1]
    t = jnp.dot(m_bf16, jnp.concatenate(_split_bf16(x, parts), axis=1), preferred_element_type=F32)
    acc = t[:, 0:n]
    for p in range(1, parts):
        acc = acc + t[:, p * n:(p + 1) * n]
    return acc


def _sigmoid(x):
    return 1.0 / (1.0 + jnp.exp2(x * (-LOG2E)))


def _silu(x):
    return x * _sigmoid(x)


def _softplus(x):
    return jnp.maximum(x, 0.0) + jnp.log(1.0 + jnp.exp(-jnp.abs(x)))


def _layer_spec(shape, layer):
    return pl.BlockSpec((None,) + shape, lambda *_: (layer,) + (0,) * len(shape))


def _const_spec(shape):
    return pl.BlockSpec(shape, lambda *_: (0,) * len(shape))


def _proj_rows(x16, w_ref, rows, oa_ref, ob_ref, oc_ref):
    nt = (((1,), (1,)), ((), ()))
    oa_ref[rows, :] = lax.dot_general(x16, w_ref[0:GA_WIDTH, :], nt,
                                      preferred_element_type=F32).astype(BF16)
    ob_ref[rows, :] = lax.dot_general(x16, w_ref[GA_WIDTH:GA_WIDTH + GB_WIDTH, :], nt,
                                      preferred_element_type=F32)
    oc_ref[rows, :] = lax.dot_general(x16, w_ref[GA_WIDTH + GB_WIDTH:D_IN_PAD, :], nt,
                                      preferred_element_type=F32)


def _proj_kernel(x_ref, w_ref, oa_ref, ob_ref, oc_ref):
    _proj_rows(x_ref[...].astype(BF16), w_ref, slice(None), oa_ref, ob_ref, oc_ref)


def _proj(x2, w_all, layer):
    n = x2.shape[0]
    return pl.pallas_call(
        _proj_kernel,
        grid=(n // PROJ_TM,),
        in_specs=[pl.BlockSpec((PROJ_TM, D_MODEL), lambda i: (i, 0)),
                  _layer_spec((D_IN_PAD, D_MODEL), layer)],
        out_specs=[pl.BlockSpec((PROJ_TM, GA_WIDTH), lambda i: (i, 0)),
                   pl.BlockSpec((PROJ_TM, GB_WIDTH), lambda i: (i, 0)),
                   pl.BlockSpec((PROJ_TM, GC_WIDTH), lambda i: (i, 0))],
        out_shape=[jax.ShapeDtypeStruct((n, GA_WIDTH), BF16),
                   jax.ShapeDtypeStruct((n, GB_WIDTH), F32),
                   jax.ShapeDtypeStruct((n, GC_WIDTH), F32)],
        compiler_params=pltpu.CompilerParams(dimension_semantics=("arbitrary",),
                                             vmem_limit_bytes=VMEM_LIMIT),
        name="proj",
    )(x2, w_all)


def _attn_kernel(layer, sink_ref, cur_ref, prev_ref, o_ref):
    n = pl.program_id(1)
    blk = ATT_BLOCK
    nsub = ATT_TILE // blk
    grp = A_Q_HEADS // A_KV_HEADS
    lo = lax.broadcasted_iota(jnp.int32, (blk, A_KV_WIDTH), 1) < HEAD_DIM

    def variants(x16):
        x = x16.astype(F32)
        xs = pltpu.roll(x, HEAD_DIM, axis=1)
        return ((jnp.where(lo, x, 0.0).astype(BF16), jnp.where(lo, 0.0, xs).astype(BF16)),
                (jnp.where(lo, xs, 0.0).astype(BF16), jnp.where(lo, 0.0, x).astype(BF16)))

    k0, v0 = A_WIDTH, A_WIDTH + A_KV_WIDTH
    g0 = A_WIDTH + 2 * A_KV_WIDTH
    kblk = [variants(prev_ref[0, :, 0:A_KV_WIDTH])]
    vblk = [variants(prev_ref[0, :, A_KV_WIDTH:2 * A_KV_WIDTH])]
    for j in range(nsub):
        kblk.append(variants(cur_ref[0, j * blk:(j + 1) * blk, k0:k0 + A_KV_WIDTH]))
        vblk.append(variants(cur_ref[0, j * blk:(j + 1) * blk, v0:v0 + A_KV_WIDTH]))

    qi = lax.broadcasted_iota(jnp.int32, (blk, 2 * blk), 0)
    kj = lax.broadcasted_iota(jnp.int32, (blk, 2 * blk), 1)
    diff = qi + blk - kj
    band = (diff >= 0) & (diff < blk)
    first = band & (kj + (n * nsub - 1) * blk >= 0)

    for j in range(nsub):
        mask = first if j == 0 else band
        rows = slice(j * blk, (j + 1) * blk)
        scores = []
        for p in range(A_Q_HEADS // 2):
            h = (2 * p) // grp
            qp = cur_ref[0, rows, p * LANES:(p + 1) * LANES]
            for half in range(2):
                kb = jnp.concatenate([kblk[j][h][half], kblk[j + 1][h][half]], axis=0)
                scores.append(lax.dot_general(qp, kb, (((1,), (1,)), ((), ())),
                                              preferred_element_type=F32))
        probs, inv = [], []
        for idx, s in enumerate(scores):
            sink2 = sink_ref[layer, idx] * LOG2E
            s = jnp.where(mask, s, -jnp.inf)
            m = jnp.maximum(jnp.max(s, axis=-1, keepdims=True), sink2)
            e = jnp.exp2(s - m)
            den = jnp.sum(e, axis=-1, keepdims=True) + jnp.exp2(sink2 - m)
            probs.append(e.astype(BF16))
            inv.append(1.0 / den)
        for p in range(A_Q_HEADS // 2):
            h = (2 * p) // grp
            acc = None
            for half in range(2):
                vb = jnp.concatenate([vblk[j][h][half], vblk[j + 1][h][half]], axis=0)
                o = jnp.dot(probs[2 * p + half], vb, preferred_element_type=F32) * inv[2 * p + half]
                acc = o if acc is None else acc + o
            g = cur_ref[0, rows, g0 + p * LANES:g0 + (p + 1) * LANES].astype(F32)
            o_ref[0, rows, p * LANES:(p + 1) * LANES] = (acc * _silu(g)).astype(BF16)


def _attn(pa, sinks, layer):
    bsz, seq, _ = pa.shape
    nsub = ATT_TILE // ATT_BLOCK
    kv_blk = A_WIDTH // (2 * A_KV_WIDTH)
    return pl.pallas_call(
        functools.partial(_attn_kernel, layer),
        grid=(bsz, seq // ATT_TILE),
        in_specs=[pl.BlockSpec(memory_space=pltpu.SMEM),
                  pl.BlockSpec((1, ATT_TILE, GA_WIDTH), lambda b, n: (b, n, 0)),
                  pl.BlockSpec((1, ATT_BLOCK, 2 * A_KV_WIDTH),
                               lambda b, n: (b, jnp.maximum(n * nsub - 1, 0), kv_blk))],
        out_specs=pl.BlockSpec((1, ATT_TILE, A_WIDTH), lambda b, n: (b, n, 0)),
        out_shape=jax.ShapeDtypeStruct((bsz, seq, A_WIDTH), BF16),
        compiler_params=pltpu.CompilerParams(dimension_semantics=("arbitrary", "arbitrary"),
                                             vmem_limit_bytes=VMEM_LIMIT),
        name="attn",
    )(sinks, pa, pa)


def _lru_kernel(pb_ref, cw_ref, vec_ref, wa_ref, wx_ref, o_ref):
    seq = pb_ref.shape[1]
    tile = LRU_TILE
    ngrp = tile // SUBLANES
    conv_b = vec_ref[0:1, :]
    ba = vec_ref[1:2, :]
    bx = vec_ref[2:3, :]
    lam = vec_ref[3:4, :]
    neg_c_sp = -LRU_C * _softplus(-lam)

    def body(i, hc):
        t0 = pl.multiple_of(i * tile, tile)
        x = pb_ref[0, pl.ds(t0, tile), 0:B_WIDTH]
        tp = pl.multiple_of(jnp.maximum(t0 - SUBLANES, 0), SUBLANES)
        xp = pb_ref[0, pl.ds(tp, SUBLANES), 0:B_WIDTH]
        xp = jnp.where(i > 0, xp, 0.0)
        xe = jnp.concatenate([xp, x], axis=0)
        xc = cw_ref[CONV_WIDTH - 1:CONV_WIDTH, :] * x + conv_b
        for j in range(1, CONV_WIDTH):
            xs = pltpu.roll(xe, j, axis=0)[SUBLANES:SUBLANES + tile]
            xc = xc + cw_ref[CONV_WIDTH - 1 - j:CONV_WIDTH - j, :] * xs
        r = _sigmoid(_mm(xc, wa_ref[...]) + ba)
        ig = _sigmoid(_mm(xc, wx_ref[...]) + bx)
        a = jnp.exp(neg_c_sp * r)
        u = jnp.sqrt(1.0 - a * a) * (ig * xc)
        a3 = a.reshape(ngrp, SUBLANES, B_WIDTH)
        u3 = u.reshape(ngrp, SUBLANES, B_WIDTH)
        sub = lax.broadcasted_iota(jnp.int32, a3.shape, 1)
        d = 1
        while d < SUBLANES:
            keep = sub >= d
            u3 = a3 * jnp.where(keep, pltpu.roll(u3, d, axis=1), 0.0) + u3
            a3 = a3 * jnp.where(keep, pltpu.roll(a3, d, axis=1), 1.0)
            d *= 2
        hs = []
        carry = hc
        for gi in range(ngrp):
            hg = u3[gi] + a3[gi] * carry
            hs.append(hg)
            carry = hg[SUBLANES - 1:SUBLANES, :]
        h = jnp.concatenate(hs, axis=0)
        g = pb_ref[0, pl.ds(t0, tile), B_WIDTH:2 * B_WIDTH]
        o_ref[0, pl.ds(t0, tile), :] = (h * _silu(g)).astype(BF16)
        return carry

    lax.fori_loop(0, seq // tile, body, jnp.zeros((1, B_WIDTH), F32))


def _lru(pb, cw_all, vec_all, wa_all, wx_all, layer):
    bsz, seq, _ = pb.shape
    return pl.pallas_call(
        _lru_kernel,
        grid=(bsz,),
        in_specs=[pl.BlockSpec((1, seq, GB_WIDTH), lambda b: (b, 0, 0)),
                  _layer_spec((CONV_WIDTH, B_WIDTH), layer), _layer_spec((4, B_WIDTH), layer),
                  _layer_spec((B_WIDTH, B_WIDTH), layer), _layer_spec((B_WIDTH, B_WIDTH), layer)],
        out_specs=pl.BlockSpec((1, seq, B_WIDTH), lambda b: (b, 0, 0)),
        out_shape=jax.ShapeDtypeStruct((bsz, seq, B_WIDTH), BF16),
        compiler_params=pltpu.CompilerParams(dimension_semantics=("arbitrary",),
                                             vmem_limit_bytes=VMEM_LIMIT),
        name="lru",
    )(pb, cw_all, vec_all, wa_all, wx_all)


def _stack_heads(x):
    x16 = x.astype(BF16)
    head = lax.broadcasted_iota(jnp.int32, x16.shape, 1) // HEAD_DIM
    return jnp.concatenate([jnp.where(head == h, x16, jnp.zeros_like(x16)) for h in range(C_HEADS)],
                           axis=0)


def _unstack_heads(x):
    head = lax.broadcasted_iota(jnp.int32, (CHUNK, STACK), 1) // HEAD_DIM
    out = x[(C_HEADS - 1) * CHUNK:C_HEADS * CHUNK]
    for h in range(C_HEADS - 2, -1, -1):
        out = jnp.where(head == h, x[h * CHUNK:(h + 1) * CHUNK], out)
    return out


def _rwkv_pre(cs, prev, vfirst, mu, vec, w2p, a2p, v1p, v2p, ones_bd, tril_l, sl, le, eye_w):
    w0, a0, k_k, k_a, r_k, _, _, v0 = (vec[i:i + 1, :] for i in range(8))
    xs = cs + (prev - cs) * mu
    r = xs[:, 0:C_WIDTH]
    k = xs[:, C_WIDTH:2 * C_WIDTH]
    v = xs[:, 2 * C_WIDTH:3 * C_WIDTH]
    la = xs[:, 3 * C_WIDTH:C_SHIFT_PAD]
    yield
    lw = (-LOG2E * math.exp(-0.5)) * _sigmoid(w0 + _mm(jnp.tanh(la), w2p))
    a = _sigmoid(a0 + _mm(la, a2p))
    v_own = v
    if vfirst is not None:
        v = v + (vfirst - v) * _sigmoid(v0 + _mm(_mm(v, v1p), v2p))
    yield
    kk = k * k_k
    kap = kk * lax.rsqrt(_mm(kk * kk, ones_bd) + 1e-12)
    k2 = k * (1.0 + (a - 1.0) * k_a)
    b = kap * a

    yield
    cum = _mm_exact_lhs(tril_l, lw, 2)
    cum_end = cum[CHUNK - 1:CHUNK, :]
    e_in = jnp.exp2(cum)
    e_ex = jnp.exp2(cum - lw)
    e_neg = jnp.exp2(-cum)
    e_end = jnp.exp2(cum_end - cum)
    p_end = jnp.exp2(cum_end)

    yield
    kt = kap * e_ex
    rt = r * e_in
    kt_st = _stack_heads(kt)
    vs_st = _stack_heads(v)
    gram = _mm_nt(jnp.concatenate([kt, rt], axis=0),
                  jnp.concatenate([_stack_heads(b * e_neg), _stack_heads(k2 * e_neg)], axis=0))
    a_ab = jnp.where(sl, gram[0:CHUNK, 0:STACK], 0.0)
    a_ak = jnp.where(sl, gram[0:CHUNK, STACK:2 * STACK], 0.0)
    a_rb = jnp.where(le, gram[CHUNK:2 * CHUNK, 0:STACK], 0.0)
    a_rk = jnp.where(le, gram[CHUNK:2 * CHUNK, STACK:2 * STACK], 0.0)

    yield
    pw = -a_ab
    t_w = eye_w + pw
    av = _mm(jnp.concatenate([a_ak, a_rk], axis=0), vs_st)
    pw = _mm(pw, _stack_heads(pw))
    for _ in range(1, int(math.log2(CHUNK)) - 1):
        yield
        both = _mm(jnp.concatenate([t_w, pw], axis=0), _stack_heads(pw))
        t_w = t_w + both[0:CHUNK]
        pw = both[CHUNK:2 * CHUNK]
    yield
    t_w = t_w + _mm(t_w, _stack_heads(pw))
    yield
    tx = _mm(t_w, jnp.concatenate([kt_st, _stack_heads(av[0:CHUNK])], axis=1))
    yield
    m1 = tx[:, 0:STACK]
    c1 = tx[:, STACK:2 * STACK]
    ab = _mm(a_rb, jnp.concatenate([_stack_heads(m1), _stack_heads(c1)], axis=1))
    q_n = rt - ab[:, 0:STACK]
    d_n = av[CHUNK:2 * CHUNK] - ab[:, STACK:2 * STACK]
    bh = b * e_end
    g_w = eye_w * p_end - _unstack_heads(_mm_tn(bh, m1))
    c_w = _unstack_heads(_mm_tn(jnp.concatenate([k2 * e_end, bh], axis=0),
                                jnp.concatenate([v, -c1], axis=0)))
    bonus_arg = r * k2 * r_k
    return q_n, d_n, g_w, c_w, bonus_arg, v, v_own


RWKV_SAVED = 6


def _rwkv_kernel(use_vres, *refs):
    if use_vres:
        (pc_ref, vf_ref, mu_ref, vec_ref, w2_ref, a2_ref, v1_ref, v2_ref, tril_ref, bd_ref,
         sl_ref, le_ref, eye_ref, y_ref, h_ref, pre_ref) = refs
        vo_ref = None
    else:
        (pc_ref, mu_ref, vec_ref, w2_ref, a2_ref, v1_ref, v2_ref, tril_ref, bd_ref,
         sl_ref, le_ref, eye_ref, y_ref, vo_ref, h_ref, pre_ref) = refs
        vf_ref = None
    seq = pc_ref.shape[1]
    ngroups = seq // (CHUNK * RWKV_GROUP)
    h_ref[...] = jnp.zeros_like(h_ref)

    def chunk_start(gi, j):
        return pl.multiple_of((gi * RWKV_GROUP + j) * CHUNK, CHUNK)

    def pre_generators(gi, first_group):
        sl = sl_ref[...] > 0.0
        le = le_ref[...] > 0.0
        gens = []
        for j in range(RWKV_GROUP):
            t0 = chunk_start(gi, j)
            cs = pc_ref[0, pl.ds(t0, CHUNK), 0:C_SHIFT_PAD]
            if first_group and j == 0:
                row = lax.broadcasted_iota(jnp.int32, cs.shape, 0)
                prev = jnp.where(row == 0, 0.0, pltpu.roll(cs, 1, axis=0))
            else:
                tp = pl.multiple_of(t0 - SUBLANES, SUBLANES)
                ext = pc_ref[0, pl.ds(tp, CHUNK + SUBLANES), 0:C_SHIFT_PAD]
                prev = pltpu.roll(ext, 1, axis=0)[SUBLANES:SUBLANES + CHUNK]
            vfirst = vf_ref[0, pl.ds(t0, CHUNK), :] if use_vres else None
            gens.append(_rwkv_pre(cs, prev, vfirst, mu_ref[...], vec_ref[...], w2_ref[...],
                                  a2_ref[...], v1_ref[...], v2_ref[...], bd_ref[...], tril_ref[...],
                                  sl, le, eye_ref[...]))
        return gens

    def chain_tasks(gi):
        ys = [None] * RWKV_GROUP
        stats = [None] * RWKV_GROUP
        state = {}

        def step(j):
            def run():
                h_w = h_ref[...] if j == 0 else state["h"]
                q_n, d_n, g_w, c_w = (pre_ref[j * RWKV_SAVED + k] for k in range(4))
                res = _mm(jnp.concatenate([q_n, g_w], axis=0), _stack_heads(h_w))
                ys[j] = res[0:CHUNK] + d_n
                state["h"] = res[CHUNK:2 * CHUNK] + c_w
                if j == RWKV_GROUP - 1:
                    h_ref[...] = state["h"]
            return run

        def sums():
            for j in range(RWKV_GROUP):
                parts = _split_bf16(ys[j], 2) + [pre_ref[j * RWKV_SAVED + 4].astype(BF16)]
                s = jnp.dot(jnp.concatenate(parts, axis=0), bd_ref[...], preferred_element_type=F32)
                yc = ys[j] - (s[0:CHUNK] + s[CHUNK:2 * CHUNK]) * (1.0 / HEAD_DIM)
                stats[j] = (yc, s[2 * CHUNK:3 * CHUNK])

        def variance():
            for j in range(RWKV_GROUP):
                yc, bonus = stats[j]
                yv = _mm(yc * yc, bd_ref[...]) * (1.0 / HEAD_DIM)
                stats[j] = (yc, bonus, yv)

        def finish():
            gn_w = vec_ref[5:6, :]
            gn_b = vec_ref[6:7, :]
            for j in range(RWKV_GROUP):
                t0 = chunk_start(gi, j)
                yc, bonus, yv = stats[j]
                yn = yc * lax.rsqrt(yv + GN_EPS) * gn_w + gn_b
                g = pc_ref[0, pl.ds(t0, CHUNK), C_SHIFT_WIDTH:GC_WIDTH]
                out = (yn + bonus * pre_ref[j * RWKV_SAVED + 5]) * _silu(g)
                y_ref[0, pl.ds(t0, CHUNK), :] = out.astype(BF16)

        return [step(j) for j in range(RWKV_GROUP)] + [sums, variance, finish]

    def run_group(gi, first_group, tasks):
        gens = pre_generators(gi, first_group)
        tasks = list(tasks)
        pre = [None] * RWKV_GROUP
        sweep = 0
        while any(p is None for p in pre):
            for j, gen in enumerate(gens):
                if pre[j] is None:
                    try:
                        next(gen)
                    except StopIteration as stop:
                        pre[j] = stop.value
            sweep += 1
            if tasks and sweep >= 2:
                tasks.pop(0)()
        for t in tasks:
            t()
        for j in range(RWKV_GROUP):
            for k in range(RWKV_SAVED):
                pre_ref[j * RWKV_SAVED + k] = pre[j][k]
            if not use_vres:
                vo_ref[0, pl.ds(chunk_start(gi, j), CHUNK), :] = pre[j][RWKV_SAVED]

    run_group(0, True, [])

    def body(i, carry):
        run_group(i, False, chain_tasks(i - 1))
        return carry

    lax.fori_loop(1, ngroups, body, 0)
    for t in chain_tasks(ngroups - 1):
        t()


def _rwkv_consts():
    idx = jnp.arange(STACK)
    same = (idx[:, None] // CHUNK) == (idx[None, :] // CHUNK)
    ci = jnp.arange(CHUNK)
    tril_l = (ci[None, :] <= ci[:, None]).astype(BF16)
    s_loc = idx[None, :] % CHUNK
    sl = (s_loc < ci[:, None]).astype(F32)
    le = (s_loc <= ci[:, None]).astype(F32)
    eye_w = (s_loc == ci[:, None]).astype(F32)
    return tril_l, same.astype(BF16), sl, le, eye_w


def _rwkv(pc, vfirst, mu_all, vec_all, w2_all, a2_all, v1_all, v2_all, consts, layer):
    bsz, seq, _ = pc.shape
    use_vres = vfirst is not None
    sq = (STACK, STACK)
    wide = (CHUNK, STACK)
    seq_spec = lambda w: pl.BlockSpec((1, seq, w), lambda b: (b, 0, 0))
    in_specs = [seq_spec(GC_WIDTH)] + ([seq_spec(C_WIDTH)] if use_vres else []) + [
        _layer_spec((1, C_SHIFT_PAD), layer), _layer_spec((8, C_WIDTH), layer),
        _layer_spec((LANES, C_WIDTH), layer), _layer_spec((LANES, C_WIDTH), layer),
        _layer_spec((C_WIDTH, LANES), layer), _layer_spec((LANES, C_WIDTH), layer),
        _const_spec((CHUNK, CHUNK)), _const_spec(sq),
        _const_spec(wide), _const_spec(wide), _const_spec(wide)]
    out_specs = [seq_spec(C_WIDTH)]
    out_shape = [jax.ShapeDtypeStruct((bsz, seq, C_WIDTH), BF16)]
    if not use_vres:
        out_specs.append(seq_spec(C_WIDTH))
        out_shape.append(jax.ShapeDtypeStruct((bsz, seq, C_WIDTH), F32))
    args = (pc,) + ((vfirst,) if use_vres else ()) + (mu_all, vec_all, w2_all, a2_all, v1_all, v2_all)
    outs = pl.pallas_call(
        functools.partial(_rwkv_kernel, use_vres),
        grid=(bsz,),
        in_specs=in_specs,
        out_specs=out_specs,
        out_shape=out_shape,
        scratch_shapes=[pltpu.VMEM(wide, F32), pltpu.VMEM((RWKV_GROUP * RWKV_SAVED,) + wide, F32)],
        compiler_params=pltpu.CompilerParams(dimension_semantics=("arbitrary",),
                                             vmem_limit_bytes=VMEM_LIMIT),
        name="rwkv",
    )(*args, *consts)
    return (outs[0], None) if use_vres else (outs[0], outs[1])


def _out_kernel(alpha, tm, fused, ya_ref, yb_ref, yc_ref, x_ref, w_ref, ln_ref, *rest):
    if fused:
        win_ref, o_ref, oa_ref, ob_ref, oc_ref = rest
    else:
        (o_ref,) = rest

    def project(rows):
        acc = jnp.dot(ya_ref[rows, :], w_ref[0:A_WIDTH, :], preferred_element_type=F32)
        acc = acc + jnp.dot(yb_ref[rows, :], w_ref[A_WIDTH:A_WIDTH + B_WIDTH, :],
                            preferred_element_type=F32)
        return acc + jnp.dot(yc_ref[rows, :], w_ref[A_WIDTH + B_WIDTH:D_MIX, :],
                             preferred_element_type=F32)

    def norm(rows, acc):
        z = alpha * x_ref[rows, :] + acc
        mu = jnp.mean(z, axis=-1, keepdims=True)
        zc = z - mu
        var = jnp.mean(zc * zc, axis=-1, keepdims=True)
        xn = zc * lax.rsqrt(var + LN_EPS) * ln_ref[0:1, :] + ln_ref[1:2, :]
        o_ref[rows, :] = xn
        if fused:
            _proj_rows(xn.astype(BF16), win_ref, rows, oa_ref, ob_ref, oc_ref)

    subs = [slice(s * OUT_SUB, (s + 1) * OUT_SUB) for s in range(tm // OUT_SUB)]
    acc_prev = project(subs[0])
    for s in range(1, len(subs)):
        acc_next = project(subs[s])
        norm(subs[s - 1], acc_prev)
        acc_prev = acc_next
    norm(subs[-1], acc_prev)


def _out(ya, yb, yc, x2, w_all, ln_all, alpha, layer, w_in_all=None):
    n = x2.shape[0]
    fused = w_in_all is not None
    tm = FUSED_TM if fused else OUT_TM
    row = lambda w: pl.BlockSpec((tm, w), lambda i: (i, 0))
    in_specs = [row(A_WIDTH), row(B_WIDTH), row(C_WIDTH), row(D_MODEL),
                _layer_spec((D_MIX, D_MODEL), layer), _layer_spec((2, D_MODEL), layer)]
    out_specs = [row(D_MODEL)]
    out_shape = [jax.ShapeDtypeStruct((n, D_MODEL), F32)]
    args = [ya, yb, yc, x2, w_all, ln_all]
    if fused:
        in_specs.append(_layer_spec((D_IN_PAD, D_MODEL), layer + 1))
        out_specs += [row(GA_WIDTH), row(GB_WIDTH), row(GC_WIDTH)]
        out_shape += [jax.ShapeDtypeStruct((n, GA_WIDTH), BF16),
                      jax.ShapeDtypeStruct((n, GB_WIDTH), F32),
                      jax.ShapeDtypeStruct((n, GC_WIDTH), F32)]
        args.append(w_in_all)
    return pl.pallas_call(
        functools.partial(_out_kernel, alpha, tm, fused),
        grid=(n // tm,),
        in_specs=in_specs,
        out_specs=out_specs,
        out_shape=out_shape,
        compiler_params=pltpu.CompilerParams(dimension_semantics=("arbitrary",),
                                             vmem_limit_bytes=VMEM_LIMIT),
        name="outproj_proj" if fused else "outproj",
    )(*args)


def _block_diag_all(w):
    depth, nblk, d, _ = w.shape
    eye = jnp.eye(nblk, dtype=w.dtype)
    return (w[:, :, :, None, :] * eye[None, :, None, :, None]).reshape(depth, nblk * d, nblk * d)


def kernel(x, w_in, w_out, ln_g, ln_b, attn_sinks, conv_w, conv_b, lru_wa, lru_ba, lru_wx, lru_bx,
           lru_lambda, rwkv_mu, rwkv_w0, rwkv_w2, rwkv_a0, rwkv_a2, rwkv_kk, rwkv_ka, rwkv_rk,
           rwkv_gn_w, rwkv_gn_b, rwkv_v0, rwkv_v1, rwkv_v2):
    bsz, seq, dm = x.shape
    depth = w_in.shape[0]
    alpha = (2 * depth) ** 0.25
    n = bsz * seq
    pad_w = C_SHIFT_PAD - C_SHIFT_WIDTH

    q_scale = jnp.concatenate([jnp.full((A_WIDTH,), HEAD_DIM ** -0.5 * LOG2E, F32),
                               jnp.ones((w_in.shape[2] - A_WIDTH,), F32)])
    w_in_p = (jnp.swapaxes(w_in, 1, 2) * q_scale[None, :, None]).astype(BF16)
    w_out_b = w_out.astype(BF16)
    ln_all = jnp.stack([ln_g, ln_b], axis=1)
    lru_vec = jnp.stack([conv_b, lru_ba, lru_bx, lru_lambda], axis=1)
    wa_bd = _block_diag_all(lru_wa).astype(BF16)
    wx_bd = _block_diag_all(lru_wx).astype(BF16)
    mu_all = jnp.pad(rwkv_mu, ((0, 0), (0, pad_w)))[:, None, :]
    v0_all = jnp.pad(rwkv_v0, ((1, 0), (0, 0)))
    vec_all = jnp.stack([rwkv_w0, rwkv_a0, rwkv_kk, rwkv_ka, rwkv_rk.reshape(depth, C_WIDTH),
                         rwkv_gn_w, rwkv_gn_b, v0_all], axis=1)
    w2_all = jnp.pad(rwkv_w2, ((0, 0), (0, LANES - DECAY_RANK), (0, 0))).astype(BF16)
    a2_all = jnp.pad(rwkv_a2, ((0, 0), (DECAY_RANK, LANES - DECAY_RANK - AICL_RANK), (0, 0))).astype(BF16)
    v1_all = jnp.pad(rwkv_v1, ((1, 0), (0, 0), (0, LANES - VRES_RANK))).astype(BF16)
    v2_all = jnp.pad(rwkv_v2, ((1, 0), (0, LANES - VRES_RANK), (0, 0))).astype(BF16)
    consts = _rwkv_consts()

    x2 = x.reshape(n, dm)
    v_first = None
    pa, pb, pc = _proj(x2, w_in_p, 0)
    for l in range(depth):
        ya = _attn(pa.reshape(bsz, seq, GA_WIDTH), attn_sinks, l)
        yb = _lru(pb.reshape(bsz, seq, GB_WIDTH), conv_w, lru_vec, wa_bd, wx_bd, l)
        yc, v_own = _rwkv(pc.reshape(bsz, seq, GC_WIDTH), v_first, mu_all, vec_all, w2_all, a2_all,
                          v1_all, v2_all, consts, l)
        if l == 0:
            v_first = v_own
        outs = _out(ya.reshape(n, A_WIDTH), yb.reshape(n, B_WIDTH), yc.reshape(n, C_WIDTH), x2,
                    w_out_b, ln_all, alpha, l, w_in_p if l + 1 < depth else None)
        x2 = outs[0]
        if l + 1 < depth:
            pa, pb, pc = outs[1:]
    return x2.reshape(bsz, seq, dm)
```

```python
import functools
import math

import jax
import jax.numpy as jnp
from jax import lax
from jax.experimental import pallas as pl
from jax.experimental.pallas import tpu as pltpu

F32 = jnp.float32
BF16 = jnp.bfloat16

D_MODEL = 1024
HEAD_DIM = 64
A_Q_HEADS = 8
A_KV_HEADS = 2
A_WIDTH = A_Q_HEADS * HEAD_DIM
A_KV_WIDTH = A_KV_HEADS * HEAD_DIM
ATT_BLOCK = 128
B_WIDTH = 256
B_BLOCKS = 4
CONV_WIDTH = 4
LRU_C = 8.0
C_HEADS = 4
C_WIDTH = C_HEADS * HEAD_DIM
DECAY_RANK = 32
AICL_RANK = 32
VRES_RANK = 16
C_SHIFT_WIDTH = 3 * C_WIDTH + DECAY_RANK + AICL_RANK
GN_EPS = 64e-5
LN_EPS = 1e-5
LOG2E = math.log2(math.e)

LANES = 128
SUBLANES = 8
C_SHIFT_PAD = ((C_SHIFT_WIDTH + LANES - 1) // LANES) * LANES
GA_WIDTH = A_WIDTH + 2 * A_KV_WIDTH + A_WIDTH
GB_WIDTH = 2 * B_WIDTH
GC_WIDTH = C_SHIFT_WIDTH + C_WIDTH
D_IN_PAD = GA_WIDTH + GB_WIDTH + GC_WIDTH
D_MIX = A_WIDTH + B_WIDTH + C_WIDTH

CHUNK = 64
STACK = C_HEADS * CHUNK
RWKV_GROUP = 8
ATT_TILE = 1024
LRU_TILE = 256
LRU_STAGE_ROWS = SUBLANES * (LRU_TILE // SUBLANES + 4)
PROJ_TM = 512
OUT_TM = 1024
FUSED_TM = 1024
OUT_SUB = 256
VMEM_LIMIT = 56 * 1024 * 1024


def _mm(a, b):
    return jnp.dot(a.astype(BF16), b.astype(BF16), preferred_element_type=F32)


def _mm_nt(a, b):
    return lax.dot_general(a.astype(BF16), b.astype(BF16), (((1,), (1,)), ((), ())),
                           preferred_element_type=F32)


def _mm_tn(a, b):
    return lax.dot_general(a.astype(BF16), b.astype(BF16), (((0,), (0,)), ((), ())),
                           preferred_element_type=F32)


def _split_bf16(x, parts):
    out = []
    rem = x
    for _ in range(parts):
        hi = rem.astype(BF16)
        out.append(hi)
        rem = rem - hi.astype(F32)
    return out


def _mm_exact_lhs(m_bf16, x, parts):
    n = x.shape[1]
    t = jnp.dot(m_bf16, jnp.concatenate(_split_bf16(x, parts), axis=1), preferred_element_type=F32)
    acc = t[:, 0:n]
    for p in range(1, parts):
        acc = acc + t[:, p * n:(p + 1) * n]
    return acc


def _sigmoid(x):
    return 1.0 / (1.0 + jnp.exp2(x * (-LOG2E)))


def _silu(x):
    return x * _sigmoid(x)


def _softplus(x):
    return jnp.maximum(x, 0.0) + jnp.log(1.0 + jnp.exp(-jnp.abs(x)))


def _layer_spec(shape, layer, single_buffer=False):
    index_map = lambda *_: (layer,) + (0,) * len(shape)
    if single_buffer:
        return pl.BlockSpec((None,) + shape, index_map, pipeline_mode=pl.Buffered(1))
    return pl.BlockSpec((None,) + shape, index_map)


def _const_spec(shape):
    return pl.BlockSpec(shape, lambda *_: (0,) * len(shape))


def _proj_rows(x16, w_ref, rows, oa_ref, ob_ref, oc_ref):
    nt = (((1,), (1,)), ((), ()))
    oa_ref[rows, :] = lax.dot_general(x16, w_ref[0:GA_WIDTH, :], nt,
                                      preferred_element_type=F32).astype(BF16)
    ob_ref[rows, :] = lax.dot_general(x16, w_ref[GA_WIDTH:GA_WIDTH + GB_WIDTH, :], nt,
                                      preferred_element_type=F32)
    oc_ref[rows, :] = lax.dot_general(x16, w_ref[GA_WIDTH + GB_WIDTH:D_IN_PAD, :], nt,
                                      preferred_element_type=F32)


def _proj_kernel(x_ref, w_ref, oa_ref, ob_ref, oc_ref):
    _proj_rows(x_ref[...].astype(BF16), w_ref, slice(None), oa_ref, ob_ref, oc_ref)


def _proj(x2, w_all, layer):
    n = x2.shape[0]
    return pl.pallas_call(
        _proj_kernel,
        grid=(n // PROJ_TM,),
        in_specs=[pl.BlockSpec((PROJ_TM, D_MODEL), lambda i: (i, 0)),
                  _layer_spec((D_IN_PAD, D_MODEL), layer)],
        out_specs=[pl.BlockSpec((PROJ_TM, GA_WIDTH), lambda i: (i, 0)),
                   pl.BlockSpec((PROJ_TM, GB_WIDTH), lambda i: (i, 0)),
                   pl.BlockSpec((PROJ_TM, GC_WIDTH), lambda i: (i, 0))],
        out_shape=[jax.ShapeDtypeStruct((n, GA_WIDTH), BF16),
                   jax.ShapeDtypeStruct((n, GB_WIDTH), F32),
                   jax.ShapeDtypeStruct((n, GC_WIDTH), F32)],
        compiler_params=pltpu.CompilerParams(dimension_semantics=("arbitrary",),
                                             vmem_limit_bytes=VMEM_LIMIT),
        name="proj",
    )(x2, w_all)


def _attn_kernel(layer, sink_ref, cur_ref, prev_ref, o_ref):
    n = pl.program_id(1)
    blk = ATT_BLOCK
    nsub = ATT_TILE // blk
    grp = A_Q_HEADS // A_KV_HEADS
    lo = lax.broadcasted_iota(jnp.int32, (blk, A_KV_WIDTH), 1) < HEAD_DIM

    def variants(x16):
        x = x16.astype(F32)
        xs = pltpu.roll(x, HEAD_DIM, axis=1)
        return ((jnp.where(lo, x, 0.0).astype(BF16), jnp.where(lo, 0.0, xs).astype(BF16)),
                (jnp.where(lo, xs, 0.0).astype(BF16), jnp.where(lo, 0.0, x).astype(BF16)))

    k0, v0 = A_WIDTH, A_WIDTH + A_KV_WIDTH
    g0 = A_WIDTH + 2 * A_KV_WIDTH
    kblk = [variants(prev_ref[0, :, 0:A_KV_WIDTH])]
    vblk = [variants(prev_ref[0, :, A_KV_WIDTH:2 * A_KV_WIDTH])]
    for j in range(nsub):
        kblk.append(variants(cur_ref[0, j * blk:(j + 1) * blk, k0:k0 + A_KV_WIDTH]))
        vblk.append(variants(cur_ref[0, j * blk:(j + 1) * blk, v0:v0 + A_KV_WIDTH]))

    qi = lax.broadcasted_iota(jnp.int32, (blk, 2 * blk), 0)
    kj = lax.broadcasted_iota(jnp.int32, (blk, 2 * blk), 1)
    diff = qi + blk - kj
    band = (diff >= 0) & (diff < blk)
    first = band & (kj + (n * nsub - 1) * blk >= 0)

    for j in range(nsub):
        mask = first if j == 0 else band
        rows = slice(j * blk, (j + 1) * blk)
        scores = []
        for p in range(A_Q_HEADS // 2):
            h = (2 * p) // grp
            qp = cur_ref[0, rows, p * LANES:(p + 1) * LANES]
            for half in range(2):
                kb = jnp.concatenate([kblk[j][h][half], kblk[j + 1][h][half]], axis=0)
                scores.append(lax.dot_general(qp, kb, (((1,), (1,)), ((), ())),
                                              preferred_element_type=F32))
        probs, inv = [], []
        for idx, s in enumerate(scores):
            sink2 = sink_ref[layer, idx] * LOG2E
            s = jnp.where(mask, s, -jnp.inf)
            m = jnp.maximum(jnp.max(s, axis=-1, keepdims=True), sink2)
            e = jnp.exp2(s - m)
            den = jnp.sum(e, axis=-1, keepdims=True) + jnp.exp2(sink2 - m)
            probs.append(e.astype(BF16))
            inv.append(1.0 / den)
        for p in range(A_Q_HEADS // 2):
            h = (2 * p) // grp
            acc = None
            for half in range(2):
                vb = jnp.concatenate([vblk[j][h][half], vblk[j + 1][h][half]], axis=0)
                o = jnp.dot(probs[2 * p + half], vb, preferred_element_type=F32) * inv[2 * p + half]
                acc = o if acc is None else acc + o
            g = cur_ref[0, rows, g0 + p * LANES:g0 + (p + 1) * LANES].astype(F32)
            o_ref[0, rows, p * LANES:(p + 1) * LANES] = (acc * _silu(g)).astype(BF16)


def _attn(pa, sinks, layer):
    bsz, seq, _ = pa.shape
    nsub = ATT_TILE // ATT_BLOCK
    kv_blk = A_WIDTH // (2 * A_KV_WIDTH)
    return pl.pallas_call(
        functools.partial(_attn_kernel, layer),
        grid=(bsz, seq // ATT_TILE),
        in_specs=[pl.BlockSpec(memory_space=pltpu.SMEM),
                  pl.BlockSpec((1, ATT_TILE, GA_WIDTH), lambda b, n: (b, n, 0)),
                  pl.BlockSpec((1, ATT_BLOCK, 2 * A_KV_WIDTH),
                               lambda b, n: (b, jnp.maximum(n * nsub - 1, 0), kv_blk))],
        out_specs=pl.BlockSpec((1, ATT_TILE, A_WIDTH), lambda b, n: (b, n, 0)),
        out_shape=jax.ShapeDtypeStruct((bsz, seq, A_WIDTH), BF16),
        compiler_params=pltpu.CompilerParams(dimension_semantics=("arbitrary", "arbitrary"),
                                             vmem_limit_bytes=VMEM_LIMIT),
        name="attn",
    )(sinks, pa, pa)


def _lru_kernel(pb_ref, cw_ref, vec_ref, wa_ref, wx_ref, o_ref, in_ref, stage_ref):
    seq = pb_ref.shape[1]
    tile = LRU_TILE
    nseg = SUBLANES
    nstep = tile // nseg
    conv_b = vec_ref[0:1, :]
    ba = vec_ref[1:2, :]
    bx = vec_ref[2:3, :]
    lam = vec_ref[3:4, :]
    neg_c_sp = -LRU_C * _softplus(-lam)
    cw = [cw_ref[j:j + 1, :] for j in range(CONV_WIDTH)]
    sub0 = lax.broadcasted_iota(jnp.int32, (nseg, B_WIDTH), 0) == 0

    nhalf = B_WIDTH // LANES

    pitch = nstep + 4
    assert pitch % SUBLANES == 4

    def rows_of(step):
        return pl.ds(step, nseg, stride=pitch)

    def permuted(k0, step):
        return jnp.concatenate([in_ref[k0 + k, rows_of(step), :] for k in range(nhalf)], axis=1)

    hc = jnp.zeros((1, B_WIDTH), F32)
    tail = [jnp.zeros((1, B_WIDTH), F32)] * (CONV_WIDTH - 1)
    for ti in range(seq // tile):
        t0 = ti * tile
        for k in range(2 * nhalf):
            for sgi in range(nseg):
                in_ref[k, sgi * pitch:sgi * pitch + nstep, :] = pb_ref[
                    0, t0 + sgi * nstep:t0 + (sgi + 1) * nstep, k * LANES:(k + 1) * LANES]
        xs = [permuted(0, v) for v in range(nstep)]
        wrapped = []
        for i in range(CONV_WIDTH - 1):
            src_v = xs[nstep - (CONV_WIDTH - 1) + i]
            wrapped.append(jnp.where(sub0, tail[i], pltpu.roll(src_v, 1, axis=0)))
        x_all = jnp.concatenate(xs, axis=0)
        xc = cw[CONV_WIDTH - 1] * x_all + conv_b
        for j in range(1, CONV_WIDTH):
            shifted = jnp.concatenate(wrapped[CONV_WIDTH - 1 - j:] + xs[:nstep - j], axis=0)
            xc = xc + cw[CONV_WIDTH - 1 - j] * shifted
        tail = [xs[nstep - (CONV_WIDTH - 1) + i][nseg - 1:nseg, :] for i in range(CONV_WIDTH - 1)]

        r = _sigmoid(_mm(xc, wa_ref[...]) + ba)
        ig = _sigmoid(_mm(xc, wx_ref[...]) + bx)
        a = jnp.exp(neg_c_sp * r)
        u = jnp.sqrt(1.0 - a * a) * (ig * xc)
        hl = [u[0:nseg]]
        al = [a[0:nseg]]
        for v in range(1, nstep):
            av = a[v * nseg:(v + 1) * nseg]
            hl.append(av * hl[-1] + u[v * nseg:(v + 1) * nseg])
            al.append(av * al[-1])
        carry = hc
        enter = []
        for sgi in range(nseg):
            enter.append(carry)
            carry = hl[-1][sgi:sgi + 1, :] + al[-1][sgi:sgi + 1, :] * carry
        hc = carry
        enter = jnp.concatenate(enter, axis=0)
        for v in range(nstep):
            out = (hl[v] + al[v] * enter) * _silu(permuted(nhalf, v))
            for k in range(nhalf):
                stage_ref[k, rows_of(v), :] = out[:, k * LANES:(k + 1) * LANES]
        for k in range(nhalf):
            for sgi in range(nseg):
                o_ref[0, t0 + sgi * nstep:t0 + (sgi + 1) * nstep, k * LANES:(k + 1) * LANES] = (
                    stage_ref[k, sgi * pitch:sgi * pitch + nstep, :].astype(BF16))


def _lru(pb, cw_all, vec_all, wa_all, wx_all, layer):
    bsz, seq, _ = pb.shape
    return pl.pallas_call(
        _lru_kernel,
        grid=(bsz,),
        in_specs=[pl.BlockSpec((1, seq, GB_WIDTH), lambda b: (b, 0, 0)),
                  _layer_spec((CONV_WIDTH, B_WIDTH), layer), _layer_spec((4, B_WIDTH), layer),
                  _layer_spec((B_WIDTH, B_WIDTH), layer), _layer_spec((B_WIDTH, B_WIDTH), layer)],
        out_specs=pl.BlockSpec((1, seq, B_WIDTH), lambda b: (b, 0, 0)),
        out_shape=jax.ShapeDtypeStruct((bsz, seq, B_WIDTH), BF16),
        scratch_shapes=[pltpu.VMEM((GB_WIDTH // LANES, LRU_STAGE_ROWS, LANES), F32),
                        pltpu.VMEM((B_WIDTH // LANES, LRU_STAGE_ROWS, LANES), F32)],
        compiler_params=pltpu.CompilerParams(dimension_semantics=("arbitrary",),
                                             vmem_limit_bytes=VMEM_LIMIT),
        name="lru",
    )(pb, cw_all, vec_all, wa_all, wx_all)


def _stack_heads(x):
    x16 = x.astype(BF16)
    head = lax.broadcasted_iota(jnp.int32, x16.shape, 1) // HEAD_DIM
    return jnp.concatenate([jnp.where(head == h, x16, jnp.zeros_like(x16)) for h in range(C_HEADS)],
                           axis=0)


def _unstack_heads(x):
    head = lax.broadcasted_iota(jnp.int32, (CHUNK, STACK), 1) // HEAD_DIM
    out = x[(C_HEADS - 1) * CHUNK:C_HEADS * CHUNK]
    for h in range(C_HEADS - 2, -1, -1):
        out = jnp.where(head == h, x[h * CHUNK:(h + 1) * CHUNK], out)
    return out


def _rwkv_pre(cs, prev, vfirst, mu, vec, w2p, a2p, v1p, v2p, ones_bd, tril_l, sl, le, eye_w):
    w0, a0, k_k, k_a, r_k, _, _, v0 = (vec[i:i + 1, :] for i in range(8))
    xs = cs + (prev - cs) * mu
    r = xs[:, 0:C_WIDTH]
    k = xs[:, C_WIDTH:2 * C_WIDTH]
    v = xs[:, 2 * C_WIDTH:3 * C_WIDTH]
    la = xs[:, 3 * C_WIDTH:C_SHIFT_PAD]
    yield
    lw = (-LOG2E * math.exp(-0.5)) * _sigmoid(w0 + _mm(jnp.tanh(la), w2p))
    a = _sigmoid(a0 + _mm(la, a2p))
    v_own = v
    if vfirst is not None:
        v = v + (vfirst - v) * _sigmoid(v0 + _mm(_mm(v, v1p), v2p))
    yield
    kk = k * k_k
    kap = kk * lax.rsqrt(_mm(kk * kk, ones_bd) + 1e-12)
    k2 = k * (1.0 + (a - 1.0) * k_a)
    b = kap * a

    yield
    cum = _mm_exact_lhs(tril_l, lw, 2)
    cum_end = cum[CHUNK - 1:CHUNK, :]
    e_in = jnp.exp2(cum)
    e_ex = jnp.exp2(cum - lw)
    e_neg = jnp.exp2(-cum)
    e_end = jnp.exp2(cum_end - cum)
    p_end = jnp.exp2(cum_end)

    yield
    kt = kap * e_ex
    rt = r * e_in
    kt_st = _stack_heads(kt)
    vs_st = _stack_heads(v)
    gram = _mm_nt(jnp.concatenate([kt, rt], axis=0),
                  jnp.concatenate([_stack_heads(b * e_neg), _stack_heads(k2 * e_neg)], axis=0))
    a_ab = jnp.where(sl, gram[0:CHUNK, 0:STACK], 0.0)
    a_ak = jnp.where(sl, gram[0:CHUNK, STACK:2 * STACK], 0.0)
    a_rb = jnp.where(le, gram[CHUNK:2 * CHUNK, 0:STACK], 0.0)
    a_rk = jnp.where(le, gram[CHUNK:2 * CHUNK, STACK:2 * STACK], 0.0)

    yield
    pw = -a_ab
    t_w = eye_w + pw
    av = _mm(jnp.concatenate([a_ak, a_rk], axis=0), vs_st)
    pw = _mm(pw, _stack_heads(pw))
    for _ in range(1, int(math.log2(CHUNK)) - 1):
        yield
        both = _mm(jnp.concatenate([t_w, pw], axis=0), _stack_heads(pw))
        t_w = t_w + both[0:CHUNK]
        pw = both[CHUNK:2 * CHUNK]
    yield
    t_w = t_w + _mm(t_w, _stack_heads(pw))
    yield
    tx = _mm(t_w, jnp.concatenate([kt_st, _stack_heads(av[0:CHUNK])], axis=1))
    yield
    m1 = tx[:, 0:STACK]
    c1 = tx[:, STACK:2 * STACK]
    ab = _mm(a_rb, jnp.concatenate([_stack_heads(m1), _stack_heads(c1)], axis=1))
    q_n = rt - ab[:, 0:STACK]
    d_n = av[CHUNK:2 * CHUNK] - ab[:, STACK:2 * STACK]
    bh = b * e_end
    g_w = eye_w * p_end - _unstack_heads(_mm_tn(bh, m1))
    c_w = _unstack_heads(_mm_tn(jnp.concatenate([k2 * e_end, bh], axis=0),
                                jnp.concatenate([v, -c1], axis=0)))
    bonus_arg = r * k2 * r_k
    return q_n, d_n, g_w, c_w, bonus_arg, v, v_own


RWKV_SAVED = 6


def _rwkv_kernel(use_vres, *refs):
    if use_vres:
        (pc_ref, vf_ref, mu_ref, vec_ref, w2_ref, a2_ref, v1_ref, v2_ref, tril_ref, bd_ref,
         sl_ref, le_ref, eye_ref, y_ref, h_ref, pre_ref) = refs
        vo_ref = None
    else:
        (pc_ref, mu_ref, vec_ref, w2_ref, a2_ref, v1_ref, v2_ref, tril_ref, bd_ref,
         sl_ref, le_ref, eye_ref, y_ref, vo_ref, h_ref, pre_ref) = refs
        vf_ref = None
    seq = pc_ref.shape[1]
    ngroups = seq // (CHUNK * RWKV_GROUP)
    h_ref[...] = jnp.zeros_like(h_ref)

    def chunk_start(gi, j):
        return pl.multiple_of((gi * RWKV_GROUP + j) * CHUNK, CHUNK)

    def pre_generators(gi, first_group):
        sl = sl_ref[...] > 0.0
        le = le_ref[...] > 0.0
        gens = []
        for j in range(RWKV_GROUP):
            t0 = chunk_start(gi, j)
            cs = pc_ref[0, pl.ds(t0, CHUNK), 0:C_SHIFT_PAD]
            if first_group and j == 0:
                row = lax.broadcasted_iota(jnp.int32, cs.shape, 0)
                prev = jnp.where(row == 0, 0.0, pltpu.roll(cs, 1, axis=0))
            else:
                tp = pl.multiple_of(t0 - SUBLANES, SUBLANES)
                ext = pc_ref[0, pl.ds(tp, CHUNK + SUBLANES), 0:C_SHIFT_PAD]
                prev = pltpu.roll(ext, 1, axis=0)[SUBLANES:SUBLANES + CHUNK]
            vfirst = vf_ref[0, pl.ds(t0, CHUNK), :] if use_vres else None
            gens.append(_rwkv_pre(cs, prev, vfirst, mu_ref[...], vec_ref[...], w2_ref[...],
                                  a2_ref[...], v1_ref[...], v2_ref[...], bd_ref[...], tril_ref[...],
                                  sl, le, eye_ref[...]))
        return gens

    def chain_tasks(gi):
        ys = [None] * RWKV_GROUP
        stats = [None] * RWKV_GROUP
        state = {}

        def step(j):
            def run():
                h_w = h_ref[...] if j == 0 else state["h"]
                q_n, d_n, g_w, c_w = (pre_ref[j * RWKV_SAVED + k] for k in range(4))
                res = _mm(jnp.concatenate([q_n, g_w], axis=0), _stack_heads(h_w))
                ys[j] = res[0:CHUNK] + d_n
                state["h"] = res[CHUNK:2 * CHUNK] + c_w
                if j == RWKV_GROUP - 1:
                    h_ref[...] = state["h"]
            return run

        def sums():
            for j in range(RWKV_GROUP):
                parts = _split_bf16(ys[j], 2) + [pre_ref[j * RWKV_SAVED + 4].astype(BF16)]
                s = jnp.dot(jnp.concatenate(parts, axis=0), bd_ref[...], preferred_element_type=F32)
                yc = ys[j] - (s[0:CHUNK] + s[CHUNK:2 * CHUNK]) * (1.0 / HEAD_DIM)
                stats[j] = (yc, s[2 * CHUNK:3 * CHUNK])

        def variance():
            for j in range(RWKV_GROUP):
                yc, bonus = stats[j]
                yv = _mm(yc * yc, bd_ref[...]) * (1.0 / HEAD_DIM)
                stats[j] = (yc, bonus, yv)

        def finish():
            gn_w = vec_ref[5:6, :]
            gn_b = vec_ref[6:7, :]
            for j in range(RWKV_GROUP):
                t0 = chunk_start(gi, j)
                yc, bonus, yv = stats[j]
                yn = yc * lax.rsqrt(yv + GN_EPS) * gn_w + gn_b
                g = pc_ref[0, pl.ds(t0, CHUNK), C_SHIFT_WIDTH:GC_WIDTH]
                out = (yn + bonus * pre_ref[j * RWKV_SAVED + 5]) * _silu(g)
                y_ref[0, pl.ds(t0, CHUNK), :] = out.astype(BF16)

        return [step(j) for j in range(RWKV_GROUP)] + [sums, variance, finish]

    def run_group(gi, first_group, tasks):
        gens = pre_generators(gi, first_group)
        tasks = list(tasks)
        pre = [None] * RWKV_GROUP
        sweep = 0
        while any(p is None for p in pre):
            for j, gen in enumerate(gens):
                if pre[j] is None:
                    try:
                        next(gen)
                    except StopIteration as stop:
                        pre[j] = stop.value
            sweep += 1
            if tasks and sweep >= 2:
                tasks.pop(0)()
        for t in tasks:
            t()
        for j in range(RWKV_GROUP):
            for k in range(RWKV_SAVED):
                pre_ref[j * RWKV_SAVED + k] = pre[j][k]
            if not use_vres:
                vo_ref[0, pl.ds(chunk_start(gi, j), CHUNK), :] = pre[j][RWKV_SAVED]

    run_group(0, True, [])

    def body(i, carry):
        run_group(i, False, chain_tasks(i - 1))
        return carry

    lax.fori_loop(1, ngroups, body, 0)
    for t in chain_tasks(ngroups - 1):
        t()


def _rwkv_consts():
    idx = jnp.arange(STACK)
    same = (idx[:, None] // CHUNK) == (idx[None, :] // CHUNK)
    ci = jnp.arange(CHUNK)
    tril_l = (ci[None, :] <= ci[:, None]).astype(BF16)
    s_loc = idx[None, :] % CHUNK
    sl = (s_loc < ci[:, None]).astype(F32)
    le = (s_loc <= ci[:, None]).astype(F32)
    eye_w = (s_loc == ci[:, None]).astype(F32)
    return tril_l, same.astype(BF16), sl, le, eye_w


def _rwkv(pc, vfirst, mu_all, vec_all, w2_all, a2_all, v1_all, v2_all, consts, layer):
    bsz, seq, _ = pc.shape
    use_vres = vfirst is not None
    sq = (STACK, STACK)
    wide = (CHUNK, STACK)
    seq_spec = lambda w: pl.BlockSpec((1, seq, w), lambda b: (b, 0, 0))
    in_specs = [seq_spec(GC_WIDTH)] + ([seq_spec(C_WIDTH)] if use_vres else []) + [
        _layer_spec((1, C_SHIFT_PAD), layer), _layer_spec((8, C_WIDTH), layer),
        _layer_spec((LANES, C_WIDTH), layer), _layer_spec((LANES, C_WIDTH), layer),
        _layer_spec((C_WIDTH, LANES), layer), _layer_spec((LANES, C_WIDTH), layer),
        _const_spec((CHUNK, CHUNK)), _const_spec(sq),
        _const_spec(wide), _const_spec(wide), _const_spec(wide)]
    out_specs = [seq_spec(C_WIDTH)]
    out_shape = [jax.ShapeDtypeStruct((bsz, seq, C_WIDTH), BF16)]
    if not use_vres:
        out_specs.append(seq_spec(C_WIDTH))
        out_shape.append(jax.ShapeDtypeStruct((bsz, seq, C_WIDTH), F32))
    args = (pc,) + ((vfirst,) if use_vres else ()) + (mu_all, vec_all, w2_all, a2_all, v1_all, v2_all)
    outs = pl.pallas_call(
        functools.partial(_rwkv_kernel, use_vres),
        grid=(bsz,),
        in_specs=in_specs,
        out_specs=out_specs,
        out_shape=out_shape,
        scratch_shapes=[pltpu.VMEM(wide, F32), pltpu.VMEM((RWKV_GROUP * RWKV_SAVED,) + wide, F32)],
        compiler_params=pltpu.CompilerParams(dimension_semantics=("arbitrary",),
                                             vmem_limit_bytes=VMEM_LIMIT),
        name="rwkv",
    )(*args, *consts)
    return (outs[0], None) if use_vres else (outs[0], outs[1])


def _out_kernel(alpha, tm, fused, ya_ref, yb_ref, yc_ref, x_ref, w_ref, ln_ref, *rest):
    if fused:
        win_ref, o_ref, oa_ref, ob_ref, oc_ref = rest
    else:
        (o_ref,) = rest

    def project(rows):
        acc = jnp.dot(ya_ref[rows, :], w_ref[0:A_WIDTH, :], preferred_element_type=F32)
        acc = acc + jnp.dot(yb_ref[rows, :], w_ref[A_WIDTH:A_WIDTH + B_WIDTH, :],
                            preferred_element_type=F32)
        return acc + jnp.dot(yc_ref[rows, :], w_ref[A_WIDTH + B_WIDTH:D_MIX, :],
                             preferred_element_type=F32)

    def norm(rows, acc):
        z = alpha * x_ref[rows, :] + acc
        mu = jnp.mean(z, axis=-1, keepdims=True)
        zc = z - mu
        var = jnp.mean(zc * zc, axis=-1, keepdims=True)
        xn = zc * lax.rsqrt(var + LN_EPS) * ln_ref[0:1, :] + ln_ref[1:2, :]
        o_ref[rows, :] = xn
        if fused:
            _proj_rows(xn.astype(BF16), win_ref, rows, oa_ref, ob_ref, oc_ref)

    subs = [slice(s * OUT_SUB, (s + 1) * OUT_SUB) for s in range(tm // OUT_SUB)]
    acc_prev = project(subs[0])
    for s in range(1, len(subs)):
        acc_next = project(subs[s])
        norm(subs[s - 1], acc_prev)
        acc_prev = acc_next
    norm(subs[-1], acc_prev)


def _out(ya, yb, yc, x2, w_all, ln_all, alpha, layer, w_in_all=None):
    n = x2.shape[0]
    fused = w_in_all is not None
    tm = FUSED_TM if fused else OUT_TM
    row = lambda w: pl.BlockSpec((tm, w), lambda i: (i, 0))
    in_specs = [row(A_WIDTH), row(B_WIDTH), row(C_WIDTH), row(D_MODEL),
                _layer_spec((D_MIX, D_MODEL), layer, single_buffer=True),
                _layer_spec((2, D_MODEL), layer)]
    out_specs = [row(D_MODEL)]
    out_shape = [jax.ShapeDtypeStruct((n, D_MODEL), F32)]
    args = [ya, yb, yc, x2, w_all, ln_all]
    if fused:
        in_specs.append(_layer_spec((D_IN_PAD, D_MODEL), layer + 1, single_buffer=True))
        out_specs += [row(GA_WIDTH), row(GB_WIDTH), row(GC_WIDTH)]
        out_shape += [jax.ShapeDtypeStruct((n, GA_WIDTH), BF16),
                      jax.ShapeDtypeStruct((n, GB_WIDTH), F32),
                      jax.ShapeDtypeStruct((n, GC_WIDTH), F32)]
        args.append(w_in_all)
    return pl.pallas_call(
        functools.partial(_out_kernel, alpha, tm, fused),
        grid=(n // tm,),
        in_specs=in_specs,
        out_specs=out_specs,
        out_shape=out_shape,
        compiler_params=pltpu.CompilerParams(dimension_semantics=("arbitrary",),
                                             vmem_limit_bytes=VMEM_LIMIT),
        name="outproj_proj" if fused else "outproj",
    )(*args)


def _block_diag_all(w):
    depth, nblk, d, _ = w.shape
    eye = jnp.eye(nblk, dtype=w.dtype)
    return (w[:, :, :, None, :] * eye[None, :, None, :, None]).reshape(depth, nblk * d, nblk * d)


def kernel(x, w_in, w_out, ln_g, ln_b, attn_sinks, conv_w, conv_b, lru_wa, lru_ba, lru_wx, lru_bx,
           lru_lambda, rwkv_mu, rwkv_w0, rwkv_w2, rwkv_a0, rwkv_a2, rwkv_kk, rwkv_ka, rwkv_rk,
           rwkv_gn_w, rwkv_gn_b, rwkv_v0, rwkv_v1, rwkv_v2):
    bsz, seq, dm = x.shape
    depth = w_in.shape[0]
    alpha = (2 * depth) ** 0.25
    n = bsz * seq
    pad_w = C_SHIFT_PAD - C_SHIFT_WIDTH

    q_scale = jnp.concatenate([jnp.full((A_WIDTH,), HEAD_DIM ** -0.5 * LOG2E, F32),
                               jnp.ones((w_in.shape[2] - A_WIDTH,), F32)])
    w_in_p = (jnp.swapaxes(w_in, 1, 2) * q_scale[None, :, None]).astype(BF16)
    w_out_b = w_out.astype(BF16)
    ln_all = jnp.stack([ln_g, ln_b], axis=1)
    lru_vec = jnp.stack([conv_b, lru_ba, lru_bx, lru_lambda], axis=1)
    wa_bd = _block_diag_all(lru_wa).astype(BF16)
    wx_bd = _block_diag_all(lru_wx).astype(BF16)
    mu_all = jnp.pad(rwkv_mu, ((0, 0), (0, pad_w)))[:, None, :]
    v0_all = jnp.pad(rwkv_v0, ((1, 0), (0, 0)))
    vec_all = jnp.stack([rwkv_w0, rwkv_a0, rwkv_kk, rwkv_ka, rwkv_rk.reshape(depth, C_WIDTH),
                         rwkv_gn_w, rwkv_gn_b, v0_all], axis=1)
    w2_all = jnp.pad(rwkv_w2, ((0, 0), (0, LANES - DECAY_RANK), (0, 0))).astype(BF16)
    a2_all = jnp.pad(rwkv_a2, ((0, 0), (DECAY_RANK, LANES - DECAY_RANK - AICL_RANK), (0, 0))).astype(BF16)
    v1_all = jnp.pad(rwkv_v1, ((1, 0), (0, 0), (0, LANES - VRES_RANK))).astype(BF16)
    v2_all = jnp.pad(rwkv_v2, ((1, 0), (0, LANES - VRES_RANK), (0, 0))).astype(BF16)
    consts = _rwkv_consts()

    x2 = x.reshape(n, dm)
    v_first = None
    pa, pb, pc = _proj(x2, w_in_p, 0)
    for l in range(depth):
        ya = _attn(pa.reshape(bsz, seq, GA_WIDTH), attn_sinks, l)
        yb = _lru(pb.reshape(bsz, seq, GB_WIDTH), conv_w, lru_vec, wa_bd, wx_bd, l)
        yc, v_own = _rwkv(pc.reshape(bsz, seq, GC_WIDTH), v_first, mu_all, vec_all, w2_all, a2_all,
                          v1_all, v2_all, consts, l)
        if l == 0:
            v_first = v_own
        outs = _out(ya.reshape(n, A_WIDTH), yb.reshape(n, B_WIDTH), yc.reshape(n, C_WIDTH), x2,
                    w_out_b, ln_all, alpha, l, w_in_p if l + 1 < depth else None)
        x2 = outs[0]
        if l + 1 < depth:
            pa, pb, pc = outs[1:]
    return x2.reshape(bsz, seq, dm)
```

```python
import functools
import math

import jax
import jax.numpy as jnp
from jax import lax
from jax.experimental import pallas as pl
from jax.experimental.pallas import tpu as pltpu

F32 = jnp.float32
BF16 = jnp.bfloat16

D_MODEL = 1024
HEAD_DIM = 64
A_Q_HEADS = 8
A_KV_HEADS = 2
A_WIDTH = A_Q_HEADS * HEAD_DIM
A_KV_WIDTH = A_KV_HEADS * HEAD_DIM
ATT_BLOCK = 128
B_WIDTH = 256
B_BLOCKS = 4
CONV_WIDTH = 4
LRU_C = 8.0
C_HEADS = 4
C_WIDTH = C_HEADS * HEAD_DIM
DECAY_RANK = 32
AICL_RANK = 32
VRES_RANK = 16
C_SHIFT_WIDTH = 3 * C_WIDTH + DECAY_RANK + AICL_RANK
GN_EPS = 64e-5
LN_EPS = 1e-5
LOG2E = math.log2(math.e)

LANES = 128
SUBLANES = 8
C_SHIFT_PAD = ((C_SHIFT_WIDTH + LANES - 1) // LANES) * LANES
GA_WIDTH = A_WIDTH + 2 * A_KV_WIDTH + A_WIDTH
GB_WIDTH = 2 * B_WIDTH
GC_WIDTH = C_SHIFT_WIDTH + C_WIDTH
D_IN_PAD = GA_WIDTH + GB_WIDTH + GC_WIDTH
D_MIX = A_WIDTH + B_WIDTH + C_WIDTH
PC_R, PC_K, PC_V, PC_KAP, PC_B, PC_LW, PC_G = range(7)
PC_WIDTH = 7 * C_WIDTH

CHUNK = 64
STACK = C_HEADS * CHUNK
RWKV_GROUP = 8
ATT_TILE = 1024
LRU_TILE = 256
LRU_STAGE_ROWS = SUBLANES * (LRU_TILE // SUBLANES + 4)
PROJ_TM = 512
OUT_TM = 1024
FUSED_TM = 512
OUT_SUB = 256
VMEM_LIMIT = 56 * 1024 * 1024


def _mm(a, b):
    return jnp.dot(a.astype(BF16), b.astype(BF16), preferred_element_type=F32)


def _mm_nt(a, b):
    return lax.dot_general(a.astype(BF16), b.astype(BF16), (((1,), (1,)), ((), ())),
                           preferred_element_type=F32)


def _mm_tn(a, b):
    return lax.dot_general(a.astype(BF16), b.astype(BF16), (((0,), (0,)), ((), ())),
                           preferred_element_type=F32)


def _split_bf16(x, parts):
    out = []
    rem = x
    for _ in range(parts):
        hi = rem.astype(BF16)
        out.append(hi)
        rem = rem - hi.astype(F32)
    return out


def _mm_exact_lhs(m_bf16, x, parts):
    n = x.shape[1]
    t = jnp.dot(m_bf16, jnp.concatenate(_split_bf16(x, parts), axis=1), preferred_element_type=F32)
    acc = t[:, 0:n]
    for p in range(1, parts):
        acc = acc + t[:, p * n:(p + 1) * n]
    return acc


def _sigmoid(x):
    return 1.0 / (1.0 + jnp.exp2(x * (-LOG2E)))


def _silu(x):
    return x * _sigmoid(x)


def _softplus(x):
    return jnp.maximum(x, 0.0) + jnp.log(1.0 + jnp.exp(-jnp.abs(x)))


def _layer_spec(shape, layer, single_buffer=False):
    index_map = lambda *_: (layer,) + (0,) * len(shape)
    if single_buffer:
        return pl.BlockSpec((None,) + shape, index_map, pipeline_mode=pl.Buffered(1))
    return pl.BlockSpec((None,) + shape, index_map)


def _const_spec(shape):
    return pl.BlockSpec(shape, lambda *_: (0,) * len(shape))


def _rwkv_tokenwise(cs, g, prev_row, mu, vec, w2p, a2p, ones_bd):
    w0, a0, k_k, k_a = (vec[i:i + 1, :] for i in range(4))
    row = lax.broadcasted_iota(jnp.int32, cs.shape, 0)
    prev = jnp.where(row == 0, prev_row, pltpu.roll(cs, 1, axis=0))
    xs = cs + (prev - cs) * mu
    r = xs[:, 0:C_WIDTH]
    k = xs[:, C_WIDTH:2 * C_WIDTH]
    v = xs[:, 2 * C_WIDTH:3 * C_WIDTH]
    la = xs[:, 3 * C_WIDTH:C_SHIFT_PAD]
    lw = (-LOG2E * math.exp(-0.5)) * _sigmoid(w0 + _mm(jnp.tanh(la), w2p))
    a = _sigmoid(a0 + _mm(la, a2p))
    kk = k * k_k
    kap = kk * lax.rsqrt(_mm(kk * kk, ones_bd) + 1e-12)
    k2 = k * (1.0 + (a - 1.0) * k_a)
    return jnp.concatenate([r, k2, v, kap, kap * a, lw, g], axis=1), cs[cs.shape[0] - 1:, :]


def _proj_rows(x16, w_ref, row0, nrows, seq, tok_refs, last_ref, oa_ref, ob_ref, oc_ref):
    mu_ref, vec_ref, w2_ref, a2_ref, ones_ref = tok_refs
    nt = (((1,), (1,)), ((), ()))
    rows = pl.ds(row0, nrows)
    oa_ref[rows, :] = lax.dot_general(x16, w_ref[0:GA_WIDTH, :], nt,
                                      preferred_element_type=F32).astype(BF16)
    ob_ref[rows, :] = lax.dot_general(x16, w_ref[GA_WIDTH:GA_WIDTH + GB_WIDTH, :], nt,
                                      preferred_element_type=F32)
    c0 = GA_WIDTH + GB_WIDTH
    cs = lax.dot_general(x16, w_ref[c0:c0 + C_SHIFT_PAD, :], nt, preferred_element_type=F32)
    g = lax.dot_general(x16, w_ref[c0 + C_SHIFT_WIDTH:D_IN_PAD, :], nt, preferred_element_type=F32)
    first = pl.program_id(0) * oa_ref.shape[0] + row0

    def finish():
        prev_row = jnp.where(first % seq == 0, 0.0, last_ref[0:1, :])
        slab, last = _rwkv_tokenwise(cs, g, prev_row, mu_ref[...], vec_ref[...], w2_ref[...],
                                     a2_ref[...], ones_ref[...])
        oc_ref[rows, :] = slab
        last_ref[0:1, :] = last

    return finish


def _init_last(last_ref):
    @pl.when(pl.program_id(0) == 0)
    def _():
        last_ref[...] = jnp.zeros_like(last_ref)


def _proj_kernel(seq, x_ref, w_ref, mu_ref, vec_ref, w2_ref, a2_ref, ones_ref, oa_ref, ob_ref, oc_ref,
                 last_ref):
    _init_last(last_ref)
    pending = None
    for row0 in range(0, PROJ_TM, OUT_SUB):
        finish = _proj_rows(x_ref[pl.ds(row0, OUT_SUB), :].astype(BF16), w_ref, row0, OUT_SUB, seq,
                            (mu_ref, vec_ref, w2_ref, a2_ref, ones_ref), last_ref, oa_ref, ob_ref,
                            oc_ref)
        if pending is not None:
            pending()
        pending = finish
    pending()


def _tok_specs(layer):
    return [_layer_spec((1, C_SHIFT_PAD), layer), _layer_spec((8, C_WIDTH), layer),
            _layer_spec((LANES, C_WIDTH), layer), _layer_spec((LANES, C_WIDTH), layer),
            _const_spec((STACK, STACK))]


def _proj(x2, w_all, tok_args, layer, seq):
    n = x2.shape[0]
    return pl.pallas_call(
        functools.partial(_proj_kernel, seq),
        grid=(n // PROJ_TM,),
        in_specs=[pl.BlockSpec((PROJ_TM, D_MODEL), lambda i: (i, 0)),
                  _layer_spec((D_IN_PAD, D_MODEL), layer)] + _tok_specs(layer),
        out_specs=[pl.BlockSpec((PROJ_TM, GA_WIDTH), lambda i: (i, 0)),
                   pl.BlockSpec((PROJ_TM, GB_WIDTH), lambda i: (i, 0)),
                   pl.BlockSpec((PROJ_TM, PC_WIDTH), lambda i: (i, 0))],
        out_shape=[jax.ShapeDtypeStruct((n, GA_WIDTH), BF16),
                   jax.ShapeDtypeStruct((n, GB_WIDTH), F32),
                   jax.ShapeDtypeStruct((n, PC_WIDTH), F32)],
        scratch_shapes=[pltpu.VMEM((SUBLANES, C_SHIFT_PAD), F32)],
        compiler_params=pltpu.CompilerParams(dimension_semantics=("arbitrary",),
                                             vmem_limit_bytes=VMEM_LIMIT),
        name="proj",
    )(x2, w_all, *tok_args)


def _attn_kernel(layer, sink_ref, cur_ref, prev_ref, o_ref):
    n = pl.program_id(1)
    blk = ATT_BLOCK
    nsub = ATT_TILE // blk
    grp = A_Q_HEADS // A_KV_HEADS
    lo = lax.broadcasted_iota(jnp.int32, (blk, A_KV_WIDTH), 1) < HEAD_DIM

    def variants(x16):
        x = x16.astype(F32)
        xs = pltpu.roll(x, HEAD_DIM, axis=1)
        return ((jnp.where(lo, x, 0.0).astype(BF16), jnp.where(lo, 0.0, xs).astype(BF16)),
                (jnp.where(lo, xs, 0.0).astype(BF16), jnp.where(lo, 0.0, x).astype(BF16)))

    k0, v0 = A_WIDTH, A_WIDTH + A_KV_WIDTH
    g0 = A_WIDTH + 2 * A_KV_WIDTH
    kblk = [variants(prev_ref[0, :, 0:A_KV_WIDTH])]
    vblk = [variants(prev_ref[0, :, A_KV_WIDTH:2 * A_KV_WIDTH])]
    for j in range(nsub):
        kblk.append(variants(cur_ref[0, j * blk:(j + 1) * blk, k0:k0 + A_KV_WIDTH]))
        vblk.append(variants(cur_ref[0, j * blk:(j + 1) * blk, v0:v0 + A_KV_WIDTH]))

    qi = lax.broadcasted_iota(jnp.int32, (blk, 2 * blk), 0)
    kj = lax.broadcasted_iota(jnp.int32, (blk, 2 * blk), 1)
    diff = qi + blk - kj
    band = (diff >= 0) & (diff < blk)
    first = band & (kj + (n * nsub - 1) * blk >= 0)

    for j in range(nsub):
        mask = first if j == 0 else band
        rows = slice(j * blk, (j + 1) * blk)
        scores = []
        for p in range(A_Q_HEADS // 2):
            h = (2 * p) // grp
            qp = cur_ref[0, rows, p * LANES:(p + 1) * LANES]
            for half in range(2):
                kb = jnp.concatenate([kblk[j][h][half], kblk[j + 1][h][half]], axis=0)
                scores.append(lax.dot_general(qp, kb, (((1,), (1,)), ((), ())),
                                              preferred_element_type=F32))
        probs, inv = [], []
        for idx, s in enumerate(scores):
            sink2 = sink_ref[layer, idx] * LOG2E
            s = jnp.where(mask, s, -jnp.inf)
            m = jnp.maximum(jnp.max(s, axis=-1, keepdims=True), sink2)
            e = jnp.exp2(s - m)
            den = jnp.sum(e, axis=-1, keepdims=True) + jnp.exp2(sink2 - m)
            probs.append(e.astype(BF16))
            inv.append(1.0 / den)
        for p in range(A_Q_HEADS // 2):
            h = (2 * p) // grp
            acc = None
            for half in range(2):
                vb = jnp.concatenate([vblk[j][h][half], vblk[j + 1][h][half]], axis=0)
                o = jnp.dot(probs[2 * p + half], vb, preferred_element_type=F32) * inv[2 * p + half]
                acc = o if acc is None else acc + o
            g = cur_ref[0, rows, g0 + p * LANES:g0 + (p + 1) * LANES].astype(F32)
            o_ref[0, rows, p * LANES:(p + 1) * LANES] = (acc * _silu(g)).astype(BF16)


def _attn(pa, sinks, layer):
    bsz, seq, _ = pa.shape
    nsub = ATT_TILE // ATT_BLOCK
    kv_blk = A_WIDTH // (2 * A_KV_WIDTH)
    return pl.pallas_call(
        functools.partial(_attn_kernel, layer),
        grid=(bsz, seq // ATT_TILE),
        in_specs=[pl.BlockSpec(memory_space=pltpu.SMEM),
                  pl.BlockSpec((1, ATT_TILE, GA_WIDTH), lambda b, n: (b, n, 0)),
                  pl.BlockSpec((1, ATT_BLOCK, 2 * A_KV_WIDTH),
                               lambda b, n: (b, jnp.maximum(n * nsub - 1, 0), kv_blk))],
        out_specs=pl.BlockSpec((1, ATT_TILE, A_WIDTH), lambda b, n: (b, n, 0)),
        out_shape=jax.ShapeDtypeStruct((bsz, seq, A_WIDTH), BF16),
        compiler_params=pltpu.CompilerParams(dimension_semantics=("arbitrary", "arbitrary"),
                                             vmem_limit_bytes=VMEM_LIMIT),
        name="attn",
    )(sinks, pa, pa)


def _lru_kernel(pb_ref, cw_ref, vec_ref, wa_ref, wx_ref, o_ref, in_ref, stage_ref):
    seq = pb_ref.shape[1]
    tile = LRU_TILE
    nseg = SUBLANES
    nstep = tile // nseg
    conv_b = vec_ref[0:1, :]
    ba = vec_ref[1:2, :]
    bx = vec_ref[2:3, :]
    lam = vec_ref[3:4, :]
    neg_c_sp = -LRU_C * _softplus(-lam)
    cw = [cw_ref[j:j + 1, :] for j in range(CONV_WIDTH)]
    sub0 = lax.broadcasted_iota(jnp.int32, (nseg, B_WIDTH), 0) == 0

    nhalf = B_WIDTH // LANES

    pitch = nstep + 4
    assert pitch % SUBLANES == 4

    def rows_of(step):
        return pl.ds(step, nseg, stride=pitch)

    def permuted(k0, step):
        return jnp.concatenate([in_ref[k0 + k, rows_of(step), :] for k in range(nhalf)], axis=1)

    hc = jnp.zeros((1, B_WIDTH), F32)
    tail = [jnp.zeros((1, B_WIDTH), F32)] * (CONV_WIDTH - 1)
    for ti in range(seq // tile):
        t0 = ti * tile
        for k in range(2 * nhalf):
            for sgi in range(nseg):
                in_ref[k, sgi * pitch:sgi * pitch + nstep, :] = pb_ref[
                    0, t0 + sgi * nstep:t0 + (sgi + 1) * nstep, k * LANES:(k + 1) * LANES]
        xs = [permuted(0, v) for v in range(nstep)]
        wrapped = []
        for i in range(CONV_WIDTH - 1):
            src_v = xs[nstep - (CONV_WIDTH - 1) + i]
            wrapped.append(jnp.where(sub0, tail[i], pltpu.roll(src_v, 1, axis=0)))
        x_all = jnp.concatenate(xs, axis=0)
        xc = cw[CONV_WIDTH - 1] * x_all + conv_b
        for j in range(1, CONV_WIDTH):
            shifted = jnp.concatenate(wrapped[CONV_WIDTH - 1 - j:] + xs[:nstep - j], axis=0)
            xc = xc + cw[CONV_WIDTH - 1 - j] * shifted
        tail = [xs[nstep - (CONV_WIDTH - 1) + i][nseg - 1:nseg, :] for i in range(CONV_WIDTH - 1)]

        r = _sigmoid(_mm(xc, wa_ref[...]) + ba)
        ig = _sigmoid(_mm(xc, wx_ref[...]) + bx)
        a = jnp.exp(neg_c_sp * r)
        u = jnp.sqrt(1.0 - a * a) * (ig * xc)
        hl = [u[0:nseg]]
        al = [a[0:nseg]]
        for v in range(1, nstep):
            av = a[v * nseg:(v + 1) * nseg]
            hl.append(av * hl[-1] + u[v * nseg:(v + 1) * nseg])
            al.append(av * al[-1])
        carry = hc
        enter = []
        for sgi in range(nseg):
            enter.append(carry)
            carry = hl[-1][sgi:sgi + 1, :] + al[-1][sgi:sgi + 1, :] * carry
        hc = carry
        enter = jnp.concatenate(enter, axis=0)
        for v in range(nstep):
            out = (hl[v] + al[v] * enter) * _silu(permuted(nhalf, v))
            for k in range(nhalf):
                stage_ref[k, rows_of(v), :] = out[:, k * LANES:(k + 1) * LANES]
        for k in range(nhalf):
            for sgi in range(nseg):
                o_ref[0, t0 + sgi * nstep:t0 + (sgi + 1) * nstep, k * LANES:(k + 1) * LANES] = (
                    stage_ref[k, sgi * pitch:sgi * pitch + nstep, :].astype(BF16))


def _lru(pb, cw_all, vec_all, wa_all, wx_all, layer):
    bsz, seq, _ = pb.shape
    return pl.pallas_call(
        _lru_kernel,
        grid=(bsz,),
        in_specs=[pl.BlockSpec((1, seq, GB_WIDTH), lambda b: (b, 0, 0)),
                  _layer_spec((CONV_WIDTH, B_WIDTH), layer), _layer_spec((4, B_WIDTH), layer),
                  _layer_spec((B_WIDTH, B_WIDTH), layer), _layer_spec((B_WIDTH, B_WIDTH), layer)],
        out_specs=pl.BlockSpec((1, seq, B_WIDTH), lambda b: (b, 0, 0)),
        out_shape=jax.ShapeDtypeStruct((bsz, seq, B_WIDTH), BF16),
        scratch_shapes=[pltpu.VMEM((GB_WIDTH // LANES, LRU_STAGE_ROWS, LANES), F32),
                        pltpu.VMEM((B_WIDTH // LANES, LRU_STAGE_ROWS, LANES), F32)],
        compiler_params=pltpu.CompilerParams(dimension_semantics=("arbitrary",),
                                             vmem_limit_bytes=VMEM_LIMIT),
        name="lru",
    )(pb, cw_all, vec_all, wa_all, wx_all)


def _stack_heads(x):
    x16 = x.astype(BF16)
    head = lax.broadcasted_iota(jnp.int32, x16.shape, 1) // HEAD_DIM
    return jnp.concatenate([jnp.where(head == h, x16, jnp.zeros_like(x16)) for h in range(C_HEADS)],
                           axis=0)


def _unstack_heads(x):
    head = lax.broadcasted_iota(jnp.int32, (CHUNK, STACK), 1) // HEAD_DIM
    out = x[(C_HEADS - 1) * CHUNK:C_HEADS * CHUNK]
    for h in range(C_HEADS - 2, -1, -1):
        out = jnp.where(head == h, x[h * CHUNK:(h + 1) * CHUNK], out)
    return out


def _rwkv_pre(slab, vfirst, vec, v1p, v2p, tril_l, sl, le, eye_w):
    r, k2, v, kap, b, lw = (slab[:, i * C_WIDTH:(i + 1) * C_WIDTH]
                            for i in (PC_R, PC_K, PC_V, PC_KAP, PC_B, PC_LW))
    r_k = vec[4:5, :]
    v0 = vec[7:8, :]
    if vfirst is not None:
        v = v + (vfirst - v) * _sigmoid(v0 + _mm(_mm(v, v1p), v2p))
    yield
    cum = _mm_exact_lhs(tril_l, lw, 2)
    cum_end = cum[CHUNK - 1:CHUNK, :]
    e_in = jnp.exp2(cum)
    e_ex = jnp.exp2(cum - lw)
    e_neg = jnp.exp2(-cum)
    e_end = jnp.exp2(cum_end - cum)
    p_end = jnp.exp2(cum_end)

    yield
    kt = kap * e_ex
    rt = r * e_in
    kt_st = _stack_heads(kt)
    vs_st = _stack_heads(v)
    gram = _mm_nt(jnp.concatenate([kt, rt], axis=0),
                  jnp.concatenate([_stack_heads(b * e_neg), _stack_heads(k2 * e_neg)], axis=0))
    a_ab = jnp.where(sl, gram[0:CHUNK, 0:STACK], 0.0)
    a_ak = jnp.where(sl, gram[0:CHUNK, STACK:2 * STACK], 0.0)
    a_rb = jnp.where(le, gram[CHUNK:2 * CHUNK, 0:STACK], 0.0)
    a_rk = jnp.where(le, gram[CHUNK:2 * CHUNK, STACK:2 * STACK], 0.0)

    yield
    pw = -a_ab
    t_w = eye_w + pw
    av = _mm(jnp.concatenate([a_ak, a_rk], axis=0), vs_st)
    pw = _mm(pw, _stack_heads(pw))
    for _ in range(1, int(math.log2(CHUNK)) - 1):
        yield
        both = _mm(jnp.concatenate([t_w, pw], axis=0), _stack_heads(pw))
        t_w = t_w + both[0:CHUNK]
        pw = both[CHUNK:2 * CHUNK]
    yield
    t_w = t_w + _mm(t_w, _stack_heads(pw))
    yield
    tx = _mm(t_w, jnp.concatenate([kt_st, _stack_heads(av[0:CHUNK])], axis=1))
    yield
    m1 = tx[:, 0:STACK]
    c1 = tx[:, STACK:2 * STACK]
    ab = _mm(a_rb, jnp.concatenate([_stack_heads(m1), _stack_heads(c1)], axis=1))
    q_n = rt - ab[:, 0:STACK]
    d_n = av[CHUNK:2 * CHUNK] - ab[:, STACK:2 * STACK]
    bh = b * e_end
    g_w = eye_w * p_end - _unstack_heads(_mm_tn(bh, m1))
    c_w = _unstack_heads(_mm_tn(jnp.concatenate([k2 * e_end, bh], axis=0),
                                jnp.concatenate([v, -c1], axis=0)))
    bonus_arg = r * k2 * r_k
    return q_n, d_n, g_w, c_w, bonus_arg, v


RWKV_SAVED = 6


def _rwkv_kernel(use_vres, *refs):
    if use_vres:
        (pc_ref, vf_ref, vec_ref, v1_ref, v2_ref, tril_ref, bd_ref,
         sl_ref, le_ref, eye_ref, y_ref, h_ref, pre_ref) = refs
    else:
        (pc_ref, vec_ref, v1_ref, v2_ref, tril_ref, bd_ref,
         sl_ref, le_ref, eye_ref, y_ref, h_ref, pre_ref) = refs
        vf_ref = None
    seq = pc_ref.shape[1]
    ngroups = seq // (CHUNK * RWKV_GROUP)
    h_ref[...] = jnp.zeros_like(h_ref)

    def chunk_start(gi, j):
        return pl.multiple_of((gi * RWKV_GROUP + j) * CHUNK, CHUNK)

    def pre_generators(gi):
        sl = sl_ref[...] > 0.0
        le = le_ref[...] > 0.0
        gens = []
        for j in range(RWKV_GROUP):
            t0 = chunk_start(gi, j)
            slab = pc_ref[0, pl.ds(t0, CHUNK), 0:PC_G * C_WIDTH]
            vfirst = vf_ref[0, pl.ds(t0, CHUNK), :] if use_vres else None
            gens.append(_rwkv_pre(slab, vfirst, vec_ref[...], v1_ref[...], v2_ref[...], tril_ref[...],
                                  sl, le, eye_ref[...]))
        return gens

    def chain_tasks(gi):
        ys = [None] * RWKV_GROUP
        stats = [None] * RWKV_GROUP
        state = {}

        def step(j):
            def run():
                h_w = h_ref[...] if j == 0 else state["h"]
                q_n, d_n, g_w, c_w = (pre_ref[j * RWKV_SAVED + k] for k in range(4))
                res = _mm(jnp.concatenate([q_n, g_w], axis=0), _stack_heads(h_w))
                ys[j] = res[0:CHUNK] + d_n
                state["h"] = res[CHUNK:2 * CHUNK] + c_w
                if j == RWKV_GROUP - 1:
                    h_ref[...] = state["h"]
            return run

        def sums():
            for j in range(RWKV_GROUP):
                parts = _split_bf16(ys[j], 2) + [pre_ref[j * RWKV_SAVED + 4].astype(BF16)]
                s = jnp.dot(jnp.concatenate(parts, axis=0), bd_ref[...], preferred_element_type=F32)
                yc = ys[j] - (s[0:CHUNK] + s[CHUNK:2 * CHUNK]) * (1.0 / HEAD_DIM)
                stats[j] = (yc, s[2 * CHUNK:3 * CHUNK])

        def variance():
            for j in range(RWKV_GROUP):
                yc, bonus = stats[j]
                yv = _mm(yc * yc, bd_ref[...]) * (1.0 / HEAD_DIM)
                stats[j] = (yc, bonus, yv)

        def finish():
            gn_w = vec_ref[5:6, :]
            gn_b = vec_ref[6:7, :]
            for j in range(RWKV_GROUP):
                t0 = chunk_start(gi, j)
                yc, bonus, yv = stats[j]
                yn = yc * lax.rsqrt(yv + GN_EPS) * gn_w + gn_b
                g = pc_ref[0, pl.ds(t0, CHUNK), PC_G * C_WIDTH:PC_WIDTH]
                out = (yn + bonus * pre_ref[j * RWKV_SAVED + 5]) * _silu(g)
                y_ref[0, pl.ds(t0, CHUNK), :] = out.astype(BF16)

        return [step(j) for j in range(RWKV_GROUP)] + [sums, variance, finish]

    def run_group(gi, tasks):
        gens = pre_generators(gi)
        tasks = list(tasks)
        pre = [None] * RWKV_GROUP
        sweep = 0
        while any(p is None for p in pre):
            for j, gen in enumerate(gens):
                if pre[j] is None:
                    try:
                        next(gen)
                    except StopIteration as stop:
                        pre[j] = stop.value
            sweep += 1
            if tasks and sweep >= 2:
                tasks.pop(0)()
        for t in tasks:
            t()
        for j in range(RWKV_GROUP):
            for k in range(RWKV_SAVED):
                pre_ref[j * RWKV_SAVED + k] = pre[j][k]

    run_group(0, [])

    def body(i, carry):
        run_group(i, chain_tasks(i - 1))
        return carry

    lax.fori_loop(1, ngroups, body, 0)
    for t in chain_tasks(ngroups - 1):
        t()


def _rwkv_consts():
    idx = jnp.arange(STACK)
    same = (idx[:, None] // CHUNK) == (idx[None, :] // CHUNK)
    ci = jnp.arange(CHUNK)
    tril_l = (ci[None, :] <= ci[:, None]).astype(BF16)
    s_loc = idx[None, :] % CHUNK
    sl = (s_loc < ci[:, None]).astype(F32)
    le = (s_loc <= ci[:, None]).astype(F32)
    eye_w = (s_loc == ci[:, None]).astype(F32)
    return tril_l, same.astype(BF16), sl, le, eye_w


def _rwkv(pc, pc_first, vec_all, v1_all, v2_all, consts, layer):
    bsz, seq, _ = pc.shape
    use_vres = pc_first is not None
    sq = (STACK, STACK)
    wide = (CHUNK, STACK)
    in_specs = [pl.BlockSpec((1, seq, PC_WIDTH), lambda b: (b, 0, 0))]
    if use_vres:
        in_specs.append(pl.BlockSpec((1, seq, C_WIDTH), lambda b: (b, 0, PC_V)))
    in_specs += [_layer_spec((8, C_WIDTH), layer),
                 _layer_spec((C_WIDTH, LANES), layer), _layer_spec((LANES, C_WIDTH), layer),
                 _const_spec((CHUNK, CHUNK)), _const_spec(sq),
                 _const_spec(wide), _const_spec(wide), _const_spec(wide)]
    args = (pc,) + ((pc_first,) if use_vres else ()) + (vec_all, v1_all, v2_all)
    return pl.pallas_call(
        functools.partial(_rwkv_kernel, use_vres),
        grid=(bsz,),
        in_specs=in_specs,
        out_specs=pl.BlockSpec((1, seq, C_WIDTH), lambda b: (b, 0, 0)),
        out_shape=jax.ShapeDtypeStruct((bsz, seq, C_WIDTH), BF16),
        scratch_shapes=[pltpu.VMEM(wide, F32), pltpu.VMEM((RWKV_GROUP * RWKV_SAVED,) + wide, F32)],
        compiler_params=pltpu.CompilerParams(dimension_semantics=("arbitrary",),
                                             vmem_limit_bytes=VMEM_LIMIT),
        name="rwkv",
    )(*args, *consts)


def _out_kernel(alpha, tm, seq, fused, ya_ref, yb_ref, yc_ref, x_ref, w_ref, ln_ref, *rest):
    if fused:
        win_ref, *tok_refs = rest[0:6]
        o_ref, oa_ref, ob_ref, oc_ref, last_ref = rest[6:]
        _init_last(last_ref)
    else:
        (o_ref,) = rest

    def project(rows):
        acc = jnp.dot(ya_ref[rows, :], w_ref[0:A_WIDTH, :], preferred_element_type=F32)
        acc = acc + jnp.dot(yb_ref[rows, :], w_ref[A_WIDTH:A_WIDTH + B_WIDTH, :],
                            preferred_element_type=F32)
        return acc + jnp.dot(yc_ref[rows, :], w_ref[A_WIDTH + B_WIDTH:D_MIX, :],
                             preferred_element_type=F32)

    def norm(rows, acc):
        z = alpha * x_ref[rows, :] + acc
        mu = jnp.mean(z, axis=-1, keepdims=True)
        zc = z - mu
        var = jnp.mean(zc * zc, axis=-1, keepdims=True)
        xn = zc * lax.rsqrt(var + LN_EPS) * ln_ref[0:1, :] + ln_ref[1:2, :]
        o_ref[rows, :] = xn
        if not fused:
            return None
        return _proj_rows(xn.astype(BF16), win_ref, rows.start, OUT_SUB, seq, tok_refs, last_ref,
                          oa_ref, ob_ref, oc_ref)

    subs = [slice(s * OUT_SUB, (s + 1) * OUT_SUB) for s in range(tm // OUT_SUB)]
    acc_prev = project(subs[0])
    pending = None
    for s in range(len(subs)):
        acc_next = project(subs[s + 1]) if s + 1 < len(subs) else None
        finish = norm(subs[s], acc_prev)
        if pending is not None:
            pending()
        pending = finish
        acc_prev = acc_next
    if pending is not None:
        pending()


def _out(ya, yb, yc, x2, w_all, ln_all, alpha, layer, seq, w_in_all=None, tok_args=()):
    n = x2.shape[0]
    fused = w_in_all is not None
    tm = FUSED_TM if fused else OUT_TM
    row = lambda w: pl.BlockSpec((tm, w), lambda i: (i, 0))
    in_specs = [row(A_WIDTH), row(B_WIDTH), row(C_WIDTH), row(D_MODEL),
                _layer_spec((D_MIX, D_MODEL), layer, single_buffer=True),
                _layer_spec((2, D_MODEL), layer)]
    out_specs = [row(D_MODEL)]
    out_shape = [jax.ShapeDtypeStruct((n, D_MODEL), F32)]
    args = [ya, yb, yc, x2, w_all, ln_all]
    if fused:
        in_specs.append(_layer_spec((D_IN_PAD, D_MODEL), layer + 1, single_buffer=True))
        in_specs += _tok_specs(layer + 1)
        out_specs += [row(GA_WIDTH), row(GB_WIDTH), row(PC_WIDTH)]
        out_shape += [jax.ShapeDtypeStruct((n, GA_WIDTH), BF16),
                      jax.ShapeDtypeStruct((n, GB_WIDTH), F32),
                      jax.ShapeDtypeStruct((n, PC_WIDTH), F32)]
        args += [w_in_all, *tok_args]
    return pl.pallas_call(
        functools.partial(_out_kernel, alpha, tm, seq, fused),
        grid=(n // tm,),
        in_specs=in_specs,
        out_specs=out_specs,
        out_shape=out_shape,
        scratch_shapes=[pltpu.VMEM((SUBLANES, C_SHIFT_PAD), F32)] if fused else [],
        compiler_params=pltpu.CompilerParams(dimension_semantics=("arbitrary",),
                                             vmem_limit_bytes=VMEM_LIMIT),
        name="outproj_proj" if fused else "outproj",
    )(*args)


def _block_diag_all(w):
    depth, nblk, d, _ = w.shape
    eye = jnp.eye(nblk, dtype=w.dtype)
    return (w[:, :, :, None, :] * eye[None, :, None, :, None]).reshape(depth, nblk * d, nblk * d)


def kernel(x, w_in, w_out, ln_g, ln_b, attn_sinks, conv_w, conv_b, lru_wa, lru_ba, lru_wx, lru_bx,
           lru_lambda, rwkv_mu, rwkv_w0, rwkv_w2, rwkv_a0, rwkv_a2, rwkv_kk, rwkv_ka, rwkv_rk,
           rwkv_gn_w, rwkv_gn_b, rwkv_v0, rwkv_v1, rwkv_v2):
    bsz, seq, dm = x.shape
    depth = w_in.shape[0]
    alpha = (2 * depth) ** 0.25
    n = bsz * seq
    pad_w = C_SHIFT_PAD - C_SHIFT_WIDTH

    q_scale = jnp.concatenate([jnp.full((A_WIDTH,), HEAD_DIM ** -0.5 * LOG2E, F32),
                               jnp.ones((w_in.shape[2] - A_WIDTH,), F32)])
    w_in_p = (jnp.swapaxes(w_in, 1, 2) * q_scale[None, :, None]).astype(BF16)
    w_out_b = w_out.astype(BF16)
    ln_all = jnp.stack([ln_g, ln_b], axis=1)
    lru_vec = jnp.stack([conv_b, lru_ba, lru_bx, lru_lambda], axis=1)
    wa_bd = _block_diag_all(lru_wa).astype(BF16)
    wx_bd = _block_diag_all(lru_wx).astype(BF16)
    mu_all = jnp.pad(rwkv_mu, ((0, 0), (0, pad_w)))[:, None, :]
    v0_all = jnp.pad(rwkv_v0, ((1, 0), (0, 0)))
    vec_all = jnp.stack([rwkv_w0, rwkv_a0, rwkv_kk, rwkv_ka, rwkv_rk.reshape(depth, C_WIDTH),
                         rwkv_gn_w, rwkv_gn_b, v0_all], axis=1)
    w2_all = jnp.pad(rwkv_w2, ((0, 0), (0, LANES - DECAY_RANK), (0, 0))).astype(BF16)
    a2_all = jnp.pad(rwkv_a2, ((0, 0), (DECAY_RANK, LANES - DECAY_RANK - AICL_RANK), (0, 0))).astype(BF16)
    v1_all = jnp.pad(rwkv_v1, ((1, 0), (0, 0), (0, LANES - VRES_RANK))).astype(BF16)
    v2_all = jnp.pad(rwkv_v2, ((1, 0), (0, LANES - VRES_RANK), (0, 0))).astype(BF16)
    consts = _rwkv_consts()

    tok_args = (mu_all, vec_all, w2_all, a2_all, consts[1])
    x2 = x.reshape(n, dm)
    pc_first = None
    pa, pb, pc = _proj(x2, w_in_p, tok_args, 0, seq)
    for l in range(depth):
        pc3 = pc.reshape(bsz, seq, PC_WIDTH)
        ya = _attn(pa.reshape(bsz, seq, GA_WIDTH), attn_sinks, l)
        yb = _lru(pb.reshape(bsz, seq, GB_WIDTH), conv_w, lru_vec, wa_bd, wx_bd, l)
        yc = _rwkv(pc3, pc_first, vec_all, v1_all, v2_all, consts, l)
        if l == 0:
            pc_first = pc3
        outs = _out(ya.reshape(n, A_WIDTH), yb.reshape(n, B_WIDTH), yc.reshape(n, C_WIDTH), x2,
                    w_out_b, ln_all, alpha, l, seq, w_in_p if l + 1 < depth else None, tok_args)
        x2 = outs[0]
        if l + 1 < depth:
            pa, pb, pc = outs[1:]
    return x2.reshape(bsz, seq, dm)
```

```python
import functools
import math

import jax
import jax.numpy as jnp
from jax import lax
from jax.experimental import pallas as pl
from jax.experimental.pallas import tpu as pltpu

F32 = jnp.float32
BF16 = jnp.bfloat16

D_MODEL = 1024
HEAD_DIM = 64
A_Q_HEADS = 8
A_KV_HEADS = 2
A_WIDTH = A_Q_HEADS * HEAD_DIM
A_KV_WIDTH = A_KV_HEADS * HEAD_DIM
ATT_BLOCK = 128
B_WIDTH = 256
B_BLOCKS = 4
CONV_WIDTH = 4
LRU_C = 8.0
C_HEADS = 4
C_WIDTH = C_HEADS * HEAD_DIM
DECAY_RANK = 32
AICL_RANK = 32
VRES_RANK = 16
C_SHIFT_WIDTH = 3 * C_WIDTH + DECAY_RANK + AICL_RANK
GN_EPS = 64e-5
LN_EPS = 1e-5
LOG2E = math.log2(math.e)

LANES = 128
SUBLANES = 8
C_SHIFT_PAD = ((C_SHIFT_WIDTH + LANES - 1) // LANES) * LANES
GA_WIDTH = A_WIDTH + 2 * A_KV_WIDTH + A_WIDTH
GB_WIDTH = 2 * B_WIDTH
GC_WIDTH = C_SHIFT_WIDTH + C_WIDTH
D_IN_PAD = GA_WIDTH + GB_WIDTH + GC_WIDTH
D_MIX = A_WIDTH + B_WIDTH + C_WIDTH
PC_R, PC_K, PC_V, PC_KAP, PC_B, PC_LW, PC_G = range(7)
PC_WIDTH = 7 * C_WIDTH

CHUNK = 64
STACK = C_HEADS * CHUNK
RWKV_GROUP = 8
ATT_TILE = 1024
LRU_TILE = 256
LRU_STAGE_ROWS = SUBLANES * (LRU_TILE // SUBLANES + 4)
PROJ_TM = 512
OUT_TM = 1024
FUSED_TM = 512
OUT_SUB = 256
VMEM_LIMIT = 56 * 1024 * 1024


def _mm(a, b):
    return jnp.dot(a.astype(BF16), b.astype(BF16), preferred_element_type=F32)


def _mm_nt(a, b):
    return lax.dot_general(a.astype(BF16), b.astype(BF16), (((1,), (1,)), ((), ())),
                           preferred_element_type=F32)


def _mm_tn(a, b):
    return lax.dot_general(a.astype(BF16), b.astype(BF16), (((0,), (0,)), ((), ())),
                           preferred_element_type=F32)


def _split_bf16(x, parts):
    out = []
    rem = x
    for _ in range(parts):
        hi = rem.astype(BF16)
        out.append(hi)
        rem = rem - hi.astype(F32)
    return out


def _mm_exact_lhs(m_bf16, x, parts):
    n = x.shape[1]
    t = jnp.dot(m_bf16, jnp.concatenate(_split_bf16(x, parts), axis=1), preferred_element_type=F32)
    acc = t[:, 0:n]
    for p in range(1, parts):
        acc = acc + t[:, p * n:(p + 1) * n]
    return acc


def _sigmoid(x):
    return 1.0 / (1.0 + jnp.exp2(x * (-LOG2E)))


def _silu(x):
    return x * _sigmoid(x)


def _softplus(x):
    return jnp.maximum(x, 0.0) + jnp.log(1.0 + jnp.exp(-jnp.abs(x)))


def _layer_spec(shape, layer, single_buffer=False):
    index_map = lambda *_: (layer,) + (0,) * len(shape)
    if single_buffer:
        return pl.BlockSpec((None,) + shape, index_map, pipeline_mode=pl.Buffered(1))
    return pl.BlockSpec((None,) + shape, index_map)


def _const_spec(shape):
    return pl.BlockSpec(shape, lambda *_: (0,) * len(shape))


def _rwkv_tokenwise(cs, g, prev_row, mu, vec, w2p, a2p, ones_bd):
    w0, a0, k_k, k_a = (vec[i:i + 1, :] for i in range(4))
    row = lax.broadcasted_iota(jnp.int32, cs.shape, 0)
    prev = jnp.where(row == 0, prev_row, pltpu.roll(cs, 1, axis=0))
    xs = cs + (prev - cs) * mu
    r = xs[:, 0:C_WIDTH]
    k = xs[:, C_WIDTH:2 * C_WIDTH]
    v = xs[:, 2 * C_WIDTH:3 * C_WIDTH]
    la = xs[:, 3 * C_WIDTH:C_SHIFT_PAD]
    lw = (-LOG2E * math.exp(-0.5)) * _sigmoid(w0 + _mm(jnp.tanh(la), w2p))
    a = _sigmoid(a0 + _mm(la, a2p))
    kk = k * k_k
    kap = kk * lax.rsqrt(_mm(kk * kk, ones_bd) + 1e-12)
    k2 = k * (1.0 + (a - 1.0) * k_a)
    return jnp.concatenate([r, k2, v, kap, kap * a, lw, g], axis=1), cs[cs.shape[0] - 1:, :]


def _proj_rows(x16, w_ref, row0, nrows, seq, tok_refs, last_ref, oa_ref, ob_ref, oc_ref):
    mu_ref, vec_ref, w2_ref, a2_ref, ones_ref = tok_refs
    nt = (((1,), (1,)), ((), ()))
    rows = pl.ds(row0, nrows)
    oa_ref[rows, :] = lax.dot_general(x16, w_ref[0:GA_WIDTH, :], nt,
                                      preferred_element_type=F32).astype(BF16)
    ob_ref[rows, :] = lax.dot_general(x16, w_ref[GA_WIDTH:GA_WIDTH + GB_WIDTH, :], nt,
                                      preferred_element_type=F32)
    c0 = GA_WIDTH + GB_WIDTH
    cs = lax.dot_general(x16, w_ref[c0:c0 + C_SHIFT_PAD, :], nt, preferred_element_type=F32)
    g = lax.dot_general(x16, w_ref[c0 + C_SHIFT_WIDTH:D_IN_PAD, :], nt, preferred_element_type=F32)
    first = pl.program_id(0) * oa_ref.shape[0] + row0

    def finish():
        prev_row = jnp.where(first % seq == 0, 0.0, last_ref[0:1, :])
        slab, last = _rwkv_tokenwise(cs, g, prev_row, mu_ref[...], vec_ref[...], w2_ref[...],
                                     a2_ref[...], ones_ref[...])
        oc_ref[rows, :] = slab
        last_ref[0:1, :] = last

    return finish


def _init_last(last_ref):
    @pl.when(pl.program_id(0) == 0)
    def _():
        last_ref[...] = jnp.zeros_like(last_ref)


def _proj_kernel(seq, x_ref, w_ref, mu_ref, vec_ref, w2_ref, a2_ref, ones_ref, oa_ref, ob_ref, oc_ref,
                 last_ref):
    _init_last(last_ref)
    pending = None
    for row0 in range(0, PROJ_TM, OUT_SUB):
        finish = _proj_rows(x_ref[pl.ds(row0, OUT_SUB), :].astype(BF16), w_ref, row0, OUT_SUB, seq,
                            (mu_ref, vec_ref, w2_ref, a2_ref, ones_ref), last_ref, oa_ref, ob_ref,
                            oc_ref)
        if pending is not None:
            pending()
        pending = finish
    pending()


def _tok_specs(layer):
    return [_layer_spec((1, C_SHIFT_PAD), layer), _layer_spec((8, C_WIDTH), layer),
            _layer_spec((LANES, C_WIDTH), layer), _layer_spec((LANES, C_WIDTH), layer),
            _const_spec((STACK, STACK))]


def _proj(x2, w_all, tok_args, layer, seq):
    n = x2.shape[0]
    return pl.pallas_call(
        functools.partial(_proj_kernel, seq),
        grid=(n // PROJ_TM,),
        in_specs=[pl.BlockSpec((PROJ_TM, D_MODEL), lambda i: (i, 0)),
                  _layer_spec((D_IN_PAD, D_MODEL), layer)] + _tok_specs(layer),
        out_specs=[pl.BlockSpec((PROJ_TM, GA_WIDTH), lambda i: (i, 0)),
                   pl.BlockSpec((PROJ_TM, GB_WIDTH), lambda i: (i, 0)),
                   pl.BlockSpec((PROJ_TM, PC_WIDTH), lambda i: (i, 0))],
        out_shape=[jax.ShapeDtypeStruct((n, GA_WIDTH), BF16),
                   jax.ShapeDtypeStruct((n, GB_WIDTH), F32),
                   jax.ShapeDtypeStruct((n, PC_WIDTH), F32)],
        scratch_shapes=[pltpu.VMEM((SUBLANES, C_SHIFT_PAD), F32)],
        compiler_params=pltpu.CompilerParams(dimension_semantics=("arbitrary",),
                                             vmem_limit_bytes=VMEM_LIMIT),
        name="proj",
    )(x2, w_all, *tok_args)


def _attn_kernel(layer, sink_ref, cur_ref, prev_ref, o_ref):
    n = pl.program_id(1)
    blk = ATT_BLOCK
    nsub = ATT_TILE // blk
    grp = A_Q_HEADS // A_KV_HEADS
    lo = lax.broadcasted_iota(jnp.int32, (blk, A_KV_WIDTH), 1) < HEAD_DIM

    def variants(x16):
        x = x16.astype(F32)
        xs = pltpu.roll(x, HEAD_DIM, axis=1)
        return ((jnp.where(lo, x, 0.0).astype(BF16), jnp.where(lo, 0.0, xs).astype(BF16)),
                (jnp.where(lo, xs, 0.0).astype(BF16), jnp.where(lo, 0.0, x).astype(BF16)))

    k0, v0 = A_WIDTH, A_WIDTH + A_KV_WIDTH
    g0 = A_WIDTH + 2 * A_KV_WIDTH
    kblk = [variants(prev_ref[0, :, 0:A_KV_WIDTH])]
    vblk = [variants(prev_ref[0, :, A_KV_WIDTH:2 * A_KV_WIDTH])]
    for j in range(nsub):
        kblk.append(variants(cur_ref[0, j * blk:(j + 1) * blk, k0:k0 + A_KV_WIDTH]))
        vblk.append(variants(cur_ref[0, j * blk:(j + 1) * blk, v0:v0 + A_KV_WIDTH]))

    qi = lax.broadcasted_iota(jnp.int32, (blk, 2 * blk), 0)
    kj = lax.broadcasted_iota(jnp.int32, (blk, 2 * blk), 1)
    diff = qi + blk - kj
    band = (diff >= 0) & (diff < blk)
    first = band & (kj + (n * nsub - 1) * blk >= 0)

    for j in range(nsub):
        mask = first if j == 0 else band
        rows = slice(j * blk, (j + 1) * blk)
        scores = []
        for p in range(A_Q_HEADS // 2):
            h = (2 * p) // grp
            qp = cur_ref[0, rows, p * LANES:(p + 1) * LANES]
            for half in range(2):
                kb = jnp.concatenate([kblk[j][h][half], kblk[j + 1][h][half]], axis=0)
                scores.append(lax.dot_general(qp, kb, (((1,), (1,)), ((), ())),
                                              preferred_element_type=F32))
        probs, inv = [], []
        for idx, s in enumerate(scores):
            sink2 = sink_ref[layer, idx] * LOG2E
            s = jnp.where(mask, s, -jnp.inf)
            m = jnp.maximum(jnp.max(s, axis=-1, keepdims=True), sink2)
            e = jnp.exp2(s - m)
            den = jnp.sum(e, axis=-1, keepdims=True) + jnp.exp2(sink2 - m)
            probs.append(e.astype(BF16))
            inv.append(1.0 / den)
        for p in range(A_Q_HEADS // 2):
            h = (2 * p) // grp
            acc = None
            for half in range(2):
                vb = jnp.concatenate([vblk[j][h][half], vblk[j + 1][h][half]], axis=0)
                o = jnp.dot(probs[2 * p + half], vb, preferred_element_type=F32) * inv[2 * p + half]
                acc = o if acc is None else acc + o
            g = cur_ref[0, rows, g0 + p * LANES:g0 + (p + 1) * LANES].astype(F32)
            o_ref[0, rows, p * LANES:(p + 1) * LANES] = (acc * _silu(g)).astype(BF16)


def _attn(pa, sinks, layer):
    bsz, seq, _ = pa.shape
    nsub = ATT_TILE // ATT_BLOCK
    kv_blk = A_WIDTH // (2 * A_KV_WIDTH)
    return pl.pallas_call(
        functools.partial(_attn_kernel, layer),
        grid=(bsz, seq // ATT_TILE),
        in_specs=[pl.BlockSpec(memory_space=pltpu.SMEM),
                  pl.BlockSpec((1, ATT_TILE, GA_WIDTH), lambda b, n: (b, n, 0)),
                  pl.BlockSpec((1, ATT_BLOCK, 2 * A_KV_WIDTH),
                               lambda b, n: (b, jnp.maximum(n * nsub - 1, 0), kv_blk))],
        out_specs=pl.BlockSpec((1, ATT_TILE, A_WIDTH), lambda b, n: (b, n, 0)),
        out_shape=jax.ShapeDtypeStruct((bsz, seq, A_WIDTH), BF16),
        compiler_params=pltpu.CompilerParams(dimension_semantics=("arbitrary", "arbitrary"),
                                             vmem_limit_bytes=VMEM_LIMIT),
        name="attn",
    )(sinks, pa, pa)


def _lru_kernel(pb_ref, cw_ref, vec_ref, wa_ref, wx_ref, o_ref, in_ref, stage_ref):
    seq = pb_ref.shape[1]
    tile = LRU_TILE
    nseg = SUBLANES
    nstep = tile // nseg
    conv_b = vec_ref[0:1, :]
    ba = vec_ref[1:2, :]
    bx = vec_ref[2:3, :]
    lam = vec_ref[3:4, :]
    neg_c_sp = -LRU_C * _softplus(-lam)
    cw = [cw_ref[j:j + 1, :] for j in range(CONV_WIDTH)]
    sub0 = lax.broadcasted_iota(jnp.int32, (nseg, B_WIDTH), 0) == 0

    nhalf = B_WIDTH // LANES

    pitch = nstep + 4
    assert pitch % SUBLANES == 4

    def rows_of(step):
        return pl.ds(step, nseg, stride=pitch)

    def permuted(k0, step):
        return jnp.concatenate([in_ref[k0 + k, rows_of(step), :] for k in range(nhalf)], axis=1)

    hc = jnp.zeros((1, B_WIDTH), F32)
    tail = [jnp.zeros((1, B_WIDTH), F32)] * (CONV_WIDTH - 1)
    for ti in range(seq // tile):
        t0 = ti * tile
        for k in range(2 * nhalf):
            for sgi in range(nseg):
                in_ref[k, sgi * pitch:sgi * pitch + nstep, :] = pb_ref[
                    0, t0 + sgi * nstep:t0 + (sgi + 1) * nstep, k * LANES:(k + 1) * LANES]
        xs = [permuted(0, v) for v in range(nstep)]
        wrapped = []
        for i in range(CONV_WIDTH - 1):
            src_v = xs[nstep - (CONV_WIDTH - 1) + i]
            wrapped.append(jnp.where(sub0, tail[i], pltpu.roll(src_v, 1, axis=0)))
        x_all = jnp.concatenate(xs, axis=0)
        xc = cw[CONV_WIDTH - 1] * x_all + conv_b
        for j in range(1, CONV_WIDTH):
            shifted = jnp.concatenate(wrapped[CONV_WIDTH - 1 - j:] + xs[:nstep - j], axis=0)
            xc = xc + cw[CONV_WIDTH - 1 - j] * shifted
        tail = [xs[nstep - (CONV_WIDTH - 1) + i][nseg - 1:nseg, :] for i in range(CONV_WIDTH - 1)]

        r = _sigmoid(_mm(xc, wa_ref[...]) + ba)
        ig = _sigmoid(_mm(xc, wx_ref[...]) + bx)
        a = jnp.exp(neg_c_sp * r)
        u = jnp.sqrt(1.0 - a * a) * (ig * xc)
        hl = [u[0:nseg]]
        al = [a[0:nseg]]
        for v in range(1, nstep):
            av = a[v * nseg:(v + 1) * nseg]
            hl.append(av * hl[-1] + u[v * nseg:(v + 1) * nseg])
            al.append(av * al[-1])
        carry = hc
        enter = []
        for sgi in range(nseg):
            enter.append(carry)
            carry = hl[-1][sgi:sgi + 1, :] + al[-1][sgi:sgi + 1, :] * carry
        hc = carry
        enter = jnp.concatenate(enter, axis=0)
        for v in range(nstep):
            out = (hl[v] + al[v] * enter) * _silu(permuted(nhalf, v))
            for k in range(nhalf):
                stage_ref[k, rows_of(v), :] = out[:, k * LANES:(k + 1) * LANES]
        for k in range(nhalf):
            for sgi in range(nseg):
                o_ref[0, t0 + sgi * nstep:t0 + (sgi + 1) * nstep, k * LANES:(k + 1) * LANES] = (
                    stage_ref[k, sgi * pitch:sgi * pitch + nstep, :].astype(BF16))


def _lru(pb, cw_all, vec_all, wa_all, wx_all, layer):
    bsz, seq, _ = pb.shape
    return pl.pallas_call(
        _lru_kernel,
        grid=(bsz,),
        in_specs=[pl.BlockSpec((1, seq, GB_WIDTH), lambda b: (b, 0, 0)),
                  _layer_spec((CONV_WIDTH, B_WIDTH), layer), _layer_spec((4, B_WIDTH), layer),
                  _layer_spec((B_WIDTH, B_WIDTH), layer), _layer_spec((B_WIDTH, B_WIDTH), layer)],
        out_specs=pl.BlockSpec((1, seq, B_WIDTH), lambda b: (b, 0, 0)),
        out_shape=jax.ShapeDtypeStruct((bsz, seq, B_WIDTH), BF16),
        scratch_shapes=[pltpu.VMEM((GB_WIDTH // LANES, LRU_STAGE_ROWS, LANES), F32),
                        pltpu.VMEM((B_WIDTH // LANES, LRU_STAGE_ROWS, LANES), F32)],
        compiler_params=pltpu.CompilerParams(dimension_semantics=("arbitrary",),
                                             vmem_limit_bytes=VMEM_LIMIT),
        name="lru",
    )(pb, cw_all, vec_all, wa_all, wx_all)


def _stack_heads(x):
    x16 = x.astype(BF16)
    head = lax.broadcasted_iota(jnp.int32, x16.shape, 1) // HEAD_DIM
    return jnp.concatenate([jnp.where(head == h, x16, jnp.zeros_like(x16)) for h in range(C_HEADS)],
                           axis=0)


def _unstack_heads(x):
    head = lax.broadcasted_iota(jnp.int32, (CHUNK, STACK), 1) // HEAD_DIM
    out = x[(C_HEADS - 1) * CHUNK:C_HEADS * CHUNK]
    for h in range(C_HEADS - 2, -1, -1):
        out = jnp.where(head == h, x[h * CHUNK:(h + 1) * CHUNK], out)
    return out


def _rwkv_pre(slab, vfirst, vec, v1p, v2p, tril_l, sl, le, eye_w):
    r, k2, v, kap, b, lw = (slab[:, i * C_WIDTH:(i + 1) * C_WIDTH]
                            for i in (PC_R, PC_K, PC_V, PC_KAP, PC_B, PC_LW))
    r_k = vec[4:5, :]
    v0 = vec[7:8, :]
    if vfirst is not None:
        v = v + (vfirst - v) * _sigmoid(v0 + _mm(_mm(v, v1p), v2p))
    yield
    cum = _mm_exact_lhs(tril_l, lw, 2)
    cum_end = cum[CHUNK - 1:CHUNK, :]
    e_in = jnp.exp2(cum)
    e_ex = jnp.exp2(cum - lw)
    e_neg = jnp.exp2(-cum)
    e_end = jnp.exp2(cum_end - cum)
    p_end = jnp.exp2(cum_end)

    yield
    kt = kap * e_ex
    rt = r * e_in
    kt_st = _stack_heads(kt)
    vs_st = _stack_heads(v)
    gram = _mm_nt(jnp.concatenate([kt, rt], axis=0),
                  jnp.concatenate([_stack_heads(b * e_neg), _stack_heads(k2 * e_neg)], axis=0))
    a_ab = jnp.where(sl, gram[0:CHUNK, 0:STACK], 0.0)
    a_ak = jnp.where(sl, gram[0:CHUNK, STACK:2 * STACK], 0.0)
    a_rb = jnp.where(le, gram[CHUNK:2 * CHUNK, 0:STACK], 0.0)
    a_rk = jnp.where(le, gram[CHUNK:2 * CHUNK, STACK:2 * STACK], 0.0)

    yield
    pw = -a_ab
    t_w = eye_w + pw
    av = _mm(jnp.concatenate([a_ak, a_rk], axis=0), vs_st)
    pw = _mm(pw, _stack_heads(pw))
    for _ in range(1, int(math.log2(CHUNK)) - 1):
        yield
        both = _mm(jnp.concatenate([t_w, pw], axis=0), _stack_heads(pw))
        t_w = t_w + both[0:CHUNK]
        pw = both[CHUNK:2 * CHUNK]
    yield
    t_w = t_w + _mm(t_w, _stack_heads(pw))
    yield
    tx = _mm(t_w, jnp.concatenate([kt_st, _stack_heads(av[0:CHUNK])], axis=1))
    yield
    m1 = tx[:, 0:STACK]
    c1 = tx[:, STACK:2 * STACK]
    ab = _mm(a_rb, jnp.concatenate([_stack_heads(m1), _stack_heads(c1)], axis=1))
    q_n = rt - ab[:, 0:STACK]
    d_n = av[CHUNK:2 * CHUNK] - ab[:, STACK:2 * STACK]
    bh = b * e_end
    g_w = eye_w * p_end - _unstack_heads(_mm_tn(bh, m1))
    c_w = _unstack_heads(_mm_tn(jnp.concatenate([k2 * e_end, bh], axis=0),
                                jnp.concatenate([v, -c1], axis=0)))
    bonus_arg = r * k2 * r_k
    return q_n, d_n, g_w, c_w, bonus_arg, v, slab[:, PC_G * C_WIDTH:PC_WIDTH]


RWKV_SAVED = 7


def _rwkv_kernel(use_vres, *refs):
    if use_vres:
        (pc_ref, vf_ref, vec_ref, v1_ref, v2_ref, tril_ref, bd_ref,
         sl_ref, le_ref, eye_ref, y_ref, h_ref, pre_ref, ystage_ref) = refs
    else:
        (pc_ref, vec_ref, v1_ref, v2_ref, tril_ref, bd_ref,
         sl_ref, le_ref, eye_ref, y_ref, h_ref, pre_ref, ystage_ref) = refs
        vf_ref = None
    seq = pc_ref.shape[1]
    ngroups = seq // (CHUNK * RWKV_GROUP)
    step_id = pl.program_id(0)
    last_step = pl.num_programs(0) - 1

    @pl.when(step_id == 0)
    def _():
        h_ref[...] = jnp.zeros_like(h_ref)
        pre_ref[...] = jnp.zeros_like(pre_ref)
        ystage_ref[...] = jnp.zeros_like(ystage_ref)

    def chunk_start(gi, j):
        return pl.multiple_of((gi * RWKV_GROUP + j) * CHUNK, CHUNK)

    def pre_generators(gi):
        sl = sl_ref[...] > 0.0
        le = le_ref[...] > 0.0
        gens = []
        for j in range(RWKV_GROUP):
            t0 = chunk_start(gi, j)
            slab = pc_ref[0, pl.ds(t0, CHUNK), :]
            vfirst = vf_ref[0, pl.ds(t0, CHUNK), :] if use_vres else None
            gens.append(_rwkv_pre(slab, vfirst, vec_ref[...], v1_ref[...], v2_ref[...], tril_ref[...],
                                  sl, le, eye_ref[...]))
        return gens

    def chain_tasks(gi):
        ys = [None] * RWKV_GROUP
        stats = [None] * RWKV_GROUP
        state = {}

        def step(j):
            def run():
                h_w = h_ref[...] if j == 0 else state["h"]
                q_n, d_n, g_w, c_w = (pre_ref[j * RWKV_SAVED + k] for k in range(4))
                res = _mm(jnp.concatenate([q_n, g_w], axis=0), _stack_heads(h_w))
                ys[j] = res[0:CHUNK] + d_n
                state["h"] = res[CHUNK:2 * CHUNK] + c_w
                if j == RWKV_GROUP - 1:
                    h_ref[...] = state["h"]
            return run

        def sums():
            for j in range(RWKV_GROUP):
                parts = _split_bf16(ys[j], 2) + [pre_ref[j * RWKV_SAVED + 4].astype(BF16)]
                s = jnp.dot(jnp.concatenate(parts, axis=0), bd_ref[...], preferred_element_type=F32)
                yc = ys[j] - (s[0:CHUNK] + s[CHUNK:2 * CHUNK]) * (1.0 / HEAD_DIM)
                stats[j] = (yc, s[2 * CHUNK:3 * CHUNK])

        def variance():
            for j in range(RWKV_GROUP):
                yc, bonus = stats[j]
                yv = _mm(yc * yc, bd_ref[...]) * (1.0 / HEAD_DIM)
                stats[j] = (yc, bonus, yv)

        def finish():
            gn_w = vec_ref[5:6, :]
            gn_b = vec_ref[6:7, :]
            for j in range(RWKV_GROUP):
                t0 = chunk_start(gi, j)
                yc, bonus, yv = stats[j]
                yn = yc * lax.rsqrt(yv + GN_EPS) * gn_w + gn_b
                out = (yn + bonus * pre_ref[j * RWKV_SAVED + 5]) * _silu(pre_ref[j * RWKV_SAVED + 6])
                ystage_ref[pl.ds(t0, CHUNK), :] = out.astype(BF16)

        return [step(j) for j in range(RWKV_GROUP)] + [sums, variance, finish]

    def run_group(gi, tasks):
        gens = pre_generators(gi)
        tasks = list(tasks)
        pre = [None] * RWKV_GROUP
        sweep = 0
        while any(p is None for p in pre):
            for j, gen in enumerate(gens):
                if pre[j] is None:
                    try:
                        next(gen)
                    except StopIteration as stop:
                        pre[j] = stop.value
            sweep += 1
            if tasks and sweep >= 2:
                tasks.pop(0)()
        for t in tasks:
            t()
        for j in range(RWKV_GROUP):
            for k in range(RWKV_SAVED):
                pre_ref[j * RWKV_SAVED + k] = pre[j][k]

    def flush():
        y_ref[0] = ystage_ref[...]
        h_ref[...] = jnp.zeros_like(h_ref)

    @pl.when(step_id < last_step)
    def _():
        run_group(0, chain_tasks(ngroups - 1) + [flush])

        def body(i, carry):
            run_group(i, chain_tasks(i - 1))
            return carry

        lax.fori_loop(1, ngroups, body, 0)

    @pl.when(step_id == last_step)
    def _():
        for t in chain_tasks(ngroups - 1) + [flush]:
            t()


def _rwkv_consts():
    idx = jnp.arange(STACK)
    same = (idx[:, None] // CHUNK) == (idx[None, :] // CHUNK)
    ci = jnp.arange(CHUNK)
    tril_l = (ci[None, :] <= ci[:, None]).astype(BF16)
    s_loc = idx[None, :] % CHUNK
    sl = (s_loc < ci[:, None]).astype(F32)
    le = (s_loc <= ci[:, None]).astype(F32)
    eye_w = (s_loc == ci[:, None]).astype(F32)
    return tril_l, same.astype(BF16), sl, le, eye_w


def _rwkv(pc, pc_first, vec_all, v1_all, v2_all, consts, layer):
    bsz, seq, _ = pc.shape
    use_vres = pc_first is not None
    sq = (STACK, STACK)
    wide = (CHUNK, STACK)
    cur = lambda b: jnp.minimum(b, bsz - 1)
    in_specs = [pl.BlockSpec((1, seq, PC_WIDTH), lambda b: (cur(b), 0, 0))]
    if use_vres:
        in_specs.append(pl.BlockSpec((1, seq, C_WIDTH), lambda b: (cur(b), 0, PC_V)))
    in_specs += [_layer_spec((8, C_WIDTH), layer),
                 _layer_spec((C_WIDTH, LANES), layer), _layer_spec((LANES, C_WIDTH), layer),
                 _const_spec((CHUNK, CHUNK)), _const_spec(sq),
                 _const_spec(wide), _const_spec(wide), _const_spec(wide)]
    args = (pc,) + ((pc_first,) if use_vres else ()) + (vec_all, v1_all, v2_all)
    return pl.pallas_call(
        functools.partial(_rwkv_kernel, use_vres),
        grid=(bsz + 1,),
        in_specs=in_specs,
        out_specs=pl.BlockSpec((1, seq, C_WIDTH), lambda b: (jnp.maximum(b - 1, 0), 0, 0)),
        out_shape=jax.ShapeDtypeStruct((bsz, seq, C_WIDTH), BF16),
        scratch_shapes=[pltpu.VMEM(wide, F32), pltpu.VMEM((RWKV_GROUP * RWKV_SAVED,) + wide, F32),
                        pltpu.VMEM((seq, C_WIDTH), BF16)],
        compiler_params=pltpu.CompilerParams(dimension_semantics=("arbitrary",),
                                             vmem_limit_bytes=VMEM_LIMIT),
        name="rwkv",
    )(*args, *consts)


def _out_kernel(alpha, tm, seq, fused, ya_ref, yb_ref, yc_ref, x_ref, w_ref, ln_ref, *rest):
    if fused:
        win_ref, *tok_refs = rest[0:6]
        o_ref, oa_ref, ob_ref, oc_ref, last_ref = rest[6:]
        _init_last(last_ref)
    else:
        (o_ref,) = rest

    def project(rows):
        acc = jnp.dot(ya_ref[rows, :], w_ref[0:A_WIDTH, :], preferred_element_type=F32)
        acc = acc + jnp.dot(yb_ref[rows, :], w_ref[A_WIDTH:A_WIDTH + B_WIDTH, :],
                            preferred_element_type=F32)
        return acc + jnp.dot(yc_ref[rows, :], w_ref[A_WIDTH + B_WIDTH:D_MIX, :],
                             preferred_element_type=F32)

    def norm(rows, acc):
        z = alpha * x_ref[rows, :] + acc
        mu = jnp.mean(z, axis=-1, keepdims=True)
        zc = z - mu
        var = jnp.mean(zc * zc, axis=-1, keepdims=True)
        xn = zc * lax.rsqrt(var + LN_EPS) * ln_ref[0:1, :] + ln_ref[1:2, :]
        o_ref[rows, :] = xn
        if not fused:
            return None
        return _proj_rows(xn.astype(BF16), win_ref, rows.start, OUT_SUB, seq, tok_refs, last_ref,
                          oa_ref, ob_ref, oc_ref)

    subs = [slice(s * OUT_SUB, (s + 1) * OUT_SUB) for s in range(tm // OUT_SUB)]
    acc_prev = project(subs[0])
    pending = None
    for s in range(len(subs)):
        acc_next = project(subs[s + 1]) if s + 1 < len(subs) else None
        finish = norm(subs[s], acc_prev)
        if pending is not None:
            pending()
        pending = finish
        acc_prev = acc_next
    if pending is not None:
        pending()


def _out(ya, yb, yc, x2, w_all, ln_all, alpha, layer, seq, w_in_all=None, tok_args=()):
    n = x2.shape[0]
    fused = w_in_all is not None
    tm = FUSED_TM if fused else OUT_TM
    row = lambda w: pl.BlockSpec((tm, w), lambda i: (i, 0))
    in_specs = [row(A_WIDTH), row(B_WIDTH), row(C_WIDTH), row(D_MODEL),
                _layer_spec((D_MIX, D_MODEL), layer, single_buffer=True),
                _layer_spec((2, D_MODEL), layer)]
    out_specs = [row(D_MODEL)]
    out_shape = [jax.ShapeDtypeStruct((n, D_MODEL), F32)]
    args = [ya, yb, yc, x2, w_all, ln_all]
    if fused:
        in_specs.append(_layer_spec((D_IN_PAD, D_MODEL), layer + 1, single_buffer=True))
        in_specs += _tok_specs(layer + 1)
        out_specs += [row(GA_WIDTH), row(GB_WIDTH), row(PC_WIDTH)]
        out_shape += [jax.ShapeDtypeStruct((n, GA_WIDTH), BF16),
                      jax.ShapeDtypeStruct((n, GB_WIDTH), F32),
                      jax.ShapeDtypeStruct((n, PC_WIDTH), F32)]
        args += [w_in_all, *tok_args]
    return pl.pallas_call(
        functools.partial(_out_kernel, alpha, tm, seq, fused),
        grid=(n // tm,),
        in_specs=in_specs,
        out_specs=out_specs,
        out_shape=out_shape,
        scratch_shapes=[pltpu.VMEM((SUBLANES, C_SHIFT_PAD), F32)] if fused else [],
        compiler_params=pltpu.CompilerParams(dimension_semantics=("arbitrary",),
                                             vmem_limit_bytes=VMEM_LIMIT),
        name="outproj_proj" if fused else "outproj",
    )(*args)


def _block_diag_all(w):
    depth, nblk, d, _ = w.shape
    eye = jnp.eye(nblk, dtype=w.dtype)
    return (w[:, :, :, None, :] * eye[None, :, None, :, None]).reshape(depth, nblk * d, nblk * d)


def kernel(x, w_in, w_out, ln_g, ln_b, attn_sinks, conv_w, conv_b, lru_wa, lru_ba, lru_wx, lru_bx,
           lru_lambda, rwkv_mu, rwkv_w0, rwkv_w2, rwkv_a0, rwkv_a2, rwkv_kk, rwkv_ka, rwkv_rk,
           rwkv_gn_w, rwkv_gn_b, rwkv_v0, rwkv_v1, rwkv_v2):
    bsz, seq, dm = x.shape
    depth = w_in.shape[0]
    alpha = (2 * depth) ** 0.25
    n = bsz * seq
    pad_w = C_SHIFT_PAD - C_SHIFT_WIDTH

    q_scale = jnp.concatenate([jnp.full((A_WIDTH,), HEAD_DIM ** -0.5 * LOG2E, F32),
                               jnp.ones((w_in.shape[2] - A_WIDTH,), F32)])
    w_in_p = (jnp.swapaxes(w_in, 1, 2) * q_scale[None, :, None]).astype(BF16)
    w_out_b = w_out.astype(BF16)
    ln_all = jnp.stack([ln_g, ln_b], axis=1)
    lru_vec = jnp.stack([conv_b, lru_ba, lru_bx, lru_lambda], axis=1)
    wa_bd = _block_diag_all(lru_wa).astype(BF16)
    wx_bd = _block_diag_all(lru_wx).astype(BF16)
    mu_all = jnp.pad(rwkv_mu, ((0, 0), (0, pad_w)))[:, None, :]
    v0_all = jnp.pad(rwkv_v0, ((1, 0), (0, 0)))
    vec_all = jnp.stack([rwkv_w0, rwkv_a0, rwkv_kk, rwkv_ka, rwkv_rk.reshape(depth, C_WIDTH),
                         rwkv_gn_w, rwkv_gn_b, v0_all], axis=1)
    w2_all = jnp.pad(rwkv_w2, ((0, 0), (0, LANES - DECAY_RANK), (0, 0))).astype(BF16)
    a2_all = jnp.pad(rwkv_a2, ((0, 0), (DECAY_RANK, LANES - DECAY_RANK - AICL_RANK), (0, 0))).astype(BF16)
    v1_all = jnp.pad(rwkv_v1, ((1, 0), (0, 0), (0, LANES - VRES_RANK))).astype(BF16)
    v2_all = jnp.pad(rwkv_v2, ((1, 0), (0, LANES - VRES_RANK), (0, 0))).astype(BF16)
    consts = _rwkv_consts()

    tok_args = (mu_all, vec_all, w2_all, a2_all, consts[1])
    x2 = x.reshape(n, dm)
    pc_first = None
    pa, pb, pc = _proj(x2, w_in_p, tok_args, 0, seq)
    for l in range(depth):
        pc3 = pc.reshape(bsz, seq, PC_WIDTH)
        ya = _attn(pa.reshape(bsz, seq, GA_WIDTH), attn_sinks, l)
        yb = _lru(pb.reshape(bsz, seq, GB_WIDTH), conv_w, lru_vec, wa_bd, wx_bd, l)
        yc = _rwkv(pc3, pc_first, vec_all, v1_all, v2_all, consts, l)
        if l == 0:
            pc_first = pc3
        outs = _out(ya.reshape(n, A_WIDTH), yb.reshape(n, B_WIDTH), yc.reshape(n, C_WIDTH), x2,
                    w_out_b, ln_all, alpha, l, seq, w_in_p if l + 1 < depth else None, tok_args)
        x2 = outs[0]
        if l + 1 < depth:
            pa, pb, pc = outs[1:]
    return x2.reshape(bsz, seq, dm)
```

```python
import functools
import math

import jax
import jax.numpy as jnp
from jax import lax
from jax.experimental import pallas as pl
from jax.experimental.pallas import tpu as pltpu

F32 = jnp.float32
BF16 = jnp.bfloat16

D_MODEL = 1024
HEAD_DIM = 64
A_Q_HEADS = 8
A_KV_HEADS = 2
A_WIDTH = A_Q_HEADS * HEAD_DIM
A_KV_WIDTH = A_KV_HEADS * HEAD_DIM
ATT_BLOCK = 128
B_WIDTH = 256
B_BLOCKS = 4
CONV_WIDTH = 4
LRU_C = 8.0
C_HEADS = 4
C_WIDTH = C_HEADS * HEAD_DIM
DECAY_RANK = 32
AICL_RANK = 32
VRES_RANK = 16
C_SHIFT_WIDTH = 3 * C_WIDTH + DECAY_RANK + AICL_RANK
GN_EPS = 64e-5
LN_EPS = 1e-5
LOG2E = math.log2(math.e)

LANES = 128
SUBLANES = 8
C_SHIFT_PAD = ((C_SHIFT_WIDTH + LANES - 1) // LANES) * LANES
GA_WIDTH = A_WIDTH + 2 * A_KV_WIDTH + A_WIDTH
GB_WIDTH = 2 * B_WIDTH
GC_WIDTH = C_SHIFT_WIDTH + C_WIDTH
D_IN_PAD = GA_WIDTH + GB_WIDTH + GC_WIDTH
D_MIX = A_WIDTH + B_WIDTH + C_WIDTH
PC_R, PC_K, PC_V, PC_KAP, PC_B, PC_LW, PC_G = range(7)
PC_WIDTH = 7 * C_WIDTH

CHUNK = 64
STACK = C_HEADS * CHUNK
RWKV_GROUP = 8
ATT_TILE = 1024
LRU_TILE = 256
LRU_STAGE_ROWS = SUBLANES * (LRU_TILE // SUBLANES + 4)
PROJ_TM = 512
OUT_TM = 1024
FUSED_TM = 512
OUT_SUB = 256
VMEM_LIMIT = 56 * 1024 * 1024


def _mm(a, b):
    return jnp.dot(a.astype(BF16), b.astype(BF16), preferred_element_type=F32)


def _mm_nt(a, b):
    return lax.dot_general(a.astype(BF16), b.astype(BF16), (((1,), (1,)), ((), ())),
                           preferred_element_type=F32)


def _mm_tn(a, b):
    return lax.dot_general(a.astype(BF16), b.astype(BF16), (((0,), (0,)), ((), ())),
                           preferred_element_type=F32)


def _split_bf16(x, parts):
    out = []
    rem = x
    for _ in range(parts):
        hi = rem.astype(BF16)
        out.append(hi)
        rem = rem - hi.astype(F32)
    return out


def _mm_exact_lhs(m_bf16, x, parts):
    n = x.shape[1]
    t = jnp.dot(m_bf16, jnp.concatenate(_split_bf16(x, parts), axis=1), preferred_element_type=F32)
    acc = t[:, 0:n]
    for p in range(1, parts):
        acc = acc + t[:, p * n:(p + 1) * n]
    return acc


def _sigmoid(x):
    return 1.0 / (1.0 + jnp.exp2(x * (-LOG2E)))


def _silu(x):
    return x * _sigmoid(x)


def _softplus(x):
    return jnp.maximum(x, 0.0) + jnp.log(1.0 + jnp.exp(-jnp.abs(x)))


def _layer_spec(shape, layer, single_buffer=False):
    index_map = lambda *_: (layer,) + (0,) * len(shape)
    if single_buffer:
        return pl.BlockSpec((None,) + shape, index_map, pipeline_mode=pl.Buffered(1))
    return pl.BlockSpec((None,) + shape, index_map)


def _const_spec(shape):
    return pl.BlockSpec(shape, lambda *_: (0,) * len(shape))


def _rwkv_tokenwise(cs, g, prev_row, mu, vec, w2p, a2p, ones_bd):
    w0, a0, k_k, k_a = (vec[i:i + 1, :] for i in range(4))
    row = lax.broadcasted_iota(jnp.int32, cs.shape, 0)
    prev = jnp.where(row == 0, prev_row, pltpu.roll(cs, 1, axis=0))
    xs = cs + (prev - cs) * mu
    r = xs[:, 0:C_WIDTH]
    k = xs[:, C_WIDTH:2 * C_WIDTH]
    v = xs[:, 2 * C_WIDTH:3 * C_WIDTH]
    la = xs[:, 3 * C_WIDTH:C_SHIFT_PAD]
    kk = k * k_k
    yield
    z_w = _mm(jnp.tanh(la), w2p)
    z_a = _mm(la, a2p)
    norm2 = _mm(kk * kk, ones_bd)
    yield
    lw = (-LOG2E * math.exp(-0.5)) * _sigmoid(w0 + z_w)
    a = _sigmoid(a0 + z_a)
    kap = kk * lax.rsqrt(norm2 + 1e-12)
    k2 = k * (1.0 + (a - 1.0) * k_a)
    return jnp.concatenate([r, k2, v, kap, kap * a, lw, g], axis=1), cs[cs.shape[0] - 1:, :]


def _drain(stage):
    if stage is not None:
        for _ in stage:
            pass


def _proj_rows(x16, w_ref, row0, nrows, seq, tok_refs, last_ref, oa_ref, ob_ref, oc_ref, pending):
    mu_ref, vec_ref, w2_ref, a2_ref, ones_ref = tok_refs
    nt = (((1,), (1,)), ((), ()))
    rows = pl.ds(row0, nrows)

    def advance():
        if pending is not None:
            next(pending, None)

    oa_ref[rows, :] = lax.dot_general(x16, w_ref[0:GA_WIDTH, :], nt,
                                      preferred_element_type=F32).astype(BF16)
    advance()
    ob_ref[rows, :] = lax.dot_general(x16, w_ref[GA_WIDTH:GA_WIDTH + GB_WIDTH, :], nt,
                                      preferred_element_type=F32)
    advance()
    c0 = GA_WIDTH + GB_WIDTH
    cs = lax.dot_general(x16, w_ref[c0:c0 + C_SHIFT_PAD, :], nt, preferred_element_type=F32)
    advance()
    g = lax.dot_general(x16, w_ref[c0 + C_SHIFT_WIDTH:D_IN_PAD, :], nt, preferred_element_type=F32)
    _drain(pending)
    first = pl.program_id(0) * oa_ref.shape[0] + row0

    def stage():
        prev_row = jnp.where(first % seq == 0, 0.0, last_ref[0:1, :])
        slab, last = yield from _rwkv_tokenwise(cs, g, prev_row, mu_ref[...], vec_ref[...],
                                                w2_ref[...], a2_ref[...], ones_ref[...])
        oc_ref[rows, :] = slab
        last_ref[0:1, :] = last

    return stage()


def _init_last(last_ref):
    @pl.when(pl.program_id(0) == 0)
    def _():
        last_ref[...] = jnp.zeros_like(last_ref)


def _proj_kernel(seq, x_ref, w_ref, mu_ref, vec_ref, w2_ref, a2_ref, ones_ref, oa_ref, ob_ref, oc_ref,
                 last_ref):
    _init_last(last_ref)
    pending = None
    for row0 in range(0, PROJ_TM, OUT_SUB):
        pending = _proj_rows(x_ref[pl.ds(row0, OUT_SUB), :].astype(BF16), w_ref, row0, OUT_SUB, seq,
                             (mu_ref, vec_ref, w2_ref, a2_ref, ones_ref), last_ref, oa_ref, ob_ref,
                             oc_ref, pending)
    _drain(pending)


def _tok_specs(layer):
    return [_layer_spec((1, C_SHIFT_PAD), layer), _layer_spec((8, C_WIDTH), layer),
            _layer_spec((LANES, C_WIDTH), layer), _layer_spec((LANES, C_WIDTH), layer),
            _const_spec((STACK, STACK))]


def _proj(x2, w_all, tok_args, layer, seq):
    n = x2.shape[0]
    return pl.pallas_call(
        functools.partial(_proj_kernel, seq),
        grid=(n // PROJ_TM,),
        in_specs=[pl.BlockSpec((PROJ_TM, D_MODEL), lambda i: (i, 0)),
                  _layer_spec((D_IN_PAD, D_MODEL), layer)] + _tok_specs(layer),
        out_specs=[pl.BlockSpec((PROJ_TM, GA_WIDTH), lambda i: (i, 0)),
                   pl.BlockSpec((PROJ_TM, GB_WIDTH), lambda i: (i, 0)),
                   pl.BlockSpec((PROJ_TM, PC_WIDTH), lambda i: (i, 0))],
        out_shape=[jax.ShapeDtypeStruct((n, GA_WIDTH), BF16),
                   jax.ShapeDtypeStruct((n, GB_WIDTH), F32),
                   jax.ShapeDtypeStruct((n, PC_WIDTH), F32)],
        scratch_shapes=[pltpu.VMEM((SUBLANES, C_SHIFT_PAD), F32)],
        compiler_params=pltpu.CompilerParams(dimension_semantics=("arbitrary",),
                                             vmem_limit_bytes=VMEM_LIMIT),
        name="proj",
    )(x2, w_all, *tok_args)


def _attn_kernel(layer, sink_ref, cur_ref, prev_ref, o_ref):
    n = pl.program_id(1)
    blk = ATT_BLOCK
    nsub = ATT_TILE // blk
    grp = A_Q_HEADS // A_KV_HEADS
    lo = lax.broadcasted_iota(jnp.int32, (blk, A_KV_WIDTH), 1) < HEAD_DIM

    def variants(x16):
        x = x16.astype(F32)
        xs = pltpu.roll(x, HEAD_DIM, axis=1)
        return ((jnp.where(lo, x, 0.0).astype(BF16), jnp.where(lo, 0.0, xs).astype(BF16)),
                (jnp.where(lo, xs, 0.0).astype(BF16), jnp.where(lo, 0.0, x).astype(BF16)))

    k0, v0 = A_WIDTH, A_WIDTH + A_KV_WIDTH
    g0 = A_WIDTH + 2 * A_KV_WIDTH
    kblk = [variants(prev_ref[0, :, 0:A_KV_WIDTH])]
    vblk = [variants(prev_ref[0, :, A_KV_WIDTH:2 * A_KV_WIDTH])]
    for j in range(nsub):
        kblk.append(variants(cur_ref[0, j * blk:(j + 1) * blk, k0:k0 + A_KV_WIDTH]))
        vblk.append(variants(cur_ref[0, j * blk:(j + 1) * blk, v0:v0 + A_KV_WIDTH]))

    qi = lax.broadcasted_iota(jnp.int32, (blk, 2 * blk), 0)
    kj = lax.broadcasted_iota(jnp.int32, (blk, 2 * blk), 1)
    diff = qi + blk - kj
    band = (diff >= 0) & (diff < blk)
    first = band & (kj + (n * nsub - 1) * blk >= 0)

    for j in range(nsub):
        mask = first if j == 0 else band
        rows = slice(j * blk, (j + 1) * blk)
        scores = []
        for p in range(A_Q_HEADS // 2):
            h = (2 * p) // grp
            qp = cur_ref[0, rows, p * LANES:(p + 1) * LANES]
            for half in range(2):
                kb = jnp.concatenate([kblk[j][h][half], kblk[j + 1][h][half]], axis=0)
                scores.append(lax.dot_general(qp, kb, (((1,), (1,)), ((), ())),
                                              preferred_element_type=F32))
        probs, inv = [], []
        for idx, s in enumerate(scores):
            sink2 = sink_ref[layer, idx] * LOG2E
            s = jnp.where(mask, s, -jnp.inf)
            m = jnp.maximum(jnp.max(s, axis=-1, keepdims=True), sink2)
            e = jnp.exp2(s - m)
            den = jnp.sum(e, axis=-1, keepdims=True) + jnp.exp2(sink2 - m)
            probs.append(e.astype(BF16))
            inv.append(1.0 / den)
        for p in range(A_Q_HEADS // 2):
            h = (2 * p) // grp
            acc = None
            for half in range(2):
                vb = jnp.concatenate([vblk[j][h][half], vblk[j + 1][h][half]], axis=0)
                o = jnp.dot(probs[2 * p + half], vb, preferred_element_type=F32) * inv[2 * p + half]
                acc = o if acc is None else acc + o
            g = cur_ref[0, rows, g0 + p * LANES:g0 + (p + 1) * LANES].astype(F32)
            o_ref[0, rows, p * LANES:(p + 1) * LANES] = (acc * _silu(g)).astype(BF16)


def _attn(pa, sinks, layer):
    bsz, seq, _ = pa.shape
    nsub = ATT_TILE // ATT_BLOCK
    kv_blk = A_WIDTH // (2 * A_KV_WIDTH)
    return pl.pallas_call(
        functools.partial(_attn_kernel, layer),
        grid=(bsz, seq // ATT_TILE),
        in_specs=[pl.BlockSpec(memory_space=pltpu.SMEM),
                  pl.BlockSpec((1, ATT_TILE, GA_WIDTH), lambda b, n: (b, n, 0)),
                  pl.BlockSpec((1, ATT_BLOCK, 2 * A_KV_WIDTH),
                               lambda b, n: (b, jnp.maximum(n * nsub - 1, 0), kv_blk))],
        out_specs=pl.BlockSpec((1, ATT_TILE, A_WIDTH), lambda b, n: (b, n, 0)),
        out_shape=jax.ShapeDtypeStruct((bsz, seq, A_WIDTH), BF16),
        compiler_params=pltpu.CompilerParams(dimension_semantics=("arbitrary", "arbitrary"),
                                             vmem_limit_bytes=VMEM_LIMIT),
        name="attn",
    )(sinks, pa, pa)


def _lru_kernel(pb_ref, cw_ref, vec_ref, wa_ref, wx_ref, o_ref, in_ref, stage_ref):
    seq = pb_ref.shape[1]
    tile = LRU_TILE
    nseg = SUBLANES
    nstep = tile // nseg
    conv_b = vec_ref[0:1, :]
    ba = vec_ref[1:2, :]
    bx = vec_ref[2:3, :]
    lam = vec_ref[3:4, :]
    neg_c_sp = -LRU_C * _softplus(-lam)
    cw = [cw_ref[j:j + 1, :] for j in range(CONV_WIDTH)]
    sub0 = lax.broadcasted_iota(jnp.int32, (nseg, B_WIDTH), 0) == 0

    nhalf = B_WIDTH // LANES

    pitch = nstep + 4
    assert pitch % SUBLANES == 4

    def rows_of(step):
        return pl.ds(step, nseg, stride=pitch)

    def permuted(k0, step):
        return jnp.concatenate([in_ref[k0 + k, rows_of(step), :] for k in range(nhalf)], axis=1)

    hc = jnp.zeros((1, B_WIDTH), F32)
    tail = [jnp.zeros((1, B_WIDTH), F32)] * (CONV_WIDTH - 1)
    for ti in range(seq // tile):
        t0 = ti * tile
        for k in range(2 * nhalf):
            for sgi in range(nseg):
                in_ref[k, sgi * pitch:sgi * pitch + nstep, :] = pb_ref[
                    0, t0 + sgi * nstep:t0 + (sgi + 1) * nstep, k * LANES:(k + 1) * LANES]
        xs = [permuted(0, v) for v in range(nstep)]
        wrapped = []
        for i in range(CONV_WIDTH - 1):
            src_v = xs[nstep - (CONV_WIDTH - 1) + i]
            wrapped.append(jnp.where(sub0, tail[i], pltpu.roll(src_v, 1, axis=0)))
        x_all = jnp.concatenate(xs, axis=0)
        xc = cw[CONV_WIDTH - 1] * x_all + conv_b
        for j in range(1, CONV_WIDTH):
            shifted = jnp.concatenate(wrapped[CONV_WIDTH - 1 - j:] + xs[:nstep - j], axis=0)
            xc = xc + cw[CONV_WIDTH - 1 - j] * shifted
        tail = [xs[nstep - (CONV_WIDTH - 1) + i][nseg - 1:nseg, :] for i in range(CONV_WIDTH - 1)]

        r = _sigmoid(_mm(xc, wa_ref[...]) + ba)
        ig = _sigmoid(_mm(xc, wx_ref[...]) + bx)
        a = jnp.exp(neg_c_sp * r)
        u = jnp.sqrt(1.0 - a * a) * (ig * xc)
        hl = [u[0:nseg]]
        al = [a[0:nseg]]
        for v in range(1, nstep):
            av = a[v * nseg:(v + 1) * nseg]
            hl.append(av * hl[-1] + u[v * nseg:(v + 1) * nseg])
            al.append(av * al[-1])
        carry = hc
        enter = []
        for sgi in range(nseg):
            enter.append(carry)
            carry = hl[-1][sgi:sgi + 1, :] + al[-1][sgi:sgi + 1, :] * carry
        hc = carry
        enter = jnp.concatenate(enter, axis=0)
        for v in range(nstep):
            out = (hl[v] + al[v] * enter) * _silu(permuted(nhalf, v))
            for k in range(nhalf):
                stage_ref[k, rows_of(v), :] = out[:, k * LANES:(k + 1) * LANES]
        for k in range(nhalf):
            for sgi in range(nseg):
                o_ref[0, t0 + sgi * nstep:t0 + (sgi + 1) * nstep, k * LANES:(k + 1) * LANES] = (
                    stage_ref[k, sgi * pitch:sgi * pitch + nstep, :].astype(BF16))


def _lru(pb, cw_all, vec_all, wa_all, wx_all, layer):
    bsz, seq, _ = pb.shape
    return pl.pallas_call(
        _lru_kernel,
        grid=(bsz,),
        in_specs=[pl.BlockSpec((1, seq, GB_WIDTH), lambda b: (b, 0, 0)),
                  _layer_spec((CONV_WIDTH, B_WIDTH), layer), _layer_spec((4, B_WIDTH), layer),
                  _layer_spec((B_WIDTH, B_WIDTH), layer), _layer_spec((B_WIDTH, B_WIDTH), layer)],
        out_specs=pl.BlockSpec((1, seq, B_WIDTH), lambda b: (b, 0, 0)),
        out_shape=jax.ShapeDtypeStruct((bsz, seq, B_WIDTH), BF16),
        scratch_shapes=[pltpu.VMEM((GB_WIDTH // LANES, LRU_STAGE_ROWS, LANES), F32),
                        pltpu.VMEM((B_WIDTH // LANES, LRU_STAGE_ROWS, LANES), F32)],
        compiler_params=pltpu.CompilerParams(dimension_semantics=("arbitrary",),
                                             vmem_limit_bytes=VMEM_LIMIT),
        name="lru",
    )(pb, cw_all, vec_all, wa_all, wx_all)


def _stack_heads(x):
    x16 = x.astype(BF16)
    head = lax.broadcasted_iota(jnp.int32, x16.shape, 1) // HEAD_DIM
    return jnp.concatenate([jnp.where(head == h, x16, jnp.zeros_like(x16)) for h in range(C_HEADS)],
                           axis=0)


def _unstack_heads(x):
    head = lax.broadcasted_iota(jnp.int32, (CHUNK, STACK), 1) // HEAD_DIM
    out = x[(C_HEADS - 1) * CHUNK:C_HEADS * CHUNK]
    for h in range(C_HEADS - 2, -1, -1):
        out = jnp.where(head == h, x[h * CHUNK:(h + 1) * CHUNK], out)
    return out


def _rwkv_pre(slab, vfirst, vec, v1p, v2p, tril_l, sl, le, eye_w):
    r, k2, v, kap, b, lw = (slab[:, i * C_WIDTH:(i + 1) * C_WIDTH]
                            for i in (PC_R, PC_K, PC_V, PC_KAP, PC_B, PC_LW))
    r_k = vec[4:5, :]
    v0 = vec[7:8, :]
    if vfirst is not None:
        v = v + (vfirst - v) * _sigmoid(v0 + _mm(_mm(v, v1p), v2p))
    yield
    cum = _mm_exact_lhs(tril_l, lw, 2)
    cum_end = cum[CHUNK - 1:CHUNK, :]
    e_in = jnp.exp2(cum)
    e_ex = jnp.exp2(cum - lw)
    e_neg = jnp.exp2(-cum)
    e_end = jnp.exp2(cum_end - cum)
    p_end = jnp.exp2(cum_end)

    yield
    kt = kap * e_ex
    rt = r * e_in
    kt_st = _stack_heads(kt)
    vs_st = _stack_heads(v)
    gram = _mm_nt(jnp.concatenate([kt, rt], axis=0),
                  jnp.concatenate([_stack_heads(b * e_neg), _stack_heads(k2 * e_neg)], axis=0))
    a_ab = jnp.where(sl, gram[0:CHUNK, 0:STACK], 0.0)
    a_ak = jnp.where(sl, gram[0:CHUNK, STACK:2 * STACK], 0.0)
    a_rb = jnp.where(le, gram[CHUNK:2 * CHUNK, 0:STACK], 0.0)
    a_rk = jnp.where(le, gram[CHUNK:2 * CHUNK, STACK:2 * STACK], 0.0)

    yield
    pw = -a_ab
    t_w = eye_w + pw
    av = _mm(jnp.concatenate([a_ak, a_rk], axis=0), vs_st)
    pw = _mm(pw, _stack_heads(pw))
    for _ in range(1, int(math.log2(CHUNK)) - 1):
        yield
        both = _mm(jnp.concatenate([t_w, pw], axis=0), _stack_heads(pw))
        t_w = t_w + both[0:CHUNK]
        pw = both[CHUNK:2 * CHUNK]
    yield
    t_w = t_w + _mm(t_w, _stack_heads(pw))
    yield
    tx = _mm(t_w, jnp.concatenate([kt_st, _stack_heads(av[0:CHUNK])], axis=1))
    yield
    m1 = tx[:, 0:STACK]
    c1 = tx[:, STACK:2 * STACK]
    ab = _mm(a_rb, jnp.concatenate([_stack_heads(m1), _stack_heads(c1)], axis=1))
    q_n = rt - ab[:, 0:STACK]
    d_n = av[CHUNK:2 * CHUNK] - ab[:, STACK:2 * STACK]
    bh = b * e_end
    g_w = eye_w * p_end - _unstack_heads(_mm_tn(bh, m1))
    c_w = _unstack_heads(_mm_tn(jnp.concatenate([k2 * e_end, bh], axis=0),
                                jnp.concatenate([v, -c1], axis=0)))
    bonus_arg = r * k2 * r_k
    return q_n, d_n, g_w, c_w, bonus_arg, v, slab[:, PC_G * C_WIDTH:PC_WIDTH]


RWKV_SAVED = 7


def _rwkv_kernel(use_vres, *refs):
    if use_vres:
        (pc_ref, vf_ref, vec_ref, v1_ref, v2_ref, tril_ref, bd_ref,
         sl_ref, le_ref, eye_ref, y_ref, h_ref, pre_ref, ystage_ref) = refs
    else:
        (pc_ref, vec_ref, v1_ref, v2_ref, tril_ref, bd_ref,
         sl_ref, le_ref, eye_ref, y_ref, h_ref, pre_ref, ystage_ref) = refs
        vf_ref = None
    seq = pc_ref.shape[1]
    ngroups = seq // (CHUNK * RWKV_GROUP)
    step_id = pl.program_id(0)
    last_step = pl.num_programs(0) - 1

    @pl.when(step_id == 0)
    def _():
        h_ref[...] = jnp.zeros_like(h_ref)
        pre_ref[...] = jnp.zeros_like(pre_ref)
        ystage_ref[...] = jnp.zeros_like(ystage_ref)

    def chunk_start(gi, j):
        return pl.multiple_of((gi * RWKV_GROUP + j) * CHUNK, CHUNK)

    def pre_generators(gi):
        sl = sl_ref[...] > 0.0
        le = le_ref[...] > 0.0
        gens = []
        for j in range(RWKV_GROUP):
            t0 = chunk_start(gi, j)
            slab = pc_ref[0, pl.ds(t0, CHUNK), :]
            vfirst = vf_ref[0, pl.ds(t0, CHUNK), :] if use_vres else None
            gens.append(_rwkv_pre(slab, vfirst, vec_ref[...], v1_ref[...], v2_ref[...], tril_ref[...],
                                  sl, le, eye_ref[...]))
        return gens

    def chain_tasks(gi):
        ys = [None] * RWKV_GROUP
        stats = [None] * RWKV_GROUP
        state = {}

        def step(j):
            def run():
                h_w = h_ref[...] if j == 0 else state["h"]
                q_n, d_n, g_w, c_w = (pre_ref[j * RWKV_SAVED + k] for k in range(4))
                res = _mm(jnp.concatenate([q_n, g_w], axis=0), _stack_heads(h_w))
                ys[j] = res[0:CHUNK] + d_n
                state["h"] = res[CHUNK:2 * CHUNK] + c_w
                if j == RWKV_GROUP - 1:
                    h_ref[...] = state["h"]
            return run

        def sums():
            for j in range(RWKV_GROUP):
                parts = _split_bf16(ys[j], 2) + [pre_ref[j * RWKV_SAVED + 4].astype(BF16)]
                s = jnp.dot(jnp.concatenate(parts, axis=0), bd_ref[...], preferred_element_type=F32)
                yc = ys[j] - (s[0:CHUNK] + s[CHUNK:2 * CHUNK]) * (1.0 / HEAD_DIM)
                stats[j] = (yc, s[2 * CHUNK:3 * CHUNK])

        def variance():
            for j in range(RWKV_GROUP):
                yc, bonus = stats[j]
                yv = _mm(yc * yc, bd_ref[...]) * (1.0 / HEAD_DIM)
                stats[j] = (yc, bonus, yv)

        def finish():
            gn_w = vec_ref[5:6, :]
            gn_b = vec_ref[6:7, :]
            for j in range(RWKV_GROUP):
                t0 = chunk_start(gi, j)
                yc, bonus, yv = stats[j]
                yn = yc * lax.rsqrt(yv + GN_EPS) * gn_w + gn_b
                out = (yn + bonus * pre_ref[j * RWKV_SAVED + 5]) * _silu(pre_ref[j * RWKV_SAVED + 6])
                ystage_ref[pl.ds(t0, CHUNK), :] = out.astype(BF16)

        return [step(j) for j in range(RWKV_GROUP)] + [sums, variance, finish]

    def run_group(gi, tasks):
        gens = pre_generators(gi)
        tasks = list(tasks)
        pre = [None] * RWKV_GROUP
        sweep = 0
        while any(p is None for p in pre):
            for j, gen in enumerate(gens):
                if pre[j] is None:
                    try:
                        next(gen)
                    except StopIteration as stop:
                        pre[j] = stop.value
            sweep += 1
            if tasks:
                tasks.pop(0)()
        for t in tasks:
            t()
        for j in range(RWKV_GROUP):
            for k in range(RWKV_SAVED):
                pre_ref[j * RWKV_SAVED + k] = pre[j][k]

    def flush():
        y_ref[0] = ystage_ref[...]
        h_ref[...] = jnp.zeros_like(h_ref)

    @pl.when(step_id < last_step)
    def _():
        run_group(0, chain_tasks(ngroups - 1) + [flush])

        def body(i, carry):
            run_group(i, chain_tasks(i - 1))
            return carry

        lax.fori_loop(1, ngroups, body, 0)

    @pl.when(step_id == last_step)
    def _():
        for t in chain_tasks(ngroups - 1) + [flush]:
            t()


def _rwkv_consts():
    idx = jnp.arange(STACK)
    same = (idx[:, None] // CHUNK) == (idx[None, :] // CHUNK)
    ci = jnp.arange(CHUNK)
    tril_l = (ci[None, :] <= ci[:, None]).astype(BF16)
    s_loc = idx[None, :] % CHUNK
    sl = (s_loc < ci[:, None]).astype(F32)
    le = (s_loc <= ci[:, None]).astype(F32)
    eye_w = (s_loc == ci[:, None]).astype(F32)
    return tril_l, same.astype(BF16), sl, le, eye_w


def _rwkv(pc, pc_first, vec_all, v1_all, v2_all, consts, layer):
    bsz, seq, _ = pc.shape
    use_vres = pc_first is not None
    sq = (STACK, STACK)
    wide = (CHUNK, STACK)
    cur = lambda b: jnp.minimum(b, bsz - 1)
    in_specs = [pl.BlockSpec((1, seq, PC_WIDTH), lambda b: (cur(b), 0, 0))]
    if use_vres:
        in_specs.append(pl.BlockSpec((1, seq, C_WIDTH), lambda b: (cur(b), 0, PC_V)))
    in_specs += [_layer_spec((8, C_WIDTH), layer),
                 _layer_spec((C_WIDTH, LANES), layer), _layer_spec((LANES, C_WIDTH), layer),
                 _const_spec((CHUNK, CHUNK)), _const_spec(sq),
                 _const_spec(wide), _const_spec(wide), _const_spec(wide)]
    args = (pc,) + ((pc_first,) if use_vres else ()) + (vec_all, v1_all, v2_all)
    return pl.pallas_call(
        functools.partial(_rwkv_kernel, use_vres),
        grid=(bsz + 1,),
        in_specs=in_specs,
        out_specs=pl.BlockSpec((1, seq, C_WIDTH), lambda b: (jnp.maximum(b - 1, 0), 0, 0)),
        out_shape=jax.ShapeDtypeStruct((bsz, seq, C_WIDTH), BF16),
        scratch_shapes=[pltpu.VMEM(wide, F32), pltpu.VMEM((RWKV_GROUP * RWKV_SAVED,) + wide, F32),
                        pltpu.VMEM((seq, C_WIDTH), BF16)],
        compiler_params=pltpu.CompilerParams(dimension_semantics=("arbitrary",),
                                             vmem_limit_bytes=VMEM_LIMIT),
        name="rwkv",
    )(*args, *consts)


def _out_kernel(alpha, tm, seq, fused, ya_ref, yb_ref, yc_ref, x_ref, w_ref, ln_ref, *rest):
    if fused:
        win_ref, *tok_refs = rest[0:6]
        o_ref, oa_ref, ob_ref, oc_ref, last_ref = rest[6:]
        _init_last(last_ref)
    else:
        (o_ref,) = rest

    def project(rows):
        acc = jnp.dot(ya_ref[rows, :], w_ref[0:A_WIDTH, :], preferred_element_type=F32)
        acc = acc + jnp.dot(yb_ref[rows, :], w_ref[A_WIDTH:A_WIDTH + B_WIDTH, :],
                            preferred_element_type=F32)
        return acc + jnp.dot(yc_ref[rows, :], w_ref[A_WIDTH + B_WIDTH:D_MIX, :],
                             preferred_element_type=F32)

    def norm(rows, acc, pending):
        z = alpha * x_ref[rows, :] + acc
        mu = jnp.mean(z, axis=-1, keepdims=True)
        zc = z - mu
        var = jnp.mean(zc * zc, axis=-1, keepdims=True)
        xn = zc * lax.rsqrt(var + LN_EPS) * ln_ref[0:1, :] + ln_ref[1:2, :]
        o_ref[rows, :] = xn
        if not fused:
            return None
        return _proj_rows(xn.astype(BF16), win_ref, rows.start, OUT_SUB, seq, tok_refs, last_ref,
                          oa_ref, ob_ref, oc_ref, pending)

    subs = [slice(s * OUT_SUB, (s + 1) * OUT_SUB) for s in range(tm // OUT_SUB)]
    acc_prev = project(subs[0])
    pending = None
    for s in range(len(subs)):
        acc_next = project(subs[s + 1]) if s + 1 < len(subs) else None
        pending = norm(subs[s], acc_prev, pending)
        acc_prev = acc_next
    _drain(pending)


def _out(ya, yb, yc, x2, w_all, ln_all, alpha, layer, seq, w_in_all=None, tok_args=()):
    n = x2.shape[0]
    fused = w_in_all is not None
    tm = FUSED_TM if fused else OUT_TM
    row = lambda w: pl.BlockSpec((tm, w), lambda i: (i, 0))
    in_specs = [row(A_WIDTH), row(B_WIDTH), row(C_WIDTH), row(D_MODEL),
                _layer_spec((D_MIX, D_MODEL), layer, single_buffer=True),
                _layer_spec((2, D_MODEL), layer)]
    out_specs = [row(D_MODEL)]
    out_shape = [jax.ShapeDtypeStruct((n, D_MODEL), F32)]
    args = [ya, yb, yc, x2, w_all, ln_all]
    if fused:
        in_specs.append(_layer_spec((D_IN_PAD, D_MODEL), layer + 1, single_buffer=True))
        in_specs += _tok_specs(layer + 1)
        out_specs += [row(GA_WIDTH), row(GB_WIDTH), row(PC_WIDTH)]
        out_shape += [jax.ShapeDtypeStruct((n, GA_WIDTH), BF16),
                      jax.ShapeDtypeStruct((n, GB_WIDTH), F32),
                      jax.ShapeDtypeStruct((n, PC_WIDTH), F32)]
        args += [w_in_all, *tok_args]
    return pl.pallas_call(
        functools.partial(_out_kernel, alpha, tm, seq, fused),
        grid=(n // tm,),
        in_specs=in_specs,
        out_specs=out_specs,
        out_shape=out_shape,
        scratch_shapes=[pltpu.VMEM((SUBLANES, C_SHIFT_PAD), F32)] if fused else [],
        compiler_params=pltpu.CompilerParams(dimension_semantics=("arbitrary",),
                                             vmem_limit_bytes=VMEM_LIMIT),
        name="outproj_proj" if fused else "outproj",
    )(*args)


def _block_diag_all(w):
    depth, nblk, d, _ = w.shape
    eye = jnp.eye(nblk, dtype=w.dtype)
    return (w[:, :, :, None, :] * eye[None, :, None, :, None]).reshape(depth, nblk * d, nblk * d)


def kernel(x, w_in, w_out, ln_g, ln_b, attn_sinks, conv_w, conv_b, lru_wa, lru_ba, lru_wx, lru_bx,
           lru_lambda, rwkv_mu, rwkv_w0, rwkv_w2, rwkv_a0, rwkv_a2, rwkv_kk, rwkv_ka, rwkv_rk,
           rwkv_gn_w, rwkv_gn_b, rwkv_v0, rwkv_v1, rwkv_v2):
    bsz, seq, dm = x.shape
    depth = w_in.shape[0]
    alpha = (2 * depth) ** 0.25
    n = bsz * seq
    pad_w = C_SHIFT_PAD - C_SHIFT_WIDTH

    q_scale = jnp.concatenate([jnp.full((A_WIDTH,), HEAD_DIM ** -0.5 * LOG2E, F32),
                               jnp.ones((w_in.shape[2] - A_WIDTH,), F32)])
    w_in_p = (jnp.swapaxes(w_in, 1, 2) * q_scale[None, :, None]).astype(BF16)
    w_out_b = w_out.astype(BF16)
    ln_all = jnp.stack([ln_g, ln_b], axis=1)
    lru_vec = jnp.stack([conv_b, lru_ba, lru_bx, lru_lambda], axis=1)
    wa_bd = _block_diag_all(lru_wa).astype(BF16)
    wx_bd = _block_diag_all(lru_wx).astype(BF16)
    mu_all = jnp.pad(rwkv_mu, ((0, 0), (0, pad_w)))[:, None, :]
    v0_all = jnp.pad(rwkv_v0, ((1, 0), (0, 0)))
    vec_all = jnp.stack([rwkv_w0, rwkv_a0, rwkv_kk, rwkv_ka, rwkv_rk.reshape(depth, C_WIDTH),
                         rwkv_gn_w, rwkv_gn_b, v0_all], axis=1)
    w2_all = jnp.pad(rwkv_w2, ((0, 0), (0, LANES - DECAY_RANK), (0, 0))).astype(BF16)
    a2_all = jnp.pad(rwkv_a2, ((0, 0), (DECAY_RANK, LANES - DECAY_RANK - AICL_RANK), (0, 0))).astype(BF16)
    v1_all = jnp.pad(rwkv_v1, ((1, 0), (0, 0), (0, LANES - VRES_RANK))).astype(BF16)
    v2_all = jnp.pad(rwkv_v2, ((1, 0), (0, LANES - VRES_RANK), (0, 0))).astype(BF16)
    consts = _rwkv_consts()

    tok_args = (mu_all, vec_all, w2_all, a2_all, consts[1])
    x2 = x.reshape(n, dm)
    pc_first = None
    pa, pb, pc = _proj(x2, w_in_p, tok_args, 0, seq)
    for l in range(depth):
        pc3 = pc.reshape(bsz, seq, PC_WIDTH)
        ya = _attn(pa.reshape(bsz, seq, GA_WIDTH), attn_sinks, l)
        yb = _lru(pb.reshape(bsz, seq, GB_WIDTH), conv_w, lru_vec, wa_bd, wx_bd, l)
        yc = _rwkv(pc3, pc_first, vec_all, v1_all, v2_all, consts, l)
        if l == 0:
            pc_first = pc3
        outs = _out(ya.reshape(n, A_WIDTH), yb.reshape(n, B_WIDTH), yc.reshape(n, C_WIDTH), x2,
                    w_out_b, ln_all, alpha, l, seq, w_in_p if l + 1 < depth else None, tok_args)
        x2 = outs[0]
        if l + 1 < depth:
            pa, pb, pc = outs[1:]
    return x2.reshape(bsz, seq, dm)
```

```python
import functools
import math

import jax
import jax.numpy as jnp
from jax import lax
from jax.experimental import pallas as pl
from jax.experimental.pallas import tpu as pltpu

F32 = jnp.float32
BF16 = jnp.bfloat16

D_MODEL = 1024
HEAD_DIM = 64
A_Q_HEADS = 8
A_KV_HEADS = 2
A_WIDTH = A_Q_HEADS * HEAD_DIM
A_KV_WIDTH = A_KV_HEADS * HEAD_DIM
ATT_BLOCK = 128
B_WIDTH = 256
B_BLOCKS = 4
CONV_WIDTH = 4
LRU_C = 8.0
C_HEADS = 4
C_WIDTH = C_HEADS * HEAD_DIM
DECAY_RANK = 32
AICL_RANK = 32
VRES_RANK = 16
C_SHIFT_WIDTH = 3 * C_WIDTH + DECAY_RANK + AICL_RANK
GN_EPS = 64e-5
LN_EPS = 1e-5
LOG2E = math.log2(math.e)

LANES = 128
SUBLANES = 8
C_SHIFT_PAD = ((C_SHIFT_WIDTH + LANES - 1) // LANES) * LANES
GA_WIDTH = A_WIDTH + 2 * A_KV_WIDTH + A_WIDTH
GB_WIDTH = 2 * B_WIDTH
GC_WIDTH = C_SHIFT_WIDTH + C_WIDTH
D_IN_PAD = GA_WIDTH + GB_WIDTH + GC_WIDTH
D_MIX = A_WIDTH + B_WIDTH + C_WIDTH
PC_R, PC_K, PC_V, PC_KAP, PC_B, PC_LW, PC_G = range(7)
PC_WIDTH = 7 * C_WIDTH

CHUNK = 64
STACK = C_HEADS * CHUNK
RWKV_GROUP = 8
ATT_TILE = 1024
LRU_TILE = 256
LRU_STAGE_ROWS = SUBLANES * (LRU_TILE // SUBLANES + 4)
PROJ_TM = 1024
OUT_TM = 1024
FUSED_TM = 1024
OUT_SUB = 256
VMEM_LIMIT = 56 * 1024 * 1024


def _mm(a, b):
    return jnp.dot(a.astype(BF16), b.astype(BF16), preferred_element_type=F32)


def _mm_nt(a, b):
    return lax.dot_general(a.astype(BF16), b.astype(BF16), (((1,), (1,)), ((), ())),
                           preferred_element_type=F32)


def _mm_tn(a, b):
    return lax.dot_general(a.astype(BF16), b.astype(BF16), (((0,), (0,)), ((), ())),
                           preferred_element_type=F32)


def _split_bf16(x, parts):
    out = []
    rem = x
    for _ in range(parts):
        hi = rem.astype(BF16)
        out.append(hi)
        rem = rem - hi.astype(F32)
    return out


def _mm_exact_lhs(m_bf16, x, parts):
    n = x.shape[1]
    t = jnp.dot(m_bf16, jnp.concatenate(_split_bf16(x, parts), axis=1), preferred_element_type=F32)
    acc = t[:, 0:n]
    for p in range(1, parts):
        acc = acc + t[:, p * n:(p + 1) * n]
    return acc


def _sigmoid(x):
    return 1.0 / (1.0 + jnp.exp2(x * (-LOG2E)))


def _silu(x):
    return x * _sigmoid(x)


def _softplus(x):
    return jnp.maximum(x, 0.0) + jnp.log(1.0 + jnp.exp(-jnp.abs(x)))


def _layer_spec(shape, layer, single_buffer=False):
    index_map = lambda *_: (layer,) + (0,) * len(shape)
    if single_buffer:
        return pl.BlockSpec((None,) + shape, index_map, pipeline_mode=pl.Buffered(1))
    return pl.BlockSpec((None,) + shape, index_map)


def _const_spec(shape):
    return pl.BlockSpec(shape, lambda *_: (0,) * len(shape))


def _rwkv_tokenwise(cs, g, prev_row, mu, vec, w2p, a2p, ones_bd):
    w0, a0, k_k, k_a = (vec[i:i + 1, :] for i in range(4))
    row = lax.broadcasted_iota(jnp.int32, cs.shape, 0)
    prev = jnp.where(row == 0, prev_row, pltpu.roll(cs, 1, axis=0))
    xs = cs + (prev - cs) * mu
    r = xs[:, 0:C_WIDTH]
    k = xs[:, C_WIDTH:2 * C_WIDTH]
    v = xs[:, 2 * C_WIDTH:3 * C_WIDTH]
    la = xs[:, 3 * C_WIDTH:C_SHIFT_PAD]
    kk = k * k_k
    yield
    z_w = _mm(jnp.tanh(la), w2p)
    z_a = _mm(la, a2p)
    norm2 = _mm(kk * kk, ones_bd)
    yield
    lw = (-LOG2E * math.exp(-0.5)) * _sigmoid(w0 + z_w)
    a = _sigmoid(a0 + z_a)
    kap = kk * lax.rsqrt(norm2 + 1e-12)
    k2 = k * (1.0 + (a - 1.0) * k_a)
    return jnp.concatenate([r, k2, v, kap, kap * a, lw, g], axis=1), cs[cs.shape[0] - 1:, :]


def _drain(stage):
    if stage is not None:
        for _ in stage:
            pass


def _proj_rows(x16, w_ref, row0, nrows, seq, tok_refs, last_ref, oa_ref, ob_ref, oc_ref, pending):
    mu_ref, vec_ref, w2_ref, a2_ref, ones_ref = tok_refs
    nt = (((1,), (1,)), ((), ()))
    rows = pl.ds(row0, nrows)

    def advance():
        if pending is not None:
            next(pending, None)

    oa_ref[rows, :] = lax.dot_general(x16, w_ref[0:GA_WIDTH, :], nt,
                                      preferred_element_type=F32).astype(BF16)
    advance()
    ob_ref[rows, :] = lax.dot_general(x16, w_ref[GA_WIDTH:GA_WIDTH + GB_WIDTH, :], nt,
                                      preferred_element_type=F32)
    advance()
    c0 = GA_WIDTH + GB_WIDTH
    cs = lax.dot_general(x16, w_ref[c0:c0 + C_SHIFT_PAD, :], nt, preferred_element_type=F32)
    advance()
    g = lax.dot_general(x16, w_ref[c0 + C_SHIFT_WIDTH:D_IN_PAD, :], nt, preferred_element_type=F32)
    _drain(pending)
    first = pl.program_id(0) * oa_ref.shape[0] + row0

    def stage():
        prev_row = jnp.where(first % seq == 0, 0.0, last_ref[0:1, :])
        slab, last = yield from _rwkv_tokenwise(cs, g, prev_row, mu_ref[...], vec_ref[...],
                                                w2_ref[...], a2_ref[...], ones_ref[...])
        oc_ref[rows, :] = slab
        last_ref[0:1, :] = last

    return stage()


def _init_last(last_ref):
    @pl.when(pl.program_id(0) == 0)
    def _():
        last_ref[...] = jnp.zeros_like(last_ref)


def _proj_kernel(seq, x_ref, w_ref, mu_ref, vec_ref, w2_ref, a2_ref, ones_ref, oa_ref, ob_ref, oc_ref,
                 last_ref):
    _init_last(last_ref)
    pending = None
    for row0 in range(0, PROJ_TM, OUT_SUB):
        pending = _proj_rows(x_ref[pl.ds(row0, OUT_SUB), :].astype(BF16), w_ref, row0, OUT_SUB, seq,
                             (mu_ref, vec_ref, w2_ref, a2_ref, ones_ref), last_ref, oa_ref, ob_ref,
                             oc_ref, pending)
    _drain(pending)


def _tok_specs(layer):
    return [_layer_spec((1, C_SHIFT_PAD), layer), _layer_spec((8, C_WIDTH), layer),
            _layer_spec((LANES, C_WIDTH), layer), _layer_spec((LANES, C_WIDTH), layer),
            _const_spec((STACK, STACK))]


def _proj(x2, w_all, tok_args, layer, seq):
    n = x2.shape[0]
    return pl.pallas_call(
        functools.partial(_proj_kernel, seq),
        grid=(n // PROJ_TM,),
        in_specs=[pl.BlockSpec((PROJ_TM, D_MODEL), lambda i: (i, 0)),
                  _layer_spec((D_IN_PAD, D_MODEL), layer, single_buffer=True)] + _tok_specs(layer),
        out_specs=[pl.BlockSpec((PROJ_TM, GA_WIDTH), lambda i: (i, 0)),
                   pl.BlockSpec((PROJ_TM, GB_WIDTH), lambda i: (i, 0)),
                   pl.BlockSpec((PROJ_TM, PC_WIDTH), lambda i: (i, 0))],
        out_shape=[jax.ShapeDtypeStruct((n, GA_WIDTH), BF16),
                   jax.ShapeDtypeStruct((n, GB_WIDTH), F32),
                   jax.ShapeDtypeStruct((n, PC_WIDTH), F32)],
        scratch_shapes=[pltpu.VMEM((SUBLANES, C_SHIFT_PAD), F32)],
        compiler_params=pltpu.CompilerParams(dimension_semantics=("arbitrary",),
                                             vmem_limit_bytes=VMEM_LIMIT),
        name="proj",
    )(x2, w_all, *tok_args)


def _attn_kernel(layer, sink_ref, cur_ref, prev_ref, o_ref):
    n = pl.program_id(1)
    blk = ATT_BLOCK
    nsub = ATT_TILE // blk
    grp = A_Q_HEADS // A_KV_HEADS
    lo = lax.broadcasted_iota(jnp.int32, (blk, A_KV_WIDTH), 1) < HEAD_DIM

    def variants(x16):
        x = x16.astype(F32)
        xs = pltpu.roll(x, HEAD_DIM, axis=1)
        return ((jnp.where(lo, x, 0.0).astype(BF16), jnp.where(lo, 0.0, xs).astype(BF16)),
                (jnp.where(lo, xs, 0.0).astype(BF16), jnp.where(lo, 0.0, x).astype(BF16)))

    k0, v0 = A_WIDTH, A_WIDTH + A_KV_WIDTH
    g0 = A_WIDTH + 2 * A_KV_WIDTH
    kblk = [variants(prev_ref[0, :, 0:A_KV_WIDTH])]
    vblk = [variants(prev_ref[0, :, A_KV_WIDTH:2 * A_KV_WIDTH])]
    for j in range(nsub):
        kblk.append(variants(cur_ref[0, j * blk:(j + 1) * blk, k0:k0 + A_KV_WIDTH]))
        vblk.append(variants(cur_ref[0, j * blk:(j + 1) * blk, v0:v0 + A_KV_WIDTH]))

    qi = lax.broadcasted_iota(jnp.int32, (blk, 2 * blk), 0)
    kj = lax.broadcasted_iota(jnp.int32, (blk, 2 * blk), 1)
    diff = qi + blk - kj
    band = (diff >= 0) & (diff < blk)
    first = band & (kj + (n * nsub - 1) * blk >= 0)

    for j in range(nsub):
        mask = first if j == 0 else band
        rows = slice(j * blk, (j + 1) * blk)
        scores = []
        for p in range(A_Q_HEADS // 2):
            h = (2 * p) // grp
            qp = cur_ref[0, rows, p * LANES:(p + 1) * LANES]
            for half in range(2):
                kb = jnp.concatenate([kblk[j][h][half], kblk[j + 1][h][half]], axis=0)
                scores.append(lax.dot_general(qp, kb, (((1,), (1,)), ((), ())),
                                              preferred_element_type=F32))
        probs, inv = [], []
        for idx, s in enumerate(scores):
            sink2 = sink_ref[layer, idx] * LOG2E
            s = jnp.where(mask, s, -jnp.inf)
            m = jnp.maximum(jnp.max(s, axis=-1, keepdims=True), sink2)
            e = jnp.exp2(s - m)
            den = jnp.sum(e, axis=-1, keepdims=True) + jnp.exp2(sink2 - m)
            probs.append(e.astype(BF16))
            inv.append(1.0 / den)
        for p in range(A_Q_HEADS // 2):
            h = (2 * p) // grp
            acc = None
            for half in range(2):
                vb = jnp.concatenate([vblk[j][h][half], vblk[j + 1][h][half]], axis=0)
                o = jnp.dot(probs[2 * p + half], vb, preferred_element_type=F32) * inv[2 * p + half]
                acc = o if acc is None else acc + o
            g = cur_ref[0, rows, g0 + p * LANES:g0 + (p + 1) * LANES].astype(F32)
            o_ref[0, rows, p * LANES:(p + 1) * LANES] = (acc * _silu(g)).astype(BF16)


def _attn(pa, sinks, layer):
    bsz, seq, _ = pa.shape
    nsub = ATT_TILE // ATT_BLOCK
    kv_blk = A_WIDTH // (2 * A_KV_WIDTH)
    return pl.pallas_call(
        functools.partial(_attn_kernel, layer),
        grid=(bsz, seq // ATT_TILE),
        in_specs=[pl.BlockSpec(memory_space=pltpu.SMEM),
                  pl.BlockSpec((1, ATT_TILE, GA_WIDTH), lambda b, n: (b, n, 0)),
                  pl.BlockSpec((1, ATT_BLOCK, 2 * A_KV_WIDTH),
                               lambda b, n: (b, jnp.maximum(n * nsub - 1, 0), kv_blk))],
        out_specs=pl.BlockSpec((1, ATT_TILE, A_WIDTH), lambda b, n: (b, n, 0)),
        out_shape=jax.ShapeDtypeStruct((bsz, seq, A_WIDTH), BF16),
        compiler_params=pltpu.CompilerParams(dimension_semantics=("arbitrary", "arbitrary"),
                                             vmem_limit_bytes=VMEM_LIMIT),
        name="attn",
    )(sinks, pa, pa)


def _lru_kernel(pb_ref, cw_ref, vec_ref, wa_ref, wx_ref, o_ref, in_ref, stage_ref):
    seq = pb_ref.shape[1]
    tile = LRU_TILE
    nseg = SUBLANES
    nstep = tile // nseg
    conv_b = vec_ref[0:1, :]
    ba = vec_ref[1:2, :]
    bx = vec_ref[2:3, :]
    lam = vec_ref[3:4, :]
    neg_c_sp = -LRU_C * _softplus(-lam)
    cw = [cw_ref[j:j + 1, :] for j in range(CONV_WIDTH)]
    sub0 = lax.broadcasted_iota(jnp.int32, (nseg, B_WIDTH), 0) == 0

    nhalf = B_WIDTH // LANES

    pitch = nstep + 4
    assert pitch % SUBLANES == 4

    def rows_of(step):
        return pl.ds(step, nseg, stride=pitch)

    def permuted(k0, step):
        return jnp.concatenate([in_ref[k0 + k, rows_of(step), :] for k in range(nhalf)], axis=1)

    hc = jnp.zeros((1, B_WIDTH), F32)
    tail = [jnp.zeros((1, B_WIDTH), F32)] * (CONV_WIDTH - 1)
    for ti in range(seq // tile):
        t0 = ti * tile
        for k in range(2 * nhalf):
            for sgi in range(nseg):
                in_ref[k, sgi * pitch:sgi * pitch + nstep, :] = pb_ref[
                    0, t0 + sgi * nstep:t0 + (sgi + 1) * nstep, k * LANES:(k + 1) * LANES]
        xs = [permuted(0, v) for v in range(nstep)]
        wrapped = []
        for i in range(CONV_WIDTH - 1):
            src_v = xs[nstep - (CONV_WIDTH - 1) + i]
            wrapped.append(jnp.where(sub0, tail[i], pltpu.roll(src_v, 1, axis=0)))
        x_all = jnp.concatenate(xs, axis=0)
        xc = cw[CONV_WIDTH - 1] * x_all + conv_b
        for j in range(1, CONV_WIDTH):
            shifted = jnp.concatenate(wrapped[CONV_WIDTH - 1 - j:] + xs[:nstep - j], axis=0)
            xc = xc + cw[CONV_WIDTH - 1 - j] * shifted
        tail = [xs[nstep - (CONV_WIDTH - 1) + i][nseg - 1:nseg, :] for i in range(CONV_WIDTH - 1)]

        r = _sigmoid(_mm(xc, wa_ref[...]) + ba)
        ig = _sigmoid(_mm(xc, wx_ref[...]) + bx)
        a = jnp.exp(neg_c_sp * r)
        u = jnp.sqrt(1.0 - a * a) * (ig * xc)
        hl = [u[0:nseg]]
        al = [a[0:nseg]]
        for v in range(1, nstep):
            av = a[v * nseg:(v + 1) * nseg]
            hl.append(av * hl[-1] + u[v * nseg:(v + 1) * nseg])
            al.append(av * al[-1])
        carry = hc
        enter = []
        for sgi in range(nseg):
            enter.append(carry)
            carry = hl[-1][sgi:sgi + 1, :] + al[-1][sgi:sgi + 1, :] * carry
        hc = carry
        enter = jnp.concatenate(enter, axis=0)
        for v in range(nstep):
            out = (hl[v] + al[v] * enter) * _silu(permuted(nhalf, v))
            for k in range(nhalf):
                stage_ref[k, rows_of(v), :] = out[:, k * LANES:(k + 1) * LANES]
        for k in range(nhalf):
            for sgi in range(nseg):
                o_ref[0, t0 + sgi * nstep:t0 + (sgi + 1) * nstep, k * LANES:(k + 1) * LANES] = (
                    stage_ref[k, sgi * pitch:sgi * pitch + nstep, :].astype(BF16))


def _lru(pb, cw_all, vec_all, wa_all, wx_all, layer):
    bsz, seq, _ = pb.shape
    return pl.pallas_call(
        _lru_kernel,
        grid=(bsz,),
        in_specs=[pl.BlockSpec((1, seq, GB_WIDTH), lambda b: (b, 0, 0)),
                  _layer_spec((CONV_WIDTH, B_WIDTH), layer), _layer_spec((4, B_WIDTH), layer),
                  _layer_spec((B_WIDTH, B_WIDTH), layer), _layer_spec((B_WIDTH, B_WIDTH), layer)],
        out_specs=pl.BlockSpec((1, seq, B_WIDTH), lambda b: (b, 0, 0)),
        out_shape=jax.ShapeDtypeStruct((bsz, seq, B_WIDTH), BF16),
        scratch_shapes=[pltpu.VMEM((GB_WIDTH // LANES, LRU_STAGE_ROWS, LANES), F32),
                        pltpu.VMEM((B_WIDTH // LANES, LRU_STAGE_ROWS, LANES), F32)],
        compiler_params=pltpu.CompilerParams(dimension_semantics=("arbitrary",),
                                             vmem_limit_bytes=VMEM_LIMIT),
        name="lru",
    )(pb, cw_all, vec_all, wa_all, wx_all)


def _stack_heads(x):
    x16 = x.astype(BF16)
    head = lax.broadcasted_iota(jnp.int32, x16.shape, 1) // HEAD_DIM
    return jnp.concatenate([jnp.where(head == h, x16, jnp.zeros_like(x16)) for h in range(C_HEADS)],
                           axis=0)


def _unstack_heads(x):
    head = lax.broadcasted_iota(jnp.int32, (CHUNK, STACK), 1) // HEAD_DIM
    out = x[(C_HEADS - 1) * CHUNK:C_HEADS * CHUNK]
    for h in range(C_HEADS - 2, -1, -1):
        out = jnp.where(head == h, x[h * CHUNK:(h + 1) * CHUNK], out)
    return out


def _rwkv_pre(slab, vfirst, vec, v1p, v2p, tril_l, sl, le, eye_w):
    r, k2, v, kap, b, lw = (slab[:, i * C_WIDTH:(i + 1) * C_WIDTH]
                            for i in (PC_R, PC_K, PC_V, PC_KAP, PC_B, PC_LW))
    r_k = vec[4:5, :]
    v0 = vec[7:8, :]
    if vfirst is not None:
        v = v + (vfirst - v) * _sigmoid(v0 + _mm(_mm(v, v1p), v2p))
    yield
    cum = _mm_exact_lhs(tril_l, lw, 2)
    cum_end = cum[CHUNK - 1:CHUNK, :]
    e_in = jnp.exp2(cum)
    e_ex = jnp.exp2(cum - lw)
    e_neg = jnp.exp2(-cum)
    e_end = jnp.exp2(cum_end - cum)
    p_end = jnp.exp2(cum_end)

    yield
    kt = kap * e_ex
    rt = r * e_in
    kt_st = _stack_heads(kt)
    vs_st = _stack_heads(v)
    gram = _mm_nt(jnp.concatenate([kt, rt], axis=0),
                  jnp.concatenate([_stack_heads(b * e_neg), _stack_heads(k2 * e_neg)], axis=0))
    a_ab = jnp.where(sl, gram[0:CHUNK, 0:STACK], 0.0)
    a_ak = jnp.where(sl, gram[0:CHUNK, STACK:2 * STACK], 0.0)
    a_rb = jnp.where(le, gram[CHUNK:2 * CHUNK, 0:STACK], 0.0)
    a_rk = jnp.where(le, gram[CHUNK:2 * CHUNK, STACK:2 * STACK], 0.0)

    yield
    pw = -a_ab
    t_w = eye_w + pw
    av = _mm(jnp.concatenate([a_ak, a_rk], axis=0), vs_st)
    pw = _mm(pw, _stack_heads(pw))
    for _ in range(1, int(math.log2(CHUNK)) - 1):
        yield
        both = _mm(jnp.concatenate([t_w, pw], axis=0), _stack_heads(pw))
        t_w = t_w + both[0:CHUNK]
        pw = both[CHUNK:2 * CHUNK]
    yield
    t_w = t_w + _mm(t_w, _stack_heads(pw))
    yield
    tx = _mm(t_w, jnp.concatenate([kt_st, _stack_heads(av[0:CHUNK])], axis=1))
    yield
    m1 = tx[:, 0:STACK]
    c1 = tx[:, STACK:2 * STACK]
    ab = _mm(a_rb, jnp.concatenate([_stack_heads(m1), _stack_heads(c1)], axis=1))
    q_n = rt - ab[:, 0:STACK]
    d_n = av[CHUNK:2 * CHUNK] - ab[:, STACK:2 * STACK]
    bh = b * e_end
    g_w = eye_w * p_end - _unstack_heads(_mm_tn(bh, m1))
    c_w = _unstack_heads(_mm_tn(jnp.concatenate([k2 * e_end, bh], axis=0),
                                jnp.concatenate([v, -c1], axis=0)))
    bonus_arg = r * k2 * r_k
    return q_n, d_n, g_w, c_w, bonus_arg, v, slab[:, PC_G * C_WIDTH:PC_WIDTH]


RWKV_SAVED = 7


def _rwkv_kernel(use_vres, *refs):
    if use_vres:
        (pc_ref, vf_ref, vec_ref, v1_ref, v2_ref, tril_ref, bd_ref,
         sl_ref, le_ref, eye_ref, y_ref, h_ref, pre_ref, ystage_ref) = refs
    else:
        (pc_ref, vec_ref, v1_ref, v2_ref, tril_ref, bd_ref,
         sl_ref, le_ref, eye_ref, y_ref, h_ref, pre_ref, ystage_ref) = refs
        vf_ref = None
    seq = pc_ref.shape[1]
    ngroups = seq // (CHUNK * RWKV_GROUP)
    step_id = pl.program_id(0)
    last_step = pl.num_programs(0) - 1

    @pl.when(step_id == 0)
    def _():
        h_ref[...] = jnp.zeros_like(h_ref)
        pre_ref[...] = jnp.zeros_like(pre_ref)
        ystage_ref[...] = jnp.zeros_like(ystage_ref)

    def chunk_start(gi, j):
        return pl.multiple_of((gi * RWKV_GROUP + j) * CHUNK, CHUNK)

    def pre_generators(gi):
        sl = sl_ref[...] > 0.0
        le = le_ref[...] > 0.0
        gens = []
        for j in range(RWKV_GROUP):
            t0 = chunk_start(gi, j)
            slab = pc_ref[0, pl.ds(t0, CHUNK), :]
            vfirst = vf_ref[0, pl.ds(t0, CHUNK), :] if use_vres else None
            gens.append(_rwkv_pre(slab, vfirst, vec_ref[...], v1_ref[...], v2_ref[...], tril_ref[...],
                                  sl, le, eye_ref[...]))
        return gens

    def chain_tasks(gi):
        ys = [None] * RWKV_GROUP
        stats = [None] * RWKV_GROUP
        state = {}

        def step(j):
            def run():
                h_w = h_ref[...] if j == 0 else state["h"]
                q_n, d_n, g_w, c_w = (pre_ref[j * RWKV_SAVED + k] for k in range(4))
                res = _mm(jnp.concatenate([q_n, g_w], axis=0), _stack_heads(h_w))
                ys[j] = res[0:CHUNK] + d_n
                state["h"] = res[CHUNK:2 * CHUNK] + c_w
                if j == RWKV_GROUP - 1:
                    h_ref[...] = state["h"]
            return run

        def sums():
            for j in range(RWKV_GROUP):
                parts = _split_bf16(ys[j], 2) + [pre_ref[j * RWKV_SAVED + 4].astype(BF16)]
                s = jnp.dot(jnp.concatenate(parts, axis=0), bd_ref[...], preferred_element_type=F32)
                yc = ys[j] - (s[0:CHUNK] + s[CHUNK:2 * CHUNK]) * (1.0 / HEAD_DIM)
                stats[j] = (yc, s[2 * CHUNK:3 * CHUNK])

        def variance():
            for j in range(RWKV_GROUP):
                yc, bonus = stats[j]
                yv = _mm(yc * yc, bd_ref[...]) * (1.0 / HEAD_DIM)
                stats[j] = (yc, bonus, yv)

        def finish():
            gn_w = vec_ref[5:6, :]
            gn_b = vec_ref[6:7, :]
            for j in range(RWKV_GROUP):
                t0 = chunk_start(gi, j)
                yc, bonus, yv = stats[j]
                yn = yc * lax.rsqrt(yv + GN_EPS) * gn_w + gn_b
                out = (yn + bonus * pre_ref[j * RWKV_SAVED + 5]) * _silu(pre_ref[j * RWKV_SAVED + 6])
                ystage_ref[pl.ds(t0, CHUNK), :] = out.astype(BF16)

        return [step(j) for j in range(RWKV_GROUP)] + [sums, variance, finish]

    def run_group(gi, tasks):
        gens = pre_generators(gi)
        tasks = list(tasks)
        pre = [None] * RWKV_GROUP
        sweep = 0
        while any(p is None for p in pre):
            for j, gen in enumerate(gens):
                if pre[j] is None:
                    try:
                        next(gen)
                    except StopIteration as stop:
                        pre[j] = stop.value
            sweep += 1
            if tasks and sweep >= 2:
                tasks.pop(0)()
        for t in tasks:
            t()
        for j in range(RWKV_GROUP):
            for k in range(RWKV_SAVED):
                pre_ref[j * RWKV_SAVED + k] = pre[j][k]

    def flush():
        y_ref[0] = ystage_ref[...]
        h_ref[...] = jnp.zeros_like(h_ref)

    @pl.when(step_id < last_step)
    def _():
        run_group(0, chain_tasks(ngroups - 1) + [flush])

        def body(i, carry):
            run_group(i, chain_tasks(i - 1))
            return carry

        lax.fori_loop(1, ngroups, body, 0)

    @pl.when(step_id == last_step)
    def _():
        for t in chain_tasks(ngroups - 1) + [flush]:
            t()


def _rwkv_consts():
    idx = jnp.arange(STACK)
    same = (idx[:, None] // CHUNK) == (idx[None, :] // CHUNK)
    ci = jnp.arange(CHUNK)
    tril_l = (ci[None, :] <= ci[:, None]).astype(BF16)
    s_loc = idx[None, :] % CHUNK
    sl = (s_loc < ci[:, None]).astype(F32)
    le = (s_loc <= ci[:, None]).astype(F32)
    eye_w = (s_loc == ci[:, None]).astype(F32)
    return tril_l, same.astype(BF16), sl, le, eye_w


def _rwkv(pc, pc_first, vec_all, v1_all, v2_all, consts, layer):
    bsz, seq, _ = pc.shape
    use_vres = pc_first is not None
    sq = (STACK, STACK)
    wide = (CHUNK, STACK)
    cur = lambda b: jnp.minimum(b, bsz - 1)
    in_specs = [pl.BlockSpec((1, seq, PC_WIDTH), lambda b: (cur(b), 0, 0))]
    if use_vres:
        in_specs.append(pl.BlockSpec((1, seq, C_WIDTH), lambda b: (cur(b), 0, PC_V)))
    in_specs += [_layer_spec((8, C_WIDTH), layer),
                 _layer_spec((C_WIDTH, LANES), layer), _layer_spec((LANES, C_WIDTH), layer),
                 _const_spec((CHUNK, CHUNK)), _const_spec(sq),
                 _const_spec(wide), _const_spec(wide), _const_spec(wide)]
    args = (pc,) + ((pc_first,) if use_vres else ()) + (vec_all, v1_all, v2_all)
    return pl.pallas_call(
        functools.partial(_rwkv_kernel, use_vres),
        grid=(bsz + 1,),
        in_specs=in_specs,
        out_specs=pl.BlockSpec((1, seq, C_WIDTH), lambda b: (jnp.maximum(b - 1, 0), 0, 0)),
        out_shape=jax.ShapeDtypeStruct((bsz, seq, C_WIDTH), BF16),
        scratch_shapes=[pltpu.VMEM(wide, F32), pltpu.VMEM((RWKV_GROUP * RWKV_SAVED,) + wide, F32),
                        pltpu.VMEM((seq, C_WIDTH), BF16)],
        compiler_params=pltpu.CompilerParams(dimension_semantics=("arbitrary",),
                                             vmem_limit_bytes=VMEM_LIMIT),
        name="rwkv",
    )(*args, *consts)


def _out_kernel(alpha, tm, seq, fused, ya_ref, yb_ref, yc_ref, x_ref, w_ref, ln_ref, *rest):
    if fused:
        win_ref, *tok_refs = rest[0:6]
        o_ref, oa_ref, ob_ref, oc_ref, last_ref = rest[6:]
        _init_last(last_ref)
    else:
        (o_ref,) = rest

    def project(rows):
        acc = jnp.dot(ya_ref[rows, :], w_ref[0:A_WIDTH, :], preferred_element_type=F32)
        acc = acc + jnp.dot(yb_ref[rows, :], w_ref[A_WIDTH:A_WIDTH + B_WIDTH, :],
                            preferred_element_type=F32)
        return acc + jnp.dot(yc_ref[rows, :], w_ref[A_WIDTH + B_WIDTH:D_MIX, :],
                             preferred_element_type=F32)

    def norm(rows, acc, pending):
        z = alpha * x_ref[rows, :] + acc
        mu = jnp.mean(z, axis=-1, keepdims=True)
        zc = z - mu
        var = jnp.mean(zc * zc, axis=-1, keepdims=True)
        xn = zc * lax.rsqrt(var + LN_EPS) * ln_ref[0:1, :] + ln_ref[1:2, :]
        o_ref[rows, :] = xn
        if not fused:
            return None
        return _proj_rows(xn.astype(BF16), win_ref, rows.start, OUT_SUB, seq, tok_refs, last_ref,
                          oa_ref, ob_ref, oc_ref, pending)

    subs = [slice(s * OUT_SUB, (s + 1) * OUT_SUB) for s in range(tm // OUT_SUB)]
    acc_prev = project(subs[0])
    pending = None
    for s in range(len(subs)):
        acc_next = project(subs[s + 1]) if s + 1 < len(subs) else None
        pending = norm(subs[s], acc_prev, pending)
        acc_prev = acc_next
    _drain(pending)


def _out(ya, yb, yc, x2, w_all, ln_all, alpha, layer, seq, w_in_all=None, tok_args=()):
    n = x2.shape[0]
    fused = w_in_all is not None
    tm = FUSED_TM if fused else OUT_TM
    row = lambda w: pl.BlockSpec((tm, w), lambda i: (i, 0))
    in_specs = [row(A_WIDTH), row(B_WIDTH), row(C_WIDTH), row(D_MODEL),
                _layer_spec((D_MIX, D_MODEL), layer, single_buffer=True),
                _layer_spec((2, D_MODEL), layer)]
    out_specs = [row(D_MODEL)]
    out_shape = [jax.ShapeDtypeStruct((n, D_MODEL), F32)]
    args = [ya, yb, yc, x2, w_all, ln_all]
    if fused:
        in_specs.append(_layer_spec((D_IN_PAD, D_MODEL), layer + 1, single_buffer=True))
        in_specs += _tok_specs(layer + 1)
        out_specs += [row(GA_WIDTH), row(GB_WIDTH), row(PC_WIDTH)]
        out_shape += [jax.ShapeDtypeStruct((n, GA_WIDTH), BF16),
                      jax.ShapeDtypeStruct((n, GB_WIDTH), F32),
                      jax.ShapeDtypeStruct((n, PC_WIDTH), F32)]
        args += [w_in_all, *tok_args]
    return pl.pallas_call(
        functools.partial(_out_kernel, alpha, tm, seq, fused),
        grid=(n // tm,),
        in_specs=in_specs,
        out_specs=out_specs,
        out_shape=out_shape,
        scratch_shapes=[pltpu.VMEM((SUBLANES, C_SHIFT_PAD), F32)] if fused else [],
        compiler_params=pltpu.CompilerParams(dimension_semantics=("arbitrary",),
                                             vmem_limit_bytes=VMEM_LIMIT),
        name="outproj_proj" if fused else "outproj",
    )(*args)


def _block_diag_all(w):
    depth, nblk, d, _ = w.shape
    eye = jnp.eye(nblk, dtype=w.dtype)
    return (w[:, :, :, None, :] * eye[None, :, None, :, None]).reshape(depth, nblk * d, nblk * d)


def kernel(x, w_in, w_out, ln_g, ln_b, attn_sinks, conv_w, conv_b, lru_wa, lru_ba, lru_wx, lru_bx,
           lru_lambda, rwkv_mu, rwkv_w0, rwkv_w2, rwkv_a0, rwkv_a2, rwkv_kk, rwkv_ka, rwkv_rk,
           rwkv_gn_w, rwkv_gn_b, rwkv_v0, rwkv_v1, rwkv_v2):
    bsz, seq, dm = x.shape
    depth = w_in.shape[0]
    alpha = (2 * depth) ** 0.25
    n = bsz * seq
    pad_w = C_SHIFT_PAD - C_SHIFT_WIDTH

    q_scale = jnp.concatenate([jnp.full((A_WIDTH,), HEAD_DIM ** -0.5 * LOG2E, F32),
                               jnp.ones((w_in.shape[2] - A_WIDTH,), F32)])
    w_in_p = (jnp.swapaxes(w_in, 1, 2) * q_scale[None, :, None]).astype(BF16)
    w_out_b = w_out.astype(BF16)
    ln_all = jnp.stack([ln_g, ln_b], axis=1)
    lru_vec = jnp.stack([conv_b, lru_ba, lru_bx, lru_lambda], axis=1)
    wa_bd = _block_diag_all(lru_wa).astype(BF16)
    wx_bd = _block_diag_all(lru_wx).astype(BF16)
    mu_all = jnp.pad(rwkv_mu, ((0, 0), (0, pad_w)))[:, None, :]
    v0_all = jnp.pad(rwkv_v0, ((1, 0), (0, 0)))
    vec_all = jnp.stack([rwkv_w0, rwkv_a0, rwkv_kk, rwkv_ka, rwkv_rk.reshape(depth, C_WIDTH),
                         rwkv_gn_w, rwkv_gn_b, v0_all], axis=1)
    w2_all = jnp.pad(rwkv_w2, ((0, 0), (0, LANES - DECAY_RANK), (0, 0))).astype(BF16)
    a2_all = jnp.pad(rwkv_a2, ((0, 0), (DECAY_RANK, LANES - DECAY_RANK - AICL_RANK), (0, 0))).astype(BF16)
    v1_all = jnp.pad(rwkv_v1, ((1, 0), (0, 0), (0, LANES - VRES_RANK))).astype(BF16)
    v2_all = jnp.pad(rwkv_v2, ((1, 0), (0, LANES - VRES_RANK), (0, 0))).astype(BF16)
    consts = _rwkv_consts()

    tok_args = (mu_all, vec_all, w2_all, a2_all, consts[1])
    x2 = x.reshape(n, dm)
    pc_first = None
    pa, pb, pc = _proj(x2, w_in_p, tok_args, 0, seq)
    for l in range(depth):
        pc3 = pc.reshape(bsz, seq, PC_WIDTH)
        ya = _attn(pa.reshape(bsz, seq, GA_WIDTH), attn_sinks, l)
        yb = _lru(pb.reshape(bsz, seq, GB_WIDTH), conv_w, lru_vec, wa_bd, wx_bd, l)
        yc = _rwkv(pc3, pc_first, vec_all, v1_all, v2_all, consts, l)
        if l == 0:
            pc_first = pc3
        outs = _out(ya.reshape(n, A_WIDTH), yb.reshape(n, B_WIDTH), yc.reshape(n, C_WIDTH), x2,
                    w_out_b, ln_all, alpha, l, seq, w_in_p if l + 1 < depth else None, tok_args)
        x2 = outs[0]
        if l + 1 < depth:
            pa, pb, pc = outs[1:]
    return x2.reshape(bsz, seq, dm)
```

```python
import functools
import math

import jax
import jax.numpy as jnp
import numpy as np
from jax import lax
from jax.experimental import pallas as pl
from jax.experimental.pallas import tpu as pltpu

F32 = jnp.float32
BF16 = jnp.bfloat16

D_MODEL = 1024
HEAD_DIM = 64
A_Q_HEADS = 8
A_KV_HEADS = 2
A_WIDTH = A_Q_HEADS * HEAD_DIM
A_KV_WIDTH = A_KV_HEADS * HEAD_DIM
ATT_BLOCK = 128
B_WIDTH = 256
CONV_WIDTH = 4
LRU_C = 8.0
C_HEADS = 4
C_WIDTH = C_HEADS * HEAD_DIM
DECAY_RANK = 32
AICL_RANK = 32
VRES_RANK = 16
C_SHIFT_WIDTH = 3 * C_WIDTH + DECAY_RANK + AICL_RANK
GN_EPS = 64e-5
LN_EPS = 1e-5
LOG2E = math.log2(math.e)

LANES = 128
SUBLANES = 8
C_SHIFT_PAD = ((C_SHIFT_WIDTH + LANES - 1) // LANES) * LANES
GA_WIDTH = A_WIDTH + 2 * A_KV_WIDTH + A_WIDTH
GB_WIDTH = 2 * B_WIDTH
GC_WIDTH = C_SHIFT_WIDTH + C_WIDTH
D_IN_PAD = GA_WIDTH + GB_WIDTH + GC_WIDTH
D_MIX = A_WIDTH + B_WIDTH + C_WIDTH
PC_R, PC_K, PC_V, PC_KAP, PC_B, PC_LW, PC_G = range(7)
PC_WIDTH = 7 * C_WIDTH

CHUNK = 64
STACK = C_HEADS * CHUNK
RWKV_GROUP = 8
ATT_TILE = 1024
LRU_TILE = 256
LRU_STAGE_ROWS = SUBLANES * (LRU_TILE // SUBLANES + 4)
PROJ_TM = 1024
OUT_TM = 1024
FUSED_TM = 1024
OUT_SUB = 256
VMEM_LIMIT = 56 * 1024 * 1024


def _mm(a, b):
    return jnp.dot(a.astype(BF16), b.astype(BF16), preferred_element_type=F32)


def _mm_nt(a, b):
    return lax.dot_general(a.astype(BF16), b.astype(BF16), (((1,), (1,)), ((), ())),
                           preferred_element_type=F32)


def _mm_tn(a, b):
    return lax.dot_general(a.astype(BF16), b.astype(BF16), (((0,), (0,)), ((), ())),
                           preferred_element_type=F32)


def _split_bf16(x, parts):
    out = []
    rem = x
    for _ in range(parts):
        hi = rem.astype(BF16)
        out.append(hi)
        rem = rem - hi.astype(F32)
    return out


def _mm_exact_lhs(m_bf16, x, parts):
    n = x.shape[1]
    t = jnp.dot(m_bf16, jnp.concatenate(_split_bf16(x, parts), axis=1), preferred_element_type=F32)
    acc = t[:, 0:n]
    for p in range(1, parts):
        acc = acc + t[:, p * n:(p + 1) * n]
    return acc


def _sigmoid(x):
    return 1.0 / (1.0 + jnp.exp2(x * (-LOG2E)))


def _silu(x):
    return x * _sigmoid(x)


def _softplus(x):
    return jnp.maximum(x, 0.0) + jnp.log(1.0 + jnp.exp(-jnp.abs(x)))


def _layer_spec(shape, layer, single_buffer=False):
    index_map = lambda *_: (layer,) + (0,) * len(shape)
    if single_buffer:
        return pl.BlockSpec((None,) + shape, index_map, pipeline_mode=pl.Buffered(1))
    return pl.BlockSpec((None,) + shape, index_map)


def _const_spec(shape):
    return pl.BlockSpec(shape, lambda *_: (0,) * len(shape))


def _rwkv_tokenwise(cs, g, prev_row, mu, vec, w2p, a2p, ones_bd):
    w0, a0, k_k, k_a = (vec[i:i + 1, :] for i in range(4))
    row = lax.broadcasted_iota(jnp.int32, cs.shape, 0)
    prev = jnp.where(row == 0, prev_row, pltpu.roll(cs, 1, axis=0))
    xs = cs + (prev - cs) * mu
    r = xs[:, 0:C_WIDTH]
    k = xs[:, C_WIDTH:2 * C_WIDTH]
    v = xs[:, 2 * C_WIDTH:3 * C_WIDTH]
    la = xs[:, 3 * C_WIDTH:C_SHIFT_PAD]
    kk = k * k_k
    yield
    z_w = _mm(jnp.tanh(la), w2p)
    z_a = _mm(la, a2p)
    norm2 = _mm(kk * kk, ones_bd)
    yield
    lw = (-LOG2E * math.exp(-0.5)) * _sigmoid(w0 + z_w)
    a = _sigmoid(a0 + z_a)
    kap = kk * lax.rsqrt(norm2 + 1e-12)
    k2 = k * (1.0 + (a - 1.0) * k_a)
    return jnp.concatenate([r, k2, v, kap, kap * a, lw, g], axis=1), cs[cs.shape[0] - 1:, :]


def _drain(stage):
    if stage is not None:
        for _ in stage:
            pass


def _proj_rows(x16, w_ref, row0, nrows, seq, tok_refs, last_ref, oa_ref, ob_ref, oc_ref, pending):
    mu_ref, vec_ref, w2_ref, a2_ref, ones_ref = tok_refs
    nt = (((1,), (1,)), ((), ()))
    rows = pl.ds(row0, nrows)

    def advance():
        if pending is not None:
            next(pending, None)

    oa_ref[rows, :] = lax.dot_general(x16, w_ref[0:GA_WIDTH, :], nt,
                                      preferred_element_type=F32).astype(BF16)
    advance()
    ob_ref[rows, :] = lax.dot_general(x16, w_ref[GA_WIDTH:GA_WIDTH + GB_WIDTH, :], nt,
                                      preferred_element_type=F32)
    advance()
    c0 = GA_WIDTH + GB_WIDTH
    cs = lax.dot_general(x16, w_ref[c0:c0 + C_SHIFT_PAD, :], nt, preferred_element_type=F32)
    advance()
    g = lax.dot_general(x16, w_ref[c0 + C_SHIFT_WIDTH:D_IN_PAD, :], nt, preferred_element_type=F32)
    _drain(pending)
    first = pl.program_id(0) * oa_ref.shape[0] + row0

    def stage():
        prev_row = jnp.where(first % seq == 0, 0.0, last_ref[0:1, :])
        slab, last = yield from _rwkv_tokenwise(cs, g, prev_row, mu_ref[...], vec_ref[...],
                                                w2_ref[...], a2_ref[...], ones_ref[...])
        oc_ref[rows, :] = slab
        last_ref[0:1, :] = last

    return stage()


def _init_last(last_ref):
    @pl.when(pl.program_id(0) == 0)
    def _():
        last_ref[...] = jnp.zeros_like(last_ref)


def _proj_kernel(seq, x_ref, w_ref, mu_ref, vec_ref, w2_ref, a2_ref, ones_ref, oa_ref, ob_ref, oc_ref,
                 last_ref):
    _init_last(last_ref)
    pending = None
    for row0 in range(0, PROJ_TM, OUT_SUB):
        pending = _proj_rows(x_ref[pl.ds(row0, OUT_SUB), :].astype(BF16), w_ref, row0, OUT_SUB, seq,
                             (mu_ref, vec_ref, w2_ref, a2_ref, ones_ref), last_ref, oa_ref, ob_ref,
                             oc_ref, pending)
    _drain(pending)


def _tok_specs(layer):
    return [_layer_spec((1, C_SHIFT_PAD), layer), _layer_spec((8, C_WIDTH), layer),
            _layer_spec((LANES, C_WIDTH), layer), _layer_spec((LANES, C_WIDTH), layer),
            _const_spec((STACK, STACK))]


def _proj(x2, w_all, tok_args, layer, seq):
    n = x2.shape[0]
    return pl.pallas_call(
        functools.partial(_proj_kernel, seq),
        grid=(n // PROJ_TM,),
        in_specs=[pl.BlockSpec((PROJ_TM, D_MODEL), lambda i: (i, 0)),
                  _layer_spec((D_IN_PAD, D_MODEL), layer, single_buffer=True)] + _tok_specs(layer),
        out_specs=[pl.BlockSpec((PROJ_TM, GA_WIDTH), lambda i: (i, 0)),
                   pl.BlockSpec((PROJ_TM, GB_WIDTH), lambda i: (i, 0)),
                   pl.BlockSpec((PROJ_TM, PC_WIDTH), lambda i: (i, 0))],
        out_shape=[jax.ShapeDtypeStruct((n, GA_WIDTH), BF16),
                   jax.ShapeDtypeStruct((n, GB_WIDTH), F32),
                   jax.ShapeDtypeStruct((n, PC_WIDTH), F32)],
        scratch_shapes=[pltpu.VMEM((SUBLANES, C_SHIFT_PAD), F32)],
        compiler_params=pltpu.CompilerParams(dimension_semantics=("arbitrary",),
                                             vmem_limit_bytes=VMEM_LIMIT),
        name="proj",
    )(x2, w_all, *tok_args)


def _attn_kernel(layer, sink_ref, cur_ref, prev_ref, o_ref):
    n = pl.program_id(1)
    blk = ATT_BLOCK
    nsub = ATT_TILE // blk
    grp = A_Q_HEADS // A_KV_HEADS
    lo = lax.broadcasted_iota(jnp.int32, (blk, A_KV_WIDTH), 1) < HEAD_DIM

    def variants(x16):
        x = x16.astype(F32)
        xs = pltpu.roll(x, HEAD_DIM, axis=1)
        return ((jnp.where(lo, x, 0.0).astype(BF16), jnp.where(lo, 0.0, xs).astype(BF16)),
                (jnp.where(lo, xs, 0.0).astype(BF16), jnp.where(lo, 0.0, x).astype(BF16)))

    k0, v0 = A_WIDTH, A_WIDTH + A_KV_WIDTH
    g0 = A_WIDTH + 2 * A_KV_WIDTH
    kblk = [variants(prev_ref[0, :, 0:A_KV_WIDTH])]
    vblk = [variants(prev_ref[0, :, A_KV_WIDTH:2 * A_KV_WIDTH])]
    for j in range(nsub):
        kblk.append(variants(cur_ref[0, j * blk:(j + 1) * blk, k0:k0 + A_KV_WIDTH]))
        vblk.append(variants(cur_ref[0, j * blk:(j + 1) * blk, v0:v0 + A_KV_WIDTH]))

    qi = lax.broadcasted_iota(jnp.int32, (blk, 2 * blk), 0)
    kj = lax.broadcasted_iota(jnp.int32, (blk, 2 * blk), 1)
    diff = qi + blk - kj
    band = (diff >= 0) & (diff < blk)
    first = band & (kj + (n * nsub - 1) * blk >= 0)

    for j in range(nsub):
        mask = first if j == 0 else band
        rows = slice(j * blk, (j + 1) * blk)
        scores = []
        for p in range(A_Q_HEADS // 2):
            h = (2 * p) // grp
            qp = cur_ref[0, rows, p * LANES:(p + 1) * LANES]
            for half in range(2):
                kb = jnp.concatenate([kblk[j][h][half], kblk[j + 1][h][half]], axis=0)
                scores.append(lax.dot_general(qp, kb, (((1,), (1,)), ((), ())),
                                              preferred_element_type=F32))
        probs, inv = [], []
        for idx, s in enumerate(scores):
            sink2 = sink_ref[layer, idx] * LOG2E
            s = jnp.where(mask, s, -jnp.inf)
            m = jnp.maximum(jnp.max(s, axis=-1, keepdims=True), sink2)
            e = jnp.exp2(s - m)
            den = jnp.sum(e, axis=-1, keepdims=True) + jnp.exp2(sink2 - m)
            probs.append(e.astype(BF16))
            inv.append(1.0 / den)
        for p in range(A_Q_HEADS // 2):
            h = (2 * p) // grp
            acc = None
            for half in range(2):
                vb = jnp.concatenate([vblk[j][h][half], vblk[j + 1][h][half]], axis=0)
                o = jnp.dot(probs[2 * p + half], vb, preferred_element_type=F32) * inv[2 * p + half]
                acc = o if acc is None else acc + o
            g = cur_ref[0, rows, g0 + p * LANES:g0 + (p + 1) * LANES].astype(F32)
            o_ref[0, rows, p * LANES:(p + 1) * LANES] = (acc * _silu(g)).astype(BF16)


def _attn(pa, sinks, layer):
    bsz, seq, _ = pa.shape
    nsub = ATT_TILE // ATT_BLOCK
    kv_blk = A_WIDTH // (2 * A_KV_WIDTH)
    return pl.pallas_call(
        functools.partial(_attn_kernel, layer),
        grid=(bsz, seq // ATT_TILE),
        in_specs=[pl.BlockSpec(memory_space=pltpu.SMEM),
                  pl.BlockSpec((1, ATT_TILE, GA_WIDTH), lambda b, n: (b, n, 0)),
                  pl.BlockSpec((1, ATT_BLOCK, 2 * A_KV_WIDTH),
                               lambda b, n: (b, jnp.maximum(n * nsub - 1, 0), kv_blk))],
        out_specs=pl.BlockSpec((1, ATT_TILE, A_WIDTH), lambda b, n: (b, n, 0)),
        out_shape=jax.ShapeDtypeStruct((bsz, seq, A_WIDTH), BF16),
        compiler_params=pltpu.CompilerParams(dimension_semantics=("arbitrary", "arbitrary"),
                                             vmem_limit_bytes=VMEM_LIMIT),
        name="attn",
    )(sinks, pa, pa)


def _lru_kernel(pb_ref, cw_ref, vec_ref, wa_ref, wx_ref, o_ref, in_ref, stage_ref):
    seq = pb_ref.shape[1]
    tile = LRU_TILE
    nseg = SUBLANES
    nstep = tile // nseg
    conv_b = vec_ref[0:1, :]
    ba = vec_ref[1:2, :]
    bx = vec_ref[2:3, :]
    lam = vec_ref[3:4, :]
    neg_c_sp = -LRU_C * _softplus(-lam)
    cw = [cw_ref[j:j + 1, :] for j in range(CONV_WIDTH)]
    sub0 = lax.broadcasted_iota(jnp.int32, (nseg, B_WIDTH), 0) == 0

    nhalf = B_WIDTH // LANES

    pitch = nstep + 4
    assert pitch % SUBLANES == 4

    def rows_of(step):
        return pl.ds(step, nseg, stride=pitch)

    def permuted(k0, step):
        return jnp.concatenate([in_ref[k0 + k, rows_of(step), :] for k in range(nhalf)], axis=1)

    hc = jnp.zeros((1, B_WIDTH), F32)
    tail = [jnp.zeros((1, B_WIDTH), F32)] * (CONV_WIDTH - 1)
    for ti in range(seq // tile):
        t0 = ti * tile
        for k in range(2 * nhalf):
            for sgi in range(nseg):
                in_ref[k, sgi * pitch:sgi * pitch + nstep, :] = pb_ref[
                    0, t0 + sgi * nstep:t0 + (sgi + 1) * nstep, k * LANES:(k + 1) * LANES]
        xs = [permuted(0, v) for v in range(nstep)]
        wrapped = []
        for i in range(CONV_WIDTH - 1):
            src_v = xs[nstep - (CONV_WIDTH - 1) + i]
            wrapped.append(jnp.where(sub0, tail[i], pltpu.roll(src_v, 1, axis=0)))
        x_all = jnp.concatenate(xs, axis=0)
        xc = cw[CONV_WIDTH - 1] * x_all + conv_b
        for j in range(1, CONV_WIDTH):
            shifted = jnp.concatenate(wrapped[CONV_WIDTH - 1 - j:] + xs[:nstep - j], axis=0)
            xc = xc + cw[CONV_WIDTH - 1 - j] * shifted
        tail = [xs[nstep - (CONV_WIDTH - 1) + i][nseg - 1:nseg, :] for i in range(CONV_WIDTH - 1)]

        r = _sigmoid(_mm(xc, wa_ref[...]) + ba)
        ig = _sigmoid(_mm(xc, wx_ref[...]) + bx)
        a = jnp.exp(neg_c_sp * r)
        u = jnp.sqrt(1.0 - a * a) * (ig * xc)
        hl = [u[0:nseg]]
        al = [a[0:nseg]]
        for v in range(1, nstep):
            av = a[v * nseg:(v + 1) * nseg]
            hl.append(av * hl[-1] + u[v * nseg:(v + 1) * nseg])
            al.append(av * al[-1])
        carry = hc
        enter = []
        for sgi in range(nseg):
            enter.append(carry)
            carry = hl[-1][sgi:sgi + 1, :] + al[-1][sgi:sgi + 1, :] * carry
        hc = carry
        enter = jnp.concatenate(enter, axis=0)
        for v in range(nstep):
            out = (hl[v] + al[v] * enter) * _silu(permuted(nhalf, v))
            for k in range(nhalf):
                stage_ref[k, rows_of(v), :] = out[:, k * LANES:(k + 1) * LANES]
        for k in range(nhalf):
            for sgi in range(nseg):
                o_ref[0, t0 + sgi * nstep:t0 + (sgi + 1) * nstep, k * LANES:(k + 1) * LANES] = (
                    stage_ref[k, sgi * pitch:sgi * pitch + nstep, :].astype(BF16))


def _lru(pb, cw_all, vec_all, wa_all, wx_all, layer):
    bsz, seq, _ = pb.shape
    return pl.pallas_call(
        _lru_kernel,
        grid=(bsz,),
        in_specs=[pl.BlockSpec((1, seq, GB_WIDTH), lambda b: (b, 0, 0)),
                  _layer_spec((CONV_WIDTH, B_WIDTH), layer), _layer_spec((4, B_WIDTH), layer),
                  _layer_spec((B_WIDTH, B_WIDTH), layer), _layer_spec((B_WIDTH, B_WIDTH), layer)],
        out_specs=pl.BlockSpec((1, seq, B_WIDTH), lambda b: (b, 0, 0)),
        out_shape=jax.ShapeDtypeStruct((bsz, seq, B_WIDTH), BF16),
        scratch_shapes=[pltpu.VMEM((GB_WIDTH // LANES, LRU_STAGE_ROWS, LANES), F32),
                        pltpu.VMEM((B_WIDTH // LANES, LRU_STAGE_ROWS, LANES), F32)],
        compiler_params=pltpu.CompilerParams(dimension_semantics=("arbitrary",),
                                             vmem_limit_bytes=VMEM_LIMIT),
        name="lru",
    )(pb, cw_all, vec_all, wa_all, wx_all)


def _stack_heads(x):
    x16 = x.astype(BF16)
    head = lax.broadcasted_iota(jnp.int32, x16.shape, 1) // HEAD_DIM
    return jnp.concatenate([jnp.where(head == h, x16, jnp.zeros_like(x16)) for h in range(C_HEADS)],
                           axis=0)


def _unstack_heads(x):
    head = lax.broadcasted_iota(jnp.int32, (CHUNK, STACK), 1) // HEAD_DIM
    out = x[(C_HEADS - 1) * CHUNK:C_HEADS * CHUNK]
    for h in range(C_HEADS - 2, -1, -1):
        out = jnp.where(head == h, x[h * CHUNK:(h + 1) * CHUNK], out)
    return out


def _rwkv_pre(slab, vfirst, vec, v1p, v2p, tril_l, sl, le, eye_w):
    r, k2, v, kap, b, lw = (slab[:, i * C_WIDTH:(i + 1) * C_WIDTH]
                            for i in (PC_R, PC_K, PC_V, PC_KAP, PC_B, PC_LW))
    r_k = vec[4:5, :]
    v0 = vec[7:8, :]
    if vfirst is not None:
        v = v + (vfirst - v) * _sigmoid(v0 + _mm(_mm(v, v1p), v2p))
    yield
    cum = _mm_exact_lhs(tril_l, lw, 2)
    cum_end = cum[CHUNK - 1:CHUNK, :]
    e_in = jnp.exp2(cum)
    e_ex = jnp.exp2(cum - lw)
    e_neg = jnp.exp2(-cum)
    e_end = jnp.exp2(cum_end - cum)
    p_end = jnp.exp2(cum_end)

    yield
    kt = kap * e_ex
    rt = r * e_in
    kt_st = _stack_heads(kt)
    vs_st = _stack_heads(v)
    gram = _mm_nt(jnp.concatenate([kt, rt], axis=0),
                  jnp.concatenate([_stack_heads(b * e_neg), _stack_heads(k2 * e_neg)], axis=0))
    a_ab = jnp.where(sl, gram[0:CHUNK, 0:STACK], 0.0)
    a_ak = jnp.where(sl, gram[0:CHUNK, STACK:2 * STACK], 0.0)
    a_rb = jnp.where(le, gram[CHUNK:2 * CHUNK, 0:STACK], 0.0)
    a_rk = jnp.where(le, gram[CHUNK:2 * CHUNK, STACK:2 * STACK], 0.0)

    yield
    pw = -a_ab
    t_w = eye_w + pw
    av = _mm(jnp.concatenate([a_ak, a_rk], axis=0), vs_st)
    pw = _mm(pw, _stack_heads(pw))
    for _ in range(1, int(math.log2(CHUNK)) - 1):
        yield
        both = _mm(jnp.concatenate([t_w, pw], axis=0), _stack_heads(pw))
        t_w = t_w + both[0:CHUNK]
        pw = both[CHUNK:2 * CHUNK]
    yield
    t_w = t_w + _mm(t_w, _stack_heads(pw))
    yield
    tx = _mm(t_w, jnp.concatenate([kt_st, _stack_heads(av[0:CHUNK])], axis=1))
    yield
    m1 = tx[:, 0:STACK]
    c1 = tx[:, STACK:2 * STACK]
    ab = _mm(a_rb, jnp.concatenate([_stack_heads(m1), _stack_heads(c1)], axis=1))
    q_n = rt - ab[:, 0:STACK]
    d_n = av[CHUNK:2 * CHUNK] - ab[:, STACK:2 * STACK]
    bh = b * e_end
    g_w = eye_w * p_end - _unstack_heads(_mm_tn(bh, m1))
    c_w = _unstack_heads(_mm_tn(jnp.concatenate([k2 * e_end, bh], axis=0),
                                jnp.concatenate([v, -c1], axis=0)))
    bonus_arg = r * k2 * r_k
    return q_n, d_n, g_w, c_w, bonus_arg, v, slab[:, PC_G * C_WIDTH:PC_WIDTH]


RWKV_SAVED = 7


def _rwkv_kernel(use_vres, *refs):
    if use_vres:
        (pc_ref, vf_ref, vec_ref, v1_ref, v2_ref, tril_ref, bd_ref,
         sl_ref, le_ref, eye_ref, y_ref, h_ref, pre_ref, ystage_ref) = refs
    else:
        (pc_ref, vec_ref, v1_ref, v2_ref, tril_ref, bd_ref,
         sl_ref, le_ref, eye_ref, y_ref, h_ref, pre_ref, ystage_ref) = refs
        vf_ref = None
    seq = pc_ref.shape[1]
    ngroups = seq // (CHUNK * RWKV_GROUP)
    step_id = pl.program_id(0)
    last_step = pl.num_programs(0) - 1

    @pl.when(step_id == 0)
    def _():
        h_ref[...] = jnp.zeros_like(h_ref)
        pre_ref[...] = jnp.zeros_like(pre_ref)
        ystage_ref[...] = jnp.zeros_like(ystage_ref)

    def chunk_start(gi, j):
        return pl.multiple_of((gi * RWKV_GROUP + j) * CHUNK, CHUNK)

    def pre_generators(gi):
        sl = sl_ref[...] > 0.0
        le = le_ref[...] > 0.0
        gens = []
        for j in range(RWKV_GROUP):
            t0 = chunk_start(gi, j)
            slab = pc_ref[0, pl.ds(t0, CHUNK), :]
            vfirst = vf_ref[0, pl.ds(t0, CHUNK), :] if use_vres else None
            gens.append(_rwkv_pre(slab, vfirst, vec_ref[...], v1_ref[...], v2_ref[...], tril_ref[...],
                                  sl, le, eye_ref[...]))
        return gens

    def chain_tasks(gi):
        ys = [None] * RWKV_GROUP
        stats = [None] * RWKV_GROUP
        state = {}

        def step(j):
            def run():
                h_w = h_ref[...] if j == 0 else state["h"]
                q_n, d_n, g_w, c_w = (pre_ref[j * RWKV_SAVED + k] for k in range(4))
                res = _mm(jnp.concatenate([q_n, g_w], axis=0), _stack_heads(h_w))
                ys[j] = res[0:CHUNK] + d_n
                state["h"] = res[CHUNK:2 * CHUNK] + c_w
                if j == RWKV_GROUP - 1:
                    h_ref[...] = state["h"]
            return run

        def sums():
            for j in range(RWKV_GROUP):
                parts = _split_bf16(ys[j], 2) + [pre_ref[j * RWKV_SAVED + 4].astype(BF16)]
                s = jnp.dot(jnp.concatenate(parts, axis=0), bd_ref[...], preferred_element_type=F32)
                yc = ys[j] - (s[0:CHUNK] + s[CHUNK:2 * CHUNK]) * (1.0 / HEAD_DIM)
                stats[j] = (yc, s[2 * CHUNK:3 * CHUNK])

        def variance():
            for j in range(RWKV_GROUP):
                yc, bonus = stats[j]
                yv = _mm(yc * yc, bd_ref[...]) * (1.0 / HEAD_DIM)
                stats[j] = (yc, bonus, yv)

        def finish():
            gn_w = vec_ref[5:6, :]
            gn_b = vec_ref[6:7, :]
            for j in range(RWKV_GROUP):
                t0 = chunk_start(gi, j)
                yc, bonus, yv = stats[j]
                yn = yc * lax.rsqrt(yv + GN_EPS) * gn_w + gn_b
                out = (yn + bonus * pre_ref[j * RWKV_SAVED + 5]) * _silu(pre_ref[j * RWKV_SAVED + 6])
                ystage_ref[pl.ds(t0, CHUNK), :] = out.astype(BF16)

        return [step(j) for j in range(RWKV_GROUP)] + [sums, variance, finish]

    def run_group(gi, tasks):
        gens = pre_generators(gi)
        tasks = list(tasks)
        pre = [None] * RWKV_GROUP
        sweep = 0
        while any(p is None for p in pre):
            for j, gen in enumerate(gens):
                if pre[j] is None:
                    try:
                        next(gen)
                    except StopIteration as stop:
                        pre[j] = stop.value
            sweep += 1
            if tasks and sweep >= 2:
                tasks.pop(0)()
        for t in tasks:
            t()
        for j in range(RWKV_GROUP):
            for k in range(RWKV_SAVED):
                pre_ref[j * RWKV_SAVED + k] = pre[j][k]

    def flush():
        y_ref[0] = ystage_ref[...]
        h_ref[...] = jnp.zeros_like(h_ref)

    @pl.when(step_id < last_step)
    def _():
        run_group(0, chain_tasks(ngroups - 1) + [flush])

        def body(i, carry):
            run_group(i, chain_tasks(i - 1))
            return carry

        lax.fori_loop(1, ngroups, body, 0)

    @pl.when(step_id == last_step)
    def _():
        for t in chain_tasks(ngroups - 1) + [flush]:
            t()


def _rwkv_consts():
    idx = np.arange(STACK)
    same = (idx[:, None] // CHUNK) == (idx[None, :] // CHUNK)
    ci = np.arange(CHUNK)
    tril_l = ci[None, :] <= ci[:, None]
    s_loc = idx[None, :] % CHUNK
    sl = s_loc < ci[:, None]
    le = s_loc <= ci[:, None]
    eye_w = s_loc == ci[:, None]
    f32 = lambda m: jnp.asarray(m.astype(np.float32))
    return f32(tril_l).astype(BF16), f32(same).astype(BF16), f32(sl), f32(le), f32(eye_w)


def _rwkv(pc, pc_first, vec_all, v1_all, v2_all, consts, layer):
    bsz, seq, _ = pc.shape
    use_vres = pc_first is not None
    sq = (STACK, STACK)
    wide = (CHUNK, STACK)
    cur = lambda b: jnp.minimum(b, bsz - 1)
    in_specs = [pl.BlockSpec((1, seq, PC_WIDTH), lambda b: (cur(b), 0, 0))]
    if use_vres:
        in_specs.append(pl.BlockSpec((1, seq, C_WIDTH), lambda b: (cur(b), 0, PC_V)))
    in_specs += [_layer_spec((8, C_WIDTH), layer),
                 _layer_spec((C_WIDTH, LANES), layer), _layer_spec((LANES, C_WIDTH), layer),
                 _const_spec((CHUNK, CHUNK)), _const_spec(sq),
                 _const_spec(wide), _const_spec(wide), _const_spec(wide)]
    args = (pc,) + ((pc_first,) if use_vres else ()) + (vec_all, v1_all, v2_all)
    return pl.pallas_call(
        functools.partial(_rwkv_kernel, use_vres),
        grid=(bsz + 1,),
        in_specs=in_specs,
        out_specs=pl.BlockSpec((1, seq, C_WIDTH), lambda b: (jnp.maximum(b - 1, 0), 0, 0)),
        out_shape=jax.ShapeDtypeStruct((bsz, seq, C_WIDTH), BF16),
        scratch_shapes=[pltpu.VMEM(wide, F32), pltpu.VMEM((RWKV_GROUP * RWKV_SAVED,) + wide, F32),
                        pltpu.VMEM((seq, C_WIDTH), BF16)],
        compiler_params=pltpu.CompilerParams(dimension_semantics=("arbitrary",),
                                             vmem_limit_bytes=VMEM_LIMIT),
        name="rwkv",
    )(*args, *consts)


def _out_kernel(alpha, tm, seq, fused, ya_ref, yb_ref, yc_ref, x_ref, w_ref, ln_ref, *rest):
    if fused:
        win_ref, *tok_refs = rest[0:6]
        o_ref, oa_ref, ob_ref, oc_ref, last_ref = rest[6:]
        _init_last(last_ref)
    else:
        (o_ref,) = rest

    def project(rows):
        acc = jnp.dot(ya_ref[rows, :], w_ref[0:A_WIDTH, :], preferred_element_type=F32)
        acc = acc + jnp.dot(yb_ref[rows, :], w_ref[A_WIDTH:A_WIDTH + B_WIDTH, :],
                            preferred_element_type=F32)
        return acc + jnp.dot(yc_ref[rows, :], w_ref[A_WIDTH + B_WIDTH:D_MIX, :],
                             preferred_element_type=F32)

    def norm(rows, acc, pending):
        z = alpha * x_ref[rows, :] + acc
        mu = jnp.mean(z, axis=-1, keepdims=True)
        zc = z - mu
        var = jnp.mean(zc * zc, axis=-1, keepdims=True)
        xn = zc * lax.rsqrt(var + LN_EPS) * ln_ref[0:1, :] + ln_ref[1:2, :]
        o_ref[rows, :] = xn
        if not fused:
            return None
        return _proj_rows(xn.astype(BF16), win_ref, rows.start, OUT_SUB, seq, tok_refs, last_ref,
                          oa_ref, ob_ref, oc_ref, pending)

    subs = [slice(s * OUT_SUB, (s + 1) * OUT_SUB) for s in range(tm // OUT_SUB)]
    acc_prev = project(subs[0])
    pending = None
    for s in range(len(subs)):
        acc_next = project(subs[s + 1]) if s + 1 < len(subs) else None
        pending = norm(subs[s], acc_prev, pending)
        acc_prev = acc_next
    _drain(pending)


def _out(ya, yb, yc, x2, w_all, ln_all, alpha, layer, seq, w_in_all=None, tok_args=()):
    n = x2.shape[0]
    fused = w_in_all is not None
    tm = FUSED_TM if fused else OUT_TM
    row = lambda w: pl.BlockSpec((tm, w), lambda i: (i, 0))
    in_specs = [row(A_WIDTH), row(B_WIDTH), row(C_WIDTH), row(D_MODEL),
                _layer_spec((D_MIX, D_MODEL), layer, single_buffer=True),
                _layer_spec((2, D_MODEL), layer)]
    out_specs = [row(D_MODEL)]
    out_shape = [jax.ShapeDtypeStruct((n, D_MODEL), F32)]
    args = [ya, yb, yc, x2, w_all, ln_all]
    if fused:
        in_specs.append(_layer_spec((D_IN_PAD, D_MODEL), layer + 1, single_buffer=True))
        in_specs += _tok_specs(layer + 1)
        out_specs += [row(GA_WIDTH), row(GB_WIDTH), row(PC_WIDTH)]
        out_shape += [jax.ShapeDtypeStruct((n, GA_WIDTH), BF16),
                      jax.ShapeDtypeStruct((n, GB_WIDTH), F32),
                      jax.ShapeDtypeStruct((n, PC_WIDTH), F32)]
        args += [w_in_all, *tok_args]
    return pl.pallas_call(
        functools.partial(_out_kernel, alpha, tm, seq, fused),
        grid=(n // tm,),
        in_specs=in_specs,
        out_specs=out_specs,
        out_shape=out_shape,
        scratch_shapes=[pltpu.VMEM((SUBLANES, C_SHIFT_PAD), F32)] if fused else [],
        compiler_params=pltpu.CompilerParams(dimension_semantics=("arbitrary",),
                                             vmem_limit_bytes=VMEM_LIMIT),
        name="outproj_proj" if fused else "outproj",
    )(*args)


def _block_diag_all(w):
    depth, nblk, d, _ = w.shape
    eye = jnp.eye(nblk, dtype=w.dtype)
    return (w[:, :, :, None, :] * eye[None, :, None, :, None]).reshape(depth, nblk * d, nblk * d)


def kernel(x, w_in, w_out, ln_g, ln_b, attn_sinks, conv_w, conv_b, lru_wa, lru_ba, lru_wx, lru_bx,
           lru_lambda, rwkv_mu, rwkv_w0, rwkv_w2, rwkv_a0, rwkv_a2, rwkv_kk, rwkv_ka, rwkv_rk,
           rwkv_gn_w, rwkv_gn_b, rwkv_v0, rwkv_v1, rwkv_v2):
    bsz, seq, dm = x.shape
    depth = w_in.shape[0]
    alpha = (2 * depth) ** 0.25
    n = bsz * seq
    pad_w = C_SHIFT_PAD - C_SHIFT_WIDTH

    q_scale = np.ones((w_in.shape[2],), np.float32)
    q_scale[:A_WIDTH] = HEAD_DIM ** -0.5 * LOG2E
    w_in_p = (jnp.swapaxes(w_in, 1, 2) * q_scale[None, :, None]).astype(BF16)
    w_out_b = w_out.astype(BF16)
    ln_all = jnp.stack([ln_g, ln_b], axis=1)
    lru_vec = jnp.stack([conv_b, lru_ba, lru_bx, lru_lambda], axis=1)
    wa_bd = _block_diag_all(lru_wa).astype(BF16)
    wx_bd = _block_diag_all(lru_wx).astype(BF16)
    mu_all = jnp.pad(rwkv_mu, ((0, 0), (0, pad_w)))[:, None, :]
    v0_all = jnp.pad(rwkv_v0, ((1, 0), (0, 0)))
    vec_all = jnp.stack([rwkv_w0, rwkv_a0, rwkv_kk, rwkv_ka, rwkv_rk.reshape(depth, C_WIDTH),
                         rwkv_gn_w, rwkv_gn_b, v0_all], axis=1)
    w2_all = jnp.pad(rwkv_w2, ((0, 0), (0, LANES - DECAY_RANK), (0, 0))).astype(BF16)
    a2_all = jnp.pad(rwkv_a2, ((0, 0), (DECAY_RANK, LANES - DECAY_RANK - AICL_RANK), (0, 0))).astype(BF16)
    v1_all = jnp.pad(rwkv_v1, ((1, 0), (0, 0), (0, LANES - VRES_RANK))).astype(BF16)
    v2_all = jnp.pad(rwkv_v2, ((1, 0), (0, LANES - VRES_RANK), (0, 0))).astype(BF16)
    consts = _rwkv_consts()

    tok_args = (mu_all, vec_all, w2_all, a2_all, consts[1])
    x2 = x.reshape(n, dm)
    pc_first = None
    pa, pb, pc = _proj(x2, w_in_p, tok_args, 0, seq)
    for l in range(depth):
        pc3 = pc.reshape(bsz, seq, PC_WIDTH)
        ya = _attn(pa.reshape(bsz, seq, GA_WIDTH), attn_sinks, l)
        yb = _lru(pb.reshape(bsz, seq, GB_WIDTH), conv_w, lru_vec, wa_bd, wx_bd, l)
        yc = _rwkv(pc3, pc_first, vec_all, v1_all, v2_all, consts, l)
        if l == 0:
            pc_first = pc3
        outs = _out(ya.reshape(n, A_WIDTH), yb.reshape(n, B_WIDTH), yc.reshape(n, C_WIDTH), x2,
                    w_out_b, ln_all, alpha, l, seq, w_in_p if l + 1 < depth else None, tok_args)
        x2 = outs[0]
        if l + 1 < depth:
            pa, pb, pc = outs[1:]
    return x2.reshape(bsz, seq, dm)
```

```python
import functools
import math

import jax
import jax.numpy as jnp
import numpy as np
from jax import lax
from jax.experimental import pallas as pl
from jax.experimental.pallas import tpu as pltpu

F32 = jnp.float32
BF16 = jnp.bfloat16

D_MODEL = 1024
HEAD_DIM = 64
A_Q_HEADS = 8
A_KV_HEADS = 2
A_WIDTH = A_Q_HEADS * HEAD_DIM
A_KV_WIDTH = A_KV_HEADS * HEAD_DIM
ATT_BLOCK = 128
B_WIDTH = 256
CONV_WIDTH = 4
LRU_C = 8.0
C_HEADS = 4
C_WIDTH = C_HEADS * HEAD_DIM
DECAY_RANK = 32
AICL_RANK = 32
VRES_RANK = 16
C_SHIFT_WIDTH = 3 * C_WIDTH + DECAY_RANK + AICL_RANK
GN_EPS = 64e-5
LN_EPS = 1e-5
LOG2E = math.log2(math.e)

LANES = 128
SUBLANES = 8
C_SHIFT_PAD = ((C_SHIFT_WIDTH + LANES - 1) // LANES) * LANES
GA_WIDTH = A_WIDTH + 2 * A_KV_WIDTH + A_WIDTH
GB_WIDTH = 2 * B_WIDTH
GC_WIDTH = C_SHIFT_WIDTH + C_WIDTH
D_IN_PAD = GA_WIDTH + GB_WIDTH + GC_WIDTH
D_MIX = A_WIDTH + B_WIDTH + C_WIDTH
PC_R, PC_K, PC_V, PC_KAP, PC_B, PC_LW, PC_G = range(7)
PC_WIDTH = 7 * C_WIDTH

CHUNK = 64
STACK = C_HEADS * CHUNK
RWKV_GROUP = 8
ATT_TILE = 2048
LRU_TILE = 256
LRU_STAGE_ROWS = SUBLANES * (LRU_TILE // SUBLANES + 4)
PROJ_TM = 1024
OUT_TM = 1024
FUSED_TM = 1024
OUT_SUB = 256
VMEM_LIMIT = 56 * 1024 * 1024


def _mm(a, b):
    return jnp.dot(a.astype(BF16), b.astype(BF16), preferred_element_type=F32)


def _mm_nt(a, b):
    return lax.dot_general(a.astype(BF16), b.astype(BF16), (((1,), (1,)), ((), ())),
                           preferred_element_type=F32)


def _mm_tn(a, b):
    return lax.dot_general(a.astype(BF16), b.astype(BF16), (((0,), (0,)), ((), ())),
                           preferred_element_type=F32)


def _split_bf16(x, parts):
    out = []
    rem = x
    for _ in range(parts):
        hi = rem.astype(BF16)
        out.append(hi)
        rem = rem - hi.astype(F32)
    return out


def _mm_exact_lhs(m_bf16, x, parts):
    n = x.shape[1]
    t = jnp.dot(m_bf16, jnp.concatenate(_split_bf16(x, parts), axis=1), preferred_element_type=F32)
    acc = t[:, 0:n]
    for p in range(1, parts):
        acc = acc + t[:, p * n:(p + 1) * n]
    return acc


def _sigmoid(x):
    return 1.0 / (1.0 + jnp.exp2(x * (-LOG2E)))


def _silu(x):
    return x * _sigmoid(x)


def _softplus(x):
    return jnp.maximum(x, 0.0) + jnp.log(1.0 + jnp.exp(-jnp.abs(x)))


def _layer_spec(shape, layer, single_buffer=False):
    index_map = lambda *_: (layer,) + (0,) * len(shape)
    if single_buffer:
        return pl.BlockSpec((None,) + shape, index_map, pipeline_mode=pl.Buffered(1))
    return pl.BlockSpec((None,) + shape, index_map)


def _const_spec(shape):
    return pl.BlockSpec(shape, lambda *_: (0,) * len(shape))


def _rwkv_tokenwise(cs, g, prev_row, mu, vec, w2p, a2p, ones_bd):
    w0, a0, k_k, k_a = (vec[i:i + 1, :] for i in range(4))
    row = lax.broadcasted_iota(jnp.int32, cs.shape, 0)
    prev = jnp.where(row == 0, prev_row, pltpu.roll(cs, 1, axis=0))
    xs = cs + (prev - cs) * mu
    r = xs[:, 0:C_WIDTH]
    k = xs[:, C_WIDTH:2 * C_WIDTH]
    v = xs[:, 2 * C_WIDTH:3 * C_WIDTH]
    la = xs[:, 3 * C_WIDTH:C_SHIFT_PAD]
    kk = k * k_k
    yield
    z_w = _mm(jnp.tanh(la), w2p)
    z_a = _mm(la, a2p)
    norm2 = _mm(kk * kk, ones_bd)
    yield
    lw = (-LOG2E * math.exp(-0.5)) * _sigmoid(w0 + z_w)
    a = _sigmoid(a0 + z_a)
    kap = kk * lax.rsqrt(norm2 + 1e-12)
    k2 = k * (1.0 + (a - 1.0) * k_a)
    return jnp.concatenate([r, k2, v, kap, kap * a, lw, g], axis=1), cs[cs.shape[0] - 1:, :]


def _drain(stage):
    if stage is not None:
        for _ in stage:
            pass


def _proj_rows(x16, w_ref, row0, nrows, seq, tok_refs, last_ref, oa_ref, ob_ref, oc_ref, pending):
    mu_ref, vec_ref, w2_ref, a2_ref, ones_ref = tok_refs
    nt = (((1,), (1,)), ((), ()))
    rows = pl.ds(row0, nrows)

    def advance():
        if pending is not None:
            next(pending, None)

    oa_ref[rows, :] = lax.dot_general(x16, w_ref[0:GA_WIDTH, :], nt,
                                      preferred_element_type=F32).astype(BF16)
    advance()
    ob_ref[rows, :] = lax.dot_general(x16, w_ref[GA_WIDTH:GA_WIDTH + GB_WIDTH, :], nt,
                                      preferred_element_type=F32)
    advance()
    c0 = GA_WIDTH + GB_WIDTH
    cs = lax.dot_general(x16, w_ref[c0:c0 + C_SHIFT_PAD, :], nt, preferred_element_type=F32)
    advance()
    g = lax.dot_general(x16, w_ref[c0 + C_SHIFT_WIDTH:D_IN_PAD, :], nt, preferred_element_type=F32)
    _drain(pending)
    first = pl.program_id(0) * oa_ref.shape[0] + row0

    def stage():
        prev_row = jnp.where(first % seq == 0, 0.0, last_ref[0:1, :])
        slab, last = yield from _rwkv_tokenwise(cs, g, prev_row, mu_ref[...], vec_ref[...],
                                                w2_ref[...], a2_ref[...], ones_ref[...])
        oc_ref[rows, :] = slab
        last_ref[0:1, :] = last

    return stage()


def _init_last(last_ref):
    @pl.when(pl.program_id(0) == 0)
    def _():
        last_ref[...] = jnp.zeros_like(last_ref)


def _proj_kernel(seq, x_ref, w_ref, mu_ref, vec_ref, w2_ref, a2_ref, ones_ref, oa_ref, ob_ref, oc_ref,
                 last_ref):
    _init_last(last_ref)
    pending = None
    for row0 in range(0, PROJ_TM, OUT_SUB):
        pending = _proj_rows(x_ref[pl.ds(row0, OUT_SUB), :].astype(BF16), w_ref, row0, OUT_SUB, seq,
                             (mu_ref, vec_ref, w2_ref, a2_ref, ones_ref), last_ref, oa_ref, ob_ref,
                             oc_ref, pending)
    _drain(pending)


def _tok_specs(layer):
    return [_layer_spec((1, C_SHIFT_PAD), layer), _layer_spec((8, C_WIDTH), layer),
            _layer_spec((LANES, C_WIDTH), layer), _layer_spec((LANES, C_WIDTH), layer),
            _const_spec((STACK, STACK))]


def _proj(x2, w_all, tok_args, layer, seq):
    n = x2.shape[0]
    return pl.pallas_call(
        functools.partial(_proj_kernel, seq),
        grid=(n // PROJ_TM,),
        in_specs=[pl.BlockSpec((PROJ_TM, D_MODEL), lambda i: (i, 0)),
                  _layer_spec((D_IN_PAD, D_MODEL), layer, single_buffer=True)] + _tok_specs(layer),
        out_specs=[pl.BlockSpec((PROJ_TM, GA_WIDTH), lambda i: (i, 0)),
                   pl.BlockSpec((PROJ_TM, GB_WIDTH), lambda i: (i, 0)),
                   pl.BlockSpec((PROJ_TM, PC_WIDTH), lambda i: (i, 0))],
        out_shape=[jax.ShapeDtypeStruct((n, GA_WIDTH), BF16),
                   jax.ShapeDtypeStruct((n, GB_WIDTH), F32),
                   jax.ShapeDtypeStruct((n, PC_WIDTH), F32)],
        scratch_shapes=[pltpu.VMEM((SUBLANES, C_SHIFT_PAD), F32)],
        compiler_params=pltpu.CompilerParams(dimension_semantics=("arbitrary",),
                                             vmem_limit_bytes=VMEM_LIMIT),
        name="proj",
    )(x2, w_all, *tok_args)


def _attn_kernel(layer, sink_ref, cur_ref, prev_ref, o_ref):
    n = pl.program_id(1)
    blk = ATT_BLOCK
    nsub = ATT_TILE // blk
    grp = A_Q_HEADS // A_KV_HEADS
    lo = lax.broadcasted_iota(jnp.int32, (blk, A_KV_WIDTH), 1) < HEAD_DIM

    def variants(x16):
        x = x16.astype(F32)
        xs = pltpu.roll(x, HEAD_DIM, axis=1)
        return ((jnp.where(lo, x, 0.0).astype(BF16), jnp.where(lo, 0.0, xs).astype(BF16)),
                (jnp.where(lo, xs, 0.0).astype(BF16), jnp.where(lo, 0.0, x).astype(BF16)))

    k0, v0 = A_WIDTH, A_WIDTH + A_KV_WIDTH
    g0 = A_WIDTH + 2 * A_KV_WIDTH
    kblk = [variants(prev_ref[0, :, 0:A_KV_WIDTH])]
    vblk = [variants(prev_ref[0, :, A_KV_WIDTH:2 * A_KV_WIDTH])]
    for j in range(nsub):
        kblk.append(variants(cur_ref[0, j * blk:(j + 1) * blk, k0:k0 + A_KV_WIDTH]))
        vblk.append(variants(cur_ref[0, j * blk:(j + 1) * blk, v0:v0 + A_KV_WIDTH]))

    qi = lax.broadcasted_iota(jnp.int32, (blk, 2 * blk), 0)
    kj = lax.broadcasted_iota(jnp.int32, (blk, 2 * blk), 1)
    diff = qi + blk - kj
    band = (diff >= 0) & (diff < blk)
    first = band & (kj + (n * nsub - 1) * blk >= 0)

    for j in range(nsub):
        mask = first if j == 0 else band
        rows = slice(j * blk, (j + 1) * blk)
        scores = []
        for p in range(A_Q_HEADS // 2):
            h = (2 * p) // grp
            qp = cur_ref[0, rows, p * LANES:(p + 1) * LANES]
            for half in range(2):
                kb = jnp.concatenate([kblk[j][h][half], kblk[j + 1][h][half]], axis=0)
                scores.append(lax.dot_general(qp, kb, (((1,), (1,)), ((), ())),
                                              preferred_element_type=F32))
        probs, inv = [], []
        for idx, s in enumerate(scores):
            sink2 = sink_ref[layer, idx] * LOG2E
            s = jnp.where(mask, s, -jnp.inf)
            m = jnp.maximum(jnp.max(s, axis=-1, keepdims=True), sink2)
            e = jnp.exp2(s - m)
            den = jnp.sum(e, axis=-1, keepdims=True) + jnp.exp2(sink2 - m)
            probs.append(e.astype(BF16))
            inv.append(1.0 / den)
        for p in range(A_Q_HEADS // 2):
            h = (2 * p) // grp
            acc = None
            for half in range(2):
                vb = jnp.concatenate([vblk[j][h][half], vblk[j + 1][h][half]], axis=0)
                o = jnp.dot(probs[2 * p + half], vb, preferred_element_type=F32) * inv[2 * p + half]
                acc = o if acc is None else acc + o
            g = cur_ref[0, rows, g0 + p * LANES:g0 + (p + 1) * LANES].astype(F32)
            o_ref[0, rows, p * LANES:(p + 1) * LANES] = (acc * _silu(g)).astype(BF16)


def _attn(pa, sinks, layer):
    bsz, seq, _ = pa.shape
    nsub = ATT_TILE // ATT_BLOCK
    kv_blk = A_WIDTH // (2 * A_KV_WIDTH)
    return pl.pallas_call(
        functools.partial(_attn_kernel, layer),
        grid=(bsz, seq // ATT_TILE),
        in_specs=[pl.BlockSpec(memory_space=pltpu.SMEM),
                  pl.BlockSpec((1, ATT_TILE, GA_WIDTH), lambda b, n: (b, n, 0)),
                  pl.BlockSpec((1, ATT_BLOCK, 2 * A_KV_WIDTH),
                               lambda b, n: (b, jnp.maximum(n * nsub - 1, 0), kv_blk))],
        out_specs=pl.BlockSpec((1, ATT_TILE, A_WIDTH), lambda b, n: (b, n, 0)),
        out_shape=jax.ShapeDtypeStruct((bsz, seq, A_WIDTH), BF16),
        compiler_params=pltpu.CompilerParams(dimension_semantics=("arbitrary", "arbitrary"),
                                             vmem_limit_bytes=VMEM_LIMIT),
        name="attn",
    )(sinks, pa, pa)


def _lru_kernel(pb_ref, cw_ref, vec_ref, wa_ref, wx_ref, o_ref, in_ref, stage_ref):
    seq = pb_ref.shape[1]
    tile = LRU_TILE
    nseg = SUBLANES
    nstep = tile // nseg
    conv_b = vec_ref[0:1, :]
    ba = vec_ref[1:2, :]
    bx = vec_ref[2:3, :]
    lam = vec_ref[3:4, :]
    neg_c_sp = -LRU_C * _softplus(-lam)
    cw = [cw_ref[j:j + 1, :] for j in range(CONV_WIDTH)]
    sub0 = lax.broadcasted_iota(jnp.int32, (nseg, B_WIDTH), 0) == 0

    nhalf = B_WIDTH // LANES

    pitch = nstep + 4
    assert pitch % SUBLANES == 4

    def rows_of(step):
        return pl.ds(step, nseg, stride=pitch)

    def permuted(k0, step):
        return jnp.concatenate([in_ref[k0 + k, rows_of(step), :] for k in range(nhalf)], axis=1)

    hc = jnp.zeros((1, B_WIDTH), F32)
    tail = [jnp.zeros((1, B_WIDTH), F32)] * (CONV_WIDTH - 1)
    for ti in range(seq // tile):
        t0 = ti * tile
        for k in range(2 * nhalf):
            for sgi in range(nseg):
                in_ref[k, sgi * pitch:sgi * pitch + nstep, :] = pb_ref[
                    0, t0 + sgi * nstep:t0 + (sgi + 1) * nstep, k * LANES:(k + 1) * LANES]
        xs = [permuted(0, v) for v in range(nstep)]
        wrapped = []
        for i in range(CONV_WIDTH - 1):
            src_v = xs[nstep - (CONV_WIDTH - 1) + i]
            wrapped.append(jnp.where(sub0, tail[i], pltpu.roll(src_v, 1, axis=0)))
        x_all = jnp.concatenate(xs, axis=0)
        xc = cw[CONV_WIDTH - 1] * x_all + conv_b
        for j in range(1, CONV_WIDTH):
            shifted = jnp.concatenate(wrapped[CONV_WIDTH - 1 - j:] + xs[:nstep - j], axis=0)
            xc = xc + cw[CONV_WIDTH - 1 - j] * shifted
        tail = [xs[nstep - (CONV_WIDTH - 1) + i][nseg - 1:nseg, :] for i in range(CONV_WIDTH - 1)]

        r = _sigmoid(_mm(xc, wa_ref[...]) + ba)
        ig = _sigmoid(_mm(xc, wx_ref[...]) + bx)
        a = jnp.exp(neg_c_sp * r)
        u = jnp.sqrt(1.0 - a * a) * (ig * xc)
        hl = [u[0:nseg]]
        al = [a[0:nseg]]
        for v in range(1, nstep):
            av = a[v * nseg:(v + 1) * nseg]
            hl.append(av * hl[-1] + u[v * nseg:(v + 1) * nseg])
            al.append(av * al[-1])
        carry = hc
        enter = []
        for sgi in range(nseg):
            enter.append(carry)
            carry = hl[-1][sgi:sgi + 1, :] + al[-1][sgi:sgi + 1, :] * carry
        hc = carry
        enter = jnp.concatenate(enter, axis=0)
        for v in range(nstep):
            out = (hl[v] + al[v] * enter) * _silu(permuted(nhalf, v))
            for k in range(nhalf):
                stage_ref[k, rows_of(v), :] = out[:, k * LANES:(k + 1) * LANES]
        for k in range(nhalf):
            for sgi in range(nseg):
                o_ref[0, t0 + sgi * nstep:t0 + (sgi + 1) * nstep, k * LANES:(k + 1) * LANES] = (
                    stage_ref[k, sgi * pitch:sgi * pitch + nstep, :].astype(BF16))


def _lru(pb, cw_all, vec_all, wa_all, wx_all, layer):
    bsz, seq, _ = pb.shape
    return pl.pallas_call(
        _lru_kernel,
        grid=(bsz,),
        in_specs=[pl.BlockSpec((1, seq, GB_WIDTH), lambda b: (b, 0, 0)),
                  _layer_spec((CONV_WIDTH, B_WIDTH), layer), _layer_spec((4, B_WIDTH), layer),
                  _layer_spec((B_WIDTH, B_WIDTH), layer), _layer_spec((B_WIDTH, B_WIDTH), layer)],
        out_specs=pl.BlockSpec((1, seq, B_WIDTH), lambda b: (b, 0, 0)),
        out_shape=jax.ShapeDtypeStruct((bsz, seq, B_WIDTH), BF16),
        scratch_shapes=[pltpu.VMEM((GB_WIDTH // LANES, LRU_STAGE_ROWS, LANES), F32),
                        pltpu.VMEM((B_WIDTH // LANES, LRU_STAGE_ROWS, LANES), F32)],
        compiler_params=pltpu.CompilerParams(dimension_semantics=("arbitrary",),
                                             vmem_limit_bytes=VMEM_LIMIT),
        name="lru",
    )(pb, cw_all, vec_all, wa_all, wx_all)


def _stack_heads(x):
    x16 = x.astype(BF16)
    head = lax.broadcasted_iota(jnp.int32, x16.shape, 1) // HEAD_DIM
    return jnp.concatenate([jnp.where(head == h, x16, jnp.zeros_like(x16)) for h in range(C_HEADS)],
                           axis=0)


def _unstack_heads(x):
    head = lax.broadcasted_iota(jnp.int32, (CHUNK, STACK), 1) // HEAD_DIM
    out = x[(C_HEADS - 1) * CHUNK:C_HEADS * CHUNK]
    for h in range(C_HEADS - 2, -1, -1):
        out = jnp.where(head == h, x[h * CHUNK:(h + 1) * CHUNK], out)
    return out


def _rwkv_pre(slab, vfirst, vec, v1p, v2p, tril_l, sl, le, eye_w):
    r, k2, v, kap, b, lw = (slab[:, i * C_WIDTH:(i + 1) * C_WIDTH]
                            for i in (PC_R, PC_K, PC_V, PC_KAP, PC_B, PC_LW))
    r_k = vec[4:5, :]
    v0 = vec[7:8, :]
    if vfirst is not None:
        v = v + (vfirst - v) * _sigmoid(v0 + _mm(_mm(v, v1p), v2p))
    yield
    cum = _mm_exact_lhs(tril_l, lw, 2)
    cum_end = cum[CHUNK - 1:CHUNK, :]
    e_in = jnp.exp2(cum)
    e_ex = jnp.exp2(cum - lw)
    e_neg = jnp.exp2(-cum)
    e_end = jnp.exp2(cum_end - cum)
    p_end = jnp.exp2(cum_end)

    yield
    kt = kap * e_ex
    rt = r * e_in
    kt_st = _stack_heads(kt)
    vs_st = _stack_heads(v)
    gram = _mm_nt(jnp.concatenate([kt, rt], axis=0),
                  jnp.concatenate([_stack_heads(b * e_neg), _stack_heads(k2 * e_neg)], axis=0))
    a_ab = jnp.where(sl, gram[0:CHUNK, 0:STACK], 0.0)
    a_ak = jnp.where(sl, gram[0:CHUNK, STACK:2 * STACK], 0.0)
    a_rb = jnp.where(le, gram[CHUNK:2 * CHUNK, 0:STACK], 0.0)
    a_rk = jnp.where(le, gram[CHUNK:2 * CHUNK, STACK:2 * STACK], 0.0)

    yield
    pw = -a_ab
    t_w = eye_w + pw
    av = _mm(jnp.concatenate([a_ak, a_rk], axis=0), vs_st)
    pw = _mm(pw, _stack_heads(pw))
    for _ in range(1, int(math.log2(CHUNK)) - 1):
        yield
        both = _mm(jnp.concatenate([t_w, pw], axis=0), _stack_heads(pw))
        t_w = t_w + both[0:CHUNK]
        pw = both[CHUNK:2 * CHUNK]
    yield
    t_w = t_w + _mm(t_w, _stack_heads(pw))
    yield
    tx = _mm(t_w, jnp.concatenate([kt_st, _stack_heads(av[0:CHUNK])], axis=1))
    yield
    m1 = tx[:, 0:STACK]
    c1 = tx[:, STACK:2 * STACK]
    ab = _mm(a_rb, jnp.concatenate([_stack_heads(m1), _stack_heads(c1)], axis=1))
    q_n = rt - ab[:, 0:STACK]
    d_n = av[CHUNK:2 * CHUNK] - ab[:, STACK:2 * STACK]
    bh = b * e_end
    g_w = eye_w * p_end - _unstack_heads(_mm_tn(bh, m1))
    c_w = _unstack_heads(_mm_tn(jnp.concatenate([k2 * e_end, bh], axis=0),
                                jnp.concatenate([v, -c1], axis=0)))
    bonus_arg = r * k2 * r_k
    return q_n, d_n, g_w, c_w, bonus_arg, v, slab[:, PC_G * C_WIDTH:PC_WIDTH]


RWKV_SAVED = 7


def _rwkv_kernel(use_vres, *refs):
    if use_vres:
        (pc_ref, vf_ref, vec_ref, v1_ref, v2_ref, tril_ref, bd_ref,
         sl_ref, le_ref, eye_ref, y_ref, h_ref, pre_ref, ystage_ref) = refs
    else:
        (pc_ref, vec_ref, v1_ref, v2_ref, tril_ref, bd_ref,
         sl_ref, le_ref, eye_ref, y_ref, h_ref, pre_ref, ystage_ref) = refs
        vf_ref = None
    seq = pc_ref.shape[1]
    ngroups = seq // (CHUNK * RWKV_GROUP)
    step_id = pl.program_id(0)
    last_step = pl.num_programs(0) - 1

    @pl.when(step_id == 0)
    def _():
        h_ref[...] = jnp.zeros_like(h_ref)
        pre_ref[...] = jnp.zeros_like(pre_ref)
        ystage_ref[...] = jnp.zeros_like(ystage_ref)

    def chunk_start(gi, j):
        return pl.multiple_of((gi * RWKV_GROUP + j) * CHUNK, CHUNK)

    def pre_generators(gi):
        sl = sl_ref[...] > 0.0
        le = le_ref[...] > 0.0
        gens = []
        for j in range(RWKV_GROUP):
            t0 = chunk_start(gi, j)
            slab = pc_ref[0, pl.ds(t0, CHUNK), :]
            vfirst = vf_ref[0, pl.ds(t0, CHUNK), :] if use_vres else None
            gens.append(_rwkv_pre(slab, vfirst, vec_ref[...], v1_ref[...], v2_ref[...], tril_ref[...],
                                  sl, le, eye_ref[...]))
        return gens

    def chain_tasks(gi):
        ys = [None] * RWKV_GROUP
        stats = [None] * RWKV_GROUP
        state = {}

        def step(j):
            def run():
                h_w = h_ref[...] if j == 0 else state["h"]
                q_n, d_n, g_w, c_w = (pre_ref[j * RWKV_SAVED + k] for k in range(4))
                res = _mm(jnp.concatenate([q_n, g_w], axis=0), _stack_heads(h_w))
                ys[j] = res[0:CHUNK] + d_n
                state["h"] = res[CHUNK:2 * CHUNK] + c_w
                if j == RWKV_GROUP - 1:
                    h_ref[...] = state["h"]
            return run

        def sums():
            for j in range(RWKV_GROUP):
                parts = _split_bf16(ys[j], 2) + [pre_ref[j * RWKV_SAVED + 4].astype(BF16)]
                s = jnp.dot(jnp.concatenate(parts, axis=0), bd_ref[...], preferred_element_type=F32)
                yc = ys[j] - (s[0:CHUNK] + s[CHUNK:2 * CHUNK]) * (1.0 / HEAD_DIM)
                stats[j] = (yc, s[2 * CHUNK:3 * CHUNK])

        def variance():
            for j in range(RWKV_GROUP):
                yc, bonus = stats[j]
                yv = _mm(yc * yc, bd_ref[...]) * (1.0 / HEAD_DIM)
                stats[j] = (yc, bonus, yv)

        def finish():
            gn_w = vec_ref[5:6, :]
            gn_b = vec_ref[6:7, :]
            for j in range(RWKV_GROUP):
                t0 = chunk_start(gi, j)
                yc, bonus, yv = stats[j]
                yn = yc * lax.rsqrt(yv + GN_EPS) * gn_w + gn_b
                out = (yn + bonus * pre_ref[j * RWKV_SAVED + 5]) * _silu(pre_ref[j * RWKV_SAVED + 6])
                ystage_ref[pl.ds(t0, CHUNK), :] = out.astype(BF16)

        return [step(j) for j in range(RWKV_GROUP)] + [sums, variance, finish]

    def run_group(gi, tasks):
        gens = pre_generators(gi)
        tasks = list(tasks)
        pre = [None] * RWKV_GROUP
        sweep = 0
        while any(p is None for p in pre):
            for j, gen in enumerate(gens):
                if pre[j] is None:
                    try:
                        next(gen)
                    except StopIteration as stop:
                        pre[j] = stop.value
            sweep += 1
            if tasks and sweep >= 2:
                tasks.pop(0)()
        for t in tasks:
            t()
        for j in range(RWKV_GROUP):
            for k in range(RWKV_SAVED):
                pre_ref[j * RWKV_SAVED + k] = pre[j][k]

    def flush():
        y_ref[0] = ystage_ref[...]
        h_ref[...] = jnp.zeros_like(h_ref)

    @pl.when(step_id < last_step)
    def _():
        run_group(0, chain_tasks(ngroups - 1) + [flush])

        def body(i, carry):
            run_group(i, chain_tasks(i - 1))
            return carry

        lax.fori_loop(1, ngroups, body, 0)

    @pl.when(step_id == last_step)
    def _():
        for t in chain_tasks(ngroups - 1) + [flush]:
            t()


def _rwkv_consts():
    idx = np.arange(STACK)
    same = (idx[:, None] // CHUNK) == (idx[None, :] // CHUNK)
    ci = np.arange(CHUNK)
    tril_l = ci[None, :] <= ci[:, None]
    s_loc = idx[None, :] % CHUNK
    sl = s_loc < ci[:, None]
    le = s_loc <= ci[:, None]
    eye_w = s_loc == ci[:, None]
    f32 = lambda m: jnp.asarray(m.astype(np.float32))
    return f32(tril_l).astype(BF16), f32(same).astype(BF16), f32(sl), f32(le), f32(eye_w)


def _rwkv(pc, pc_first, vec_all, v1_all, v2_all, consts, layer):
    bsz, seq, _ = pc.shape
    use_vres = pc_first is not None
    sq = (STACK, STACK)
    wide = (CHUNK, STACK)
    cur = lambda b: jnp.minimum(b, bsz - 1)
    in_specs = [pl.BlockSpec((1, seq, PC_WIDTH), lambda b: (cur(b), 0, 0))]
    if use_vres:
        in_specs.append(pl.BlockSpec((1, seq, C_WIDTH), lambda b: (cur(b), 0, PC_V)))
    in_specs += [_layer_spec((8, C_WIDTH), layer),
                 _layer_spec((C_WIDTH, LANES), layer), _layer_spec((LANES, C_WIDTH), layer),
                 _const_spec((CHUNK, CHUNK)), _const_spec(sq),
                 _const_spec(wide), _const_spec(wide), _const_spec(wide)]
    args = (pc,) + ((pc_first,) if use_vres else ()) + (vec_all, v1_all, v2_all)
    return pl.pallas_call(
        functools.partial(_rwkv_kernel, use_vres),
        grid=(bsz + 1,),
        in_specs=in_specs,
        out_specs=pl.BlockSpec((1, seq, C_WIDTH), lambda b: (jnp.maximum(b - 1, 0), 0, 0)),
        out_shape=jax.ShapeDtypeStruct((bsz, seq, C_WIDTH), BF16),
        scratch_shapes=[pltpu.VMEM(wide, F32), pltpu.VMEM((RWKV_GROUP * RWKV_SAVED,) + wide, F32),
                        pltpu.VMEM((seq, C_WIDTH), BF16)],
        compiler_params=pltpu.CompilerParams(dimension_semantics=("arbitrary",),
                                             vmem_limit_bytes=VMEM_LIMIT),
        name="rwkv",
    )(*args, *consts)


def _out_kernel(alpha, tm, seq, fused, ya_ref, yb_ref, yc_ref, x_ref, w_ref, ln_ref, *rest):
    if fused:
        win_ref, *tok_refs = rest[0:6]
        o_ref, oa_ref, ob_ref, oc_ref, last_ref = rest[6:]
        _init_last(last_ref)
    else:
        (o_ref,) = rest

    def project(rows):
        acc = jnp.dot(ya_ref[rows, :], w_ref[0:A_WIDTH, :], preferred_element_type=F32)
        acc = acc + jnp.dot(yb_ref[rows, :], w_ref[A_WIDTH:A_WIDTH + B_WIDTH, :],
                            preferred_element_type=F32)
        return acc + jnp.dot(yc_ref[rows, :], w_ref[A_WIDTH + B_WIDTH:D_MIX, :],
                             preferred_element_type=F32)

    def norm(rows, acc, pending):
        z = alpha * x_ref[rows, :] + acc
        mu = jnp.mean(z, axis=-1, keepdims=True)
        zc = z - mu
        var = jnp.mean(zc * zc, axis=-1, keepdims=True)
        xn = zc * lax.rsqrt(var + LN_EPS) * ln_ref[0:1, :] + ln_ref[1:2, :]
        o_ref[rows, :] = xn
        if not fused:
            return None
        return _proj_rows(xn.astype(BF16), win_ref, rows.start, OUT_SUB, seq, tok_refs, last_ref,
                          oa_ref, ob_ref, oc_ref, pending)

    subs = [slice(s * OUT_SUB, (s + 1) * OUT_SUB) for s in range(tm // OUT_SUB)]
    acc_prev = project(subs[0])
    pending = None
    for s in range(len(subs)):
        acc_next = project(subs[s + 1]) if s + 1 < len(subs) else None
        pending = norm(subs[s], acc_prev, pending)
        acc_prev = acc_next
    _drain(pending)


def _out(ya, yb, yc, x2, w_all, ln_all, alpha, layer, seq, w_in_all=None, tok_args=()):
    n = x2.shape[0]
    fused = w_in_all is not None
    tm = FUSED_TM if fused else OUT_TM
    row = lambda w: pl.BlockSpec((tm, w), lambda i: (i, 0))
    in_specs = [row(A_WIDTH), row(B_WIDTH), row(C_WIDTH), row(D_MODEL),
                _layer_spec((D_MIX, D_MODEL), layer, single_buffer=True),
                _layer_spec((2, D_MODEL), layer)]
    out_specs = [row(D_MODEL)]
    out_shape = [jax.ShapeDtypeStruct((n, D_MODEL), F32)]
    args = [ya, yb, yc, x2, w_all, ln_all]
    if fused:
        in_specs.append(_layer_spec((D_IN_PAD, D_MODEL), layer + 1, single_buffer=True))
        in_specs += _tok_specs(layer + 1)
        out_specs += [row(GA_WIDTH), row(GB_WIDTH), row(PC_WIDTH)]
        out_shape += [jax.ShapeDtypeStruct((n, GA_WIDTH), BF16),
                      jax.ShapeDtypeStruct((n, GB_WIDTH), F32),
                      jax.ShapeDtypeStruct((n, PC_WIDTH), F32)]
        args += [w_in_all, *tok_args]
    return pl.pallas_call(
        functools.partial(_out_kernel, alpha, tm, seq, fused),
        grid=(n // tm,),
        in_specs=in_specs,
        out_specs=out_specs,
        out_shape=out_shape,
        scratch_shapes=[pltpu.VMEM((SUBLANES, C_SHIFT_PAD), F32)] if fused else [],
        compiler_params=pltpu.CompilerParams(dimension_semantics=("arbitrary",),
                                             vmem_limit_bytes=VMEM_LIMIT),
        name="outproj_proj" if fused else "outproj",
    )(*args)


def _block_diag_all(w):
    depth, nblk, d, _ = w.shape
    eye = jnp.eye(nblk, dtype=w.dtype)
    return (w[:, :, :, None, :] * eye[None, :, None, :, None]).reshape(depth, nblk * d, nblk * d)


def kernel(x, w_in, w_out, ln_g, ln_b, attn_sinks, conv_w, conv_b, lru_wa, lru_ba, lru_wx, lru_bx,
           lru_lambda, rwkv_mu, rwkv_w0, rwkv_w2, rwkv_a0, rwkv_a2, rwkv_kk, rwkv_ka, rwkv_rk,
           rwkv_gn_w, rwkv_gn_b, rwkv_v0, rwkv_v1, rwkv_v2):
    bsz, seq, dm = x.shape
    depth = w_in.shape[0]
    alpha = (2 * depth) ** 0.25
    n = bsz * seq
    pad_w = C_SHIFT_PAD - C_SHIFT_WIDTH

    q_scale = np.ones((w_in.shape[2],), np.float32)
    q_scale[:A_WIDTH] = HEAD_DIM ** -0.5 * LOG2E
    w_in_p = (jnp.swapaxes(w_in, 1, 2) * q_scale[None, :, None]).astype(BF16)
    w_out_b = w_out.astype(BF16)
    ln_all = jnp.stack([ln_g, ln_b], axis=1)
    lru_vec = jnp.stack([conv_b, lru_ba, lru_bx, lru_lambda], axis=1)
    wa_bd = _block_diag_all(lru_wa).astype(BF16)
    wx_bd = _block_diag_all(lru_wx).astype(BF16)
    mu_all = jnp.pad(rwkv_mu, ((0, 0), (0, pad_w)))[:, None, :]
    v0_all = jnp.pad(rwkv_v0, ((1, 0), (0, 0)))
    vec_all = jnp.stack([rwkv_w0, rwkv_a0, rwkv_kk, rwkv_ka, rwkv_rk.reshape(depth, C_WIDTH),
                         rwkv_gn_w, rwkv_gn_b, v0_all], axis=1)
    w2_all = jnp.pad(rwkv_w2, ((0, 0), (0, LANES - DECAY_RANK), (0, 0))).astype(BF16)
    a2_all = jnp.pad(rwkv_a2, ((0, 0), (DECAY_RANK, LANES - DECAY_RANK - AICL_RANK), (0, 0))).astype(BF16)
    v1_all = jnp.pad(rwkv_v1, ((1, 0), (0, 0), (0, LANES - VRES_RANK))).astype(BF16)
    v2_all = jnp.pad(rwkv_v2, ((1, 0), (0, LANES - VRES_RANK), (0, 0))).astype(BF16)
    consts = _rwkv_consts()

    tok_args = (mu_all, vec_all, w2_all, a2_all, consts[1])
    x2 = x.reshape(n, dm)
    pc_first = None
    pa, pb, pc = _proj(x2, w_in_p, tok_args, 0, seq)
    for l in range(depth):
        pc3 = pc.reshape(bsz, seq, PC_WIDTH)
        ya = _attn(pa.reshape(bsz, seq, GA_WIDTH), attn_sinks, l)
        yb = _lru(pb.reshape(bsz, seq, GB_WIDTH), conv_w, lru_vec, wa_bd, wx_bd, l)
        yc = _rwkv(pc3, pc_first, vec_all, v1_all, v2_all, consts, l)
        if l == 0:
            pc_first = pc3
        outs = _out(ya.reshape(n, A_WIDTH), yb.reshape(n, B_WIDTH), yc.reshape(n, C_WIDTH), x2,
                    w_out_b, ln_all, alpha, l, seq, w_in_p if l + 1 < depth else None, tok_args)
        x2 = outs[0]
        if l + 1 < depth:
            pa, pb, pc = outs[1:]
    return x2.reshape(bsz, seq, dm)
```

```python
import functools
import math

import jax
import jax.numpy as jnp
import numpy as np
from jax import lax
from jax.experimental import pallas as pl
from jax.experimental.pallas import tpu as pltpu

F32 = jnp.float32
BF16 = jnp.bfloat16

D_MODEL = 1024
HEAD_DIM = 64
A_Q_HEADS = 8
A_KV_HEADS = 2
A_WIDTH = A_Q_HEADS * HEAD_DIM
A_KV_WIDTH = A_KV_HEADS * HEAD_DIM
ATT_BLOCK = 128
B_WIDTH = 256
CONV_WIDTH = 4
LRU_C = 8.0
C_HEADS = 4
C_WIDTH = C_HEADS * HEAD_DIM
DECAY_RANK = 32
AICL_RANK = 32
VRES_RANK = 16
C_SHIFT_WIDTH = 3 * C_WIDTH + DECAY_RANK + AICL_RANK
GN_EPS = 64e-5
LN_EPS = 1e-5
LOG2E = math.log2(math.e)

LANES = 128
SUBLANES = 8
C_SHIFT_PAD = ((C_SHIFT_WIDTH + LANES - 1) // LANES) * LANES
GA_WIDTH = A_WIDTH + 2 * A_KV_WIDTH + A_WIDTH
GB_WIDTH = 2 * B_WIDTH
GC_WIDTH = C_SHIFT_WIDTH + C_WIDTH
D_IN_PAD = GA_WIDTH + GB_WIDTH + GC_WIDTH
D_MIX = A_WIDTH + B_WIDTH + C_WIDTH
PC_R, PC_K, PC_V, PC_KAP, PC_B, PC_LW, PC_G = range(7)
PC_WIDTH = 7 * C_WIDTH

CHUNK = 64
STACK = C_HEADS * CHUNK
RWKV_GROUP = 8
ATT_TILE = 2048
LRU_TILE = 256
LRU_STAGE_ROWS = SUBLANES * (LRU_TILE // SUBLANES + 4)
PROJ_TM = 1024
OUT_TM = 1024
FUSED_TM = 1024
OUT_SUB = 256
VMEM_LIMIT = 56 * 1024 * 1024


def _mm(a, b):
    return jnp.dot(a.astype(BF16), b.astype(BF16), preferred_element_type=F32)


def _mm_nt(a, b):
    return lax.dot_general(a.astype(BF16), b.astype(BF16), (((1,), (1,)), ((), ())),
                           preferred_element_type=F32)


def _mm_tn(a, b):
    return lax.dot_general(a.astype(BF16), b.astype(BF16), (((0,), (0,)), ((), ())),
                           preferred_element_type=F32)


def _split_bf16(x, parts):
    out = []
    rem = x
    for _ in range(parts):
        hi = rem.astype(BF16)
        out.append(hi)
        rem = rem - hi.astype(F32)
    return out


def _mm_exact_lhs(m_bf16, x, parts):
    n = x.shape[1]
    t = jnp.dot(m_bf16, jnp.concatenate(_split_bf16(x, parts), axis=1), preferred_element_type=F32)
    acc = t[:, 0:n]
    for p in range(1, parts):
        acc = acc + t[:, p * n:(p + 1) * n]
    return acc


def _sigmoid(x):
    return 1.0 / (1.0 + jnp.exp2(x * (-LOG2E)))


def _silu(x):
    return x * _sigmoid(x)


def _softplus(x):
    return jnp.maximum(x, 0.0) + jnp.log(1.0 + jnp.exp(-jnp.abs(x)))


def _layer_spec(shape, layer, single_buffer=False):
    index_map = lambda *_: (layer,) + (0,) * len(shape)
    if single_buffer:
        return pl.BlockSpec((None,) + shape, index_map, pipeline_mode=pl.Buffered(1))
    return pl.BlockSpec((None,) + shape, index_map)


def _const_spec(shape):
    return pl.BlockSpec(shape, lambda *_: (0,) * len(shape))


def _rwkv_tokenwise(cs, g, prev_row, mu, vec, w2p, a2p, ones_bd):
    w0, a0, k_k, k_a = (vec[i:i + 1, :] for i in range(4))
    row = lax.broadcasted_iota(jnp.int32, cs.shape, 0)
    prev = jnp.where(row == 0, prev_row, pltpu.roll(cs, 1, axis=0))
    xs = cs + (prev - cs) * mu
    r = xs[:, 0:C_WIDTH]
    k = xs[:, C_WIDTH:2 * C_WIDTH]
    v = xs[:, 2 * C_WIDTH:3 * C_WIDTH]
    la = xs[:, 3 * C_WIDTH:C_SHIFT_PAD]
    kk = k * k_k
    yield
    z_w = _mm(jnp.tanh(la), w2p)
    z_a = _mm(la, a2p)
    norm2 = _mm(kk * kk, ones_bd)
    yield
    lw = (-LOG2E * math.exp(-0.5)) * _sigmoid(w0 + z_w)
    a = _sigmoid(a0 + z_a)
    kap = kk * lax.rsqrt(norm2 + 1e-12)
    k2 = k * (1.0 + (a - 1.0) * k_a)
    return jnp.concatenate([r, k2, v, kap, kap * a, lw, g], axis=1), cs[cs.shape[0] - 1:, :]


def _drain(stage):
    if stage is not None:
        for _ in stage:
            pass


def _proj_rows(x16, w_ref, row0, nrows, seq, tok_refs, last_ref, oa_ref, ob_ref, oc_ref, pending):
    mu_ref, vec_ref, w2_ref, a2_ref, ones_ref = tok_refs
    nt = (((1,), (1,)), ((), ()))
    rows = pl.ds(row0, nrows)

    def advance():
        if pending is not None:
            next(pending, None)

    oa_ref[rows, :] = lax.dot_general(x16, w_ref[0:GA_WIDTH, :], nt,
                                      preferred_element_type=F32).astype(BF16)
    advance()
    ob_ref[rows, :] = lax.dot_general(x16, w_ref[GA_WIDTH:GA_WIDTH + GB_WIDTH, :], nt,
                                      preferred_element_type=F32)
    advance()
    c0 = GA_WIDTH + GB_WIDTH
    cs = lax.dot_general(x16, w_ref[c0:c0 + C_SHIFT_PAD, :], nt, preferred_element_type=F32)
    advance()
    g = lax.dot_general(x16, w_ref[c0 + C_SHIFT_WIDTH:D_IN_PAD, :], nt, preferred_element_type=F32)
    _drain(pending)
    first = pl.program_id(0) * oa_ref.shape[0] + row0

    def stage():
        prev_row = jnp.where(first % seq == 0, 0.0, last_ref[0:1, :])
        slab, last = yield from _rwkv_tokenwise(cs, g, prev_row, mu_ref[...], vec_ref[...],
                                                w2_ref[...], a2_ref[...], ones_ref[...])
        oc_ref[rows, :] = slab
        last_ref[0:1, :] = last

    return stage()


def _init_last(last_ref):
    @pl.when(pl.program_id(0) == 0)
    def _():
        last_ref[...] = jnp.zeros_like(last_ref)


def _proj_kernel(seq, x_ref, w_ref, mu_ref, vec_ref, w2_ref, a2_ref, ones_ref, oa_ref, ob_ref, oc_ref,
                 last_ref):
    _init_last(last_ref)
    pending = None
    for row0 in range(0, PROJ_TM, OUT_SUB):
        pending = _proj_rows(x_ref[pl.ds(row0, OUT_SUB), :].astype(BF16), w_ref, row0, OUT_SUB, seq,
                             (mu_ref, vec_ref, w2_ref, a2_ref, ones_ref), last_ref, oa_ref, ob_ref,
                             oc_ref, pending)
    _drain(pending)


def _tok_specs(layer):
    return [_layer_spec((1, C_SHIFT_PAD), layer), _layer_spec((8, C_WIDTH), layer),
            _layer_spec((LANES, C_WIDTH), layer), _layer_spec((LANES, C_WIDTH), layer),
            _const_spec((STACK, STACK))]


def _proj(x2, w_all, tok_args, layer, seq):
    n = x2.shape[0]
    return pl.pallas_call(
        functools.partial(_proj_kernel, seq),
        grid=(n // PROJ_TM,),
        in_specs=[pl.BlockSpec((PROJ_TM, D_MODEL), lambda i: (i, 0)),
                  _layer_spec((D_IN_PAD, D_MODEL), layer, single_buffer=True)] + _tok_specs(layer),
        out_specs=[pl.BlockSpec((PROJ_TM, GA_WIDTH), lambda i: (i, 0)),
                   pl.BlockSpec((PROJ_TM, GB_WIDTH), lambda i: (i, 0)),
                   pl.BlockSpec((PROJ_TM, PC_WIDTH), lambda i: (i, 0))],
        out_shape=[jax.ShapeDtypeStruct((n, GA_WIDTH), BF16),
                   jax.ShapeDtypeStruct((n, GB_WIDTH), F32),
                   jax.ShapeDtypeStruct((n, PC_WIDTH), F32)],
        scratch_shapes=[pltpu.VMEM((SUBLANES, C_SHIFT_PAD), F32)],
        compiler_params=pltpu.CompilerParams(dimension_semantics=("arbitrary",),
                                             vmem_limit_bytes=VMEM_LIMIT),
        name="proj",
    )(x2, w_all, *tok_args)


def _attn_kernel(layer, sink_ref, cur_ref, prev_ref, o_ref):
    n = pl.program_id(1)
    blk = ATT_BLOCK
    nsub = ATT_TILE // blk
    grp = A_Q_HEADS // A_KV_HEADS
    lo = lax.broadcasted_iota(jnp.int32, (blk, A_KV_WIDTH), 1) < HEAD_DIM

    def variants(x16):
        x = x16.astype(F32)
        xs = pltpu.roll(x, HEAD_DIM, axis=1)
        return ((jnp.where(lo, x, 0.0).astype(BF16), jnp.where(lo, 0.0, xs).astype(BF16)),
                (jnp.where(lo, xs, 0.0).astype(BF16), jnp.where(lo, 0.0, x).astype(BF16)))

    k0, v0 = A_WIDTH, A_WIDTH + A_KV_WIDTH
    g0 = A_WIDTH + 2 * A_KV_WIDTH
    kblk = [variants(prev_ref[0, :, 0:A_KV_WIDTH])]
    vblk = [variants(prev_ref[0, :, A_KV_WIDTH:2 * A_KV_WIDTH])]
    for j in range(nsub):
        kblk.append(variants(cur_ref[0, j * blk:(j + 1) * blk, k0:k0 + A_KV_WIDTH]))
        vblk.append(variants(cur_ref[0, j * blk:(j + 1) * blk, v0:v0 + A_KV_WIDTH]))

    qi = lax.broadcasted_iota(jnp.int32, (blk, 2 * blk), 0)
    kj = lax.broadcasted_iota(jnp.int32, (blk, 2 * blk), 1)
    diff = qi + blk - kj
    band = (diff >= 0) & (diff < blk)
    first = band & (kj + (n * nsub - 1) * blk >= 0)

    for j in range(nsub):
        mask = first if j == 0 else band
        rows = slice(j * blk, (j + 1) * blk)
        scores = []
        for p in range(A_Q_HEADS // 2):
            h = (2 * p) // grp
            qp = cur_ref[0, rows, p * LANES:(p + 1) * LANES]
            for half in range(2):
                kb = jnp.concatenate([kblk[j][h][half], kblk[j + 1][h][half]], axis=0)
                scores.append(lax.dot_general(qp, kb, (((1,), (1,)), ((), ())),
                                              preferred_element_type=F32))
        probs, inv = [], []
        for idx, s in enumerate(scores):
            sink2 = sink_ref[layer, idx] * LOG2E
            s = jnp.where(mask, s, -jnp.inf)
            m = jnp.maximum(jnp.max(s, axis=-1, keepdims=True), sink2)
            e = jnp.exp2(s - m)
            den = jnp.sum(e, axis=-1, keepdims=True) + jnp.exp2(sink2 - m)
            probs.append(e.astype(BF16))
            inv.append(1.0 / den)
        for p in range(A_Q_HEADS // 2):
            h = (2 * p) // grp
            acc = None
            for half in range(2):
                vb = jnp.concatenate([vblk[j][h][half], vblk[j + 1][h][half]], axis=0)
                o = jnp.dot(probs[2 * p + half], vb, preferred_element_type=F32) * inv[2 * p + half]
                acc = o if acc is None else acc + o
            g = cur_ref[0, rows, g0 + p * LANES:g0 + (p + 1) * LANES].astype(F32)
            o_ref[0, rows, p * LANES:(p + 1) * LANES] = (acc * _silu(g)).astype(BF16)


def _attn(pa, sinks, layer):
    bsz, seq, _ = pa.shape
    nsub = ATT_TILE // ATT_BLOCK
    kv_blk = A_WIDTH // (2 * A_KV_WIDTH)
    return pl.pallas_call(
        functools.partial(_attn_kernel, layer),
        grid=(bsz, seq // ATT_TILE),
        in_specs=[pl.BlockSpec(memory_space=pltpu.SMEM),
                  pl.BlockSpec((1, ATT_TILE, GA_WIDTH), lambda b, n: (b, n, 0)),
                  pl.BlockSpec((1, ATT_BLOCK, 2 * A_KV_WIDTH),
                               lambda b, n: (b, jnp.maximum(n * nsub - 1, 0), kv_blk))],
        out_specs=pl.BlockSpec((1, ATT_TILE, A_WIDTH), lambda b, n: (b, n, 0)),
        out_shape=jax.ShapeDtypeStruct((bsz, seq, A_WIDTH), BF16),
        compiler_params=pltpu.CompilerParams(dimension_semantics=("arbitrary", "arbitrary"),
                                             vmem_limit_bytes=VMEM_LIMIT),
        name="attn",
    )(sinks, pa, pa)


def _lru_kernel(pb_ref, cw_ref, vec_ref, wa_ref, wx_ref, o_ref, in_ref, stage_ref):
    seq = pb_ref.shape[1]
    tile = LRU_TILE
    nseg = SUBLANES
    nstep = tile // nseg
    conv_b = vec_ref[0:1, :]
    ba = vec_ref[1:2, :]
    bx = vec_ref[2:3, :]
    lam = vec_ref[3:4, :]
    neg_c_sp = -LRU_C * _softplus(-lam)
    cw = [cw_ref[j:j + 1, :] for j in range(CONV_WIDTH)]
    sub0 = lax.broadcasted_iota(jnp.int32, (nseg, B_WIDTH), 0) == 0

    nhalf = B_WIDTH // LANES

    pitch = nstep + 4
    assert pitch % SUBLANES == 4

    def rows_of(step):
        return pl.ds(step, nseg, stride=pitch)

    def permuted(k0, step):
        return jnp.concatenate([in_ref[k0 + k, rows_of(step), :] for k in range(nhalf)], axis=1)

    hc = jnp.zeros((1, B_WIDTH), F32)
    tail = [jnp.zeros((1, B_WIDTH), F32)] * (CONV_WIDTH - 1)
    for ti in range(seq // tile):
        t0 = ti * tile
        for k in range(2 * nhalf):
            for sgi in range(nseg):
                in_ref[k, sgi * pitch:sgi * pitch + nstep, :] = pb_ref[
                    0, t0 + sgi * nstep:t0 + (sgi + 1) * nstep, k * LANES:(k + 1) * LANES]
        xs = [permuted(0, v) for v in range(nstep)]
        wrapped = []
        for i in range(CONV_WIDTH - 1):
            src_v = xs[nstep - (CONV_WIDTH - 1) + i]
            wrapped.append(jnp.where(sub0, tail[i], pltpu.roll(src_v, 1, axis=0)))
        x_all = jnp.concatenate(xs, axis=0)
        xc = cw[CONV_WIDTH - 1] * x_all + conv_b
        for j in range(1, CONV_WIDTH):
            shifted = jnp.concatenate(wrapped[CONV_WIDTH - 1 - j:] + xs[:nstep - j], axis=0)
            xc = xc + cw[CONV_WIDTH - 1 - j] * shifted
        tail = [xs[nstep - (CONV_WIDTH - 1) + i][nseg - 1:nseg, :] for i in range(CONV_WIDTH - 1)]

        r = _sigmoid(_mm(xc, wa_ref[...]) + ba)
        ig = _sigmoid(_mm(xc, wx_ref[...]) + bx)
        a = jnp.exp(neg_c_sp * r)
        u = jnp.sqrt(1.0 - a * a) * (ig * xc)
        hl = [u[0:nseg]]
        al = [a[0:nseg]]
        for v in range(1, nstep):
            av = a[v * nseg:(v + 1) * nseg]
            hl.append(av * hl[-1] + u[v * nseg:(v + 1) * nseg])
            al.append(av * al[-1])
        carry = hc
        enter = []
        for sgi in range(nseg):
            enter.append(carry)
            carry = hl[-1][sgi:sgi + 1, :] + al[-1][sgi:sgi + 1, :] * carry
        hc = carry
        enter = jnp.concatenate(enter, axis=0)
        for v in range(nstep):
            out = (hl[v] + al[v] * enter) * _silu(permuted(nhalf, v))
            for k in range(nhalf):
                stage_ref[k, rows_of(v), :] = out[:, k * LANES:(k + 1) * LANES]
        for k in range(nhalf):
            for sgi in range(nseg):
                o_ref[0, t0 + sgi * nstep:t0 + (sgi + 1) * nstep, k * LANES:(k + 1) * LANES] = (
                    stage_ref[k, sgi * pitch:sgi * pitch + nstep, :].astype(BF16))


def _lru(pb, cw_all, vec_all, wa_all, wx_all, layer):
    bsz, seq, _ = pb.shape
    return pl.pallas_call(
        _lru_kernel,
        grid=(bsz,),
        in_specs=[pl.BlockSpec((1, seq, GB_WIDTH), lambda b: (b, 0, 0)),
                  _layer_spec((CONV_WIDTH, B_WIDTH), layer), _layer_spec((4, B_WIDTH), layer),
                  _layer_spec((B_WIDTH, B_WIDTH), layer), _layer_spec((B_WIDTH, B_WIDTH), layer)],
        out_specs=pl.BlockSpec((1, seq, B_WIDTH), lambda b: (b, 0, 0)),
        out_shape=jax.ShapeDtypeStruct((bsz, seq, B_WIDTH), BF16),
        scratch_shapes=[pltpu.VMEM((GB_WIDTH // LANES, LRU_STAGE_ROWS, LANES), F32),
                        pltpu.VMEM((B_WIDTH // LANES, LRU_STAGE_ROWS, LANES), F32)],
        compiler_params=pltpu.CompilerParams(dimension_semantics=("arbitrary",),
                                             vmem_limit_bytes=VMEM_LIMIT),
        name="lru",
    )(pb, cw_all, vec_all, wa_all, wx_all)


def _stack_heads(x):
    x16 = x.astype(BF16)
    head = lax.broadcasted_iota(jnp.int32, x16.shape, 1) // HEAD_DIM
    return jnp.concatenate([jnp.where(head == h, x16, jnp.zeros_like(x16)) for h in range(C_HEADS)],
                           axis=0)


def _unstack_heads(x):
    head = lax.broadcasted_iota(jnp.int32, (CHUNK, STACK), 1) // HEAD_DIM
    out = x[(C_HEADS - 1) * CHUNK:C_HEADS * CHUNK]
    for h in range(C_HEADS - 2, -1, -1):
        out = jnp.where(head == h, x[h * CHUNK:(h + 1) * CHUNK], out)
    return out


def _rwkv_pre(slab, vfirst, vec, v1p, v2p, tril_l, sl, le, eye_w):
    r, k2, v, kap, b, lw = (slab[:, i * C_WIDTH:(i + 1) * C_WIDTH]
                            for i in (PC_R, PC_K, PC_V, PC_KAP, PC_B, PC_LW))
    r_k = vec[4:5, :]
    v0 = vec[7:8, :]
    if vfirst is not None:
        v = v + (vfirst - v) * _sigmoid(v0 + _mm(_mm(v, v1p), v2p))
    yield
    cum = _mm_exact_lhs(tril_l, lw, 2)
    cum_end = cum[CHUNK - 1:CHUNK, :]
    e_in = jnp.exp2(cum)
    e_ex = jnp.exp2(cum - lw)
    e_neg = jnp.exp2(-cum)
    e_end = jnp.exp2(cum_end - cum)
    p_end = jnp.exp2(cum_end)

    yield
    kt = kap * e_ex
    rt = r * e_in
    kt_st = _stack_heads(kt)
    vs_st = _stack_heads(v)
    gram = _mm_nt(jnp.concatenate([kt, rt], axis=0),
                  jnp.concatenate([_stack_heads(b * e_neg), _stack_heads(k2 * e_neg)], axis=0))
    a_ab = jnp.where(sl, gram[0:CHUNK, 0:STACK], 0.0)
    a_ak = jnp.where(sl, gram[0:CHUNK, STACK:2 * STACK], 0.0)
    a_rb = jnp.where(le, gram[CHUNK:2 * CHUNK, 0:STACK], 0.0)
    a_rk = jnp.where(le, gram[CHUNK:2 * CHUNK, STACK:2 * STACK], 0.0)

    yield
    pw = -a_ab
    t_w = eye_w + pw
    av = _mm(jnp.concatenate([a_ak, a_rk], axis=0), vs_st)
    pw = _mm(pw, _stack_heads(pw))
    for _ in range(1, int(math.log2(CHUNK)) - 1):
        yield
        both = _mm(jnp.concatenate([t_w, pw], axis=0), _stack_heads(pw))
        t_w = t_w + both[0:CHUNK]
        pw = both[CHUNK:2 * CHUNK]
    yield
    t_w = t_w + _mm(t_w, _stack_heads(pw))
    yield
    tx = _mm(t_w, jnp.concatenate([kt_st, _stack_heads(av[0:CHUNK])], axis=1))
    yield
    m1 = tx[:, 0:STACK]
    c1 = tx[:, STACK:2 * STACK]
    ab = _mm(a_rb, jnp.concatenate([_stack_heads(m1), _stack_heads(c1)], axis=1))
    q_n = rt - ab[:, 0:STACK]
    d_n = av[CHUNK:2 * CHUNK] - ab[:, STACK:2 * STACK]
    bh = b * e_end
    g_w = eye_w * p_end - _unstack_heads(_mm_tn(bh, m1))
    c_w = _unstack_heads(_mm_tn(jnp.concatenate([k2 * e_end, bh], axis=0),
                                jnp.concatenate([v, -c1], axis=0)))
    bonus_arg = r * k2 * r_k
    return q_n, d_n, g_w, c_w, bonus_arg, v, slab[:, PC_G * C_WIDTH:PC_WIDTH]


RWKV_STAGGER = 1
RWKV_SAVED = 7


def _rwkv_kernel(use_vres, *refs):
    if use_vres:
        (pc_ref, vf_ref, vec_ref, v1_ref, v2_ref, tril_ref, bd_ref,
         sl_ref, le_ref, eye_ref, y_ref, h_ref, pre_ref, ystage_ref) = refs
    else:
        (pc_ref, vec_ref, v1_ref, v2_ref, tril_ref, bd_ref,
         sl_ref, le_ref, eye_ref, y_ref, h_ref, pre_ref, ystage_ref) = refs
        vf_ref = None
    seq = pc_ref.shape[1]
    ngroups = seq // (CHUNK * RWKV_GROUP)
    step_id = pl.program_id(0)
    last_step = pl.num_programs(0) - 1

    @pl.when(step_id == 0)
    def _():
        h_ref[...] = jnp.zeros_like(h_ref)
        pre_ref[...] = jnp.zeros_like(pre_ref)
        ystage_ref[...] = jnp.zeros_like(ystage_ref)

    def chunk_start(gi, j):
        return pl.multiple_of((gi * RWKV_GROUP + j) * CHUNK, CHUNK)

    def pre_generators(gi):
        sl = sl_ref[...] > 0.0
        le = le_ref[...] > 0.0
        gens = []
        for j in range(RWKV_GROUP):
            t0 = chunk_start(gi, j)
            slab = pc_ref[0, pl.ds(t0, CHUNK), :]
            vfirst = vf_ref[0, pl.ds(t0, CHUNK), :] if use_vres else None
            gens.append(_rwkv_pre(slab, vfirst, vec_ref[...], v1_ref[...], v2_ref[...], tril_ref[...],
                                  sl, le, eye_ref[...]))
        return gens

    def chain_tasks(gi):
        ys = [None] * RWKV_GROUP
        stats = [None] * RWKV_GROUP
        state = {}

        def step(j):
            def run():
                h_w = h_ref[...] if j == 0 else state["h"]
                q_n, d_n, g_w, c_w = (pre_ref[j * RWKV_SAVED + k] for k in range(4))
                res = _mm(jnp.concatenate([q_n, g_w], axis=0), _stack_heads(h_w))
                ys[j] = res[0:CHUNK] + d_n
                state["h"] = res[CHUNK:2 * CHUNK] + c_w
                if j == RWKV_GROUP - 1:
                    h_ref[...] = state["h"]
            return run

        def sums():
            for j in range(RWKV_GROUP):
                parts = _split_bf16(ys[j], 2) + [pre_ref[j * RWKV_SAVED + 4].astype(BF16)]
                s = jnp.dot(jnp.concatenate(parts, axis=0), bd_ref[...], preferred_element_type=F32)
                yc = ys[j] - (s[0:CHUNK] + s[CHUNK:2 * CHUNK]) * (1.0 / HEAD_DIM)
                stats[j] = (yc, s[2 * CHUNK:3 * CHUNK])

        def variance():
            for j in range(RWKV_GROUP):
                yc, bonus = stats[j]
                yv = _mm(yc * yc, bd_ref[...]) * (1.0 / HEAD_DIM)
                stats[j] = (yc, bonus, yv)

        def finish():
            gn_w = vec_ref[5:6, :]
            gn_b = vec_ref[6:7, :]
            for j in range(RWKV_GROUP):
                t0 = chunk_start(gi, j)
                yc, bonus, yv = stats[j]
                yn = yc * lax.rsqrt(yv + GN_EPS) * gn_w + gn_b
                out = (yn + bonus * pre_ref[j * RWKV_SAVED + 5]) * _silu(pre_ref[j * RWKV_SAVED + 6])
                ystage_ref[pl.ds(t0, CHUNK), :] = out.astype(BF16)

        return [step(j) for j in range(RWKV_GROUP)] + [sums, variance, finish]

    def run_group(gi, tasks):
        gens = pre_generators(gi)
        tasks = list(tasks)
        pre = [None] * RWKV_GROUP
        sweep = 0
        while any(p is None for p in pre):
            for j, gen in enumerate(gens):
                if pre[j] is None and sweep >= RWKV_STAGGER * (j // (RWKV_GROUP // 2)):
                    try:
                        next(gen)
                    except StopIteration as stop:
                        pre[j] = stop.value
            sweep += 1
            if tasks and sweep >= 2:
                tasks.pop(0)()
        for t in tasks:
            t()
        for j in range(RWKV_GROUP):
            for k in range(RWKV_SAVED):
                pre_ref[j * RWKV_SAVED + k] = pre[j][k]

    def flush():
        y_ref[0] = ystage_ref[...]
        h_ref[...] = jnp.zeros_like(h_ref)

    @pl.when(step_id < last_step)
    def _():
        run_group(0, chain_tasks(ngroups - 1) + [flush])

        def body(i, carry):
            run_group(i, chain_tasks(i - 1))
            return carry

        lax.fori_loop(1, ngroups, body, 0)

    @pl.when(step_id == last_step)
    def _():
        for t in chain_tasks(ngroups - 1) + [flush]:
            t()


def _rwkv_consts():
    idx = np.arange(STACK)
    same = (idx[:, None] // CHUNK) == (idx[None, :] // CHUNK)
    ci = np.arange(CHUNK)
    tril_l = ci[None, :] <= ci[:, None]
    s_loc = idx[None, :] % CHUNK
    sl = s_loc < ci[:, None]
    le = s_loc <= ci[:, None]
    eye_w = s_loc == ci[:, None]
    f32 = lambda m: jnp.asarray(m.astype(np.float32))
    return f32(tril_l).astype(BF16), f32(same).astype(BF16), f32(sl), f32(le), f32(eye_w)


def _rwkv(pc, pc_first, vec_all, v1_all, v2_all, consts, layer):
    bsz, seq, _ = pc.shape
    use_vres = pc_first is not None
    sq = (STACK, STACK)
    wide = (CHUNK, STACK)
    cur = lambda b: jnp.minimum(b, bsz - 1)
    in_specs = [pl.BlockSpec((1, seq, PC_WIDTH), lambda b: (cur(b), 0, 0))]
    if use_vres:
        in_specs.append(pl.BlockSpec((1, seq, C_WIDTH), lambda b: (cur(b), 0, PC_V)))
    in_specs += [_layer_spec((8, C_WIDTH), layer),
                 _layer_spec((C_WIDTH, LANES), layer), _layer_spec((LANES, C_WIDTH), layer),
                 _const_spec((CHUNK, CHUNK)), _const_spec(sq),
                 _const_spec(wide), _const_spec(wide), _const_spec(wide)]
    args = (pc,) + ((pc_first,) if use_vres else ()) + (vec_all, v1_all, v2_all)
    return pl.pallas_call(
        functools.partial(_rwkv_kernel, use_vres),
        grid=(bsz + 1,),
        in_specs=in_specs,
        out_specs=pl.BlockSpec((1, seq, C_WIDTH), lambda b: (jnp.maximum(b - 1, 0), 0, 0)),
        out_shape=jax.ShapeDtypeStruct((bsz, seq, C_WIDTH), BF16),
        scratch_shapes=[pltpu.VMEM(wide, F32), pltpu.VMEM((RWKV_GROUP * RWKV_SAVED,) + wide, F32),
                        pltpu.VMEM((seq, C_WIDTH), BF16)],
        compiler_params=pltpu.CompilerParams(dimension_semantics=("arbitrary",),
                                             vmem_limit_bytes=VMEM_LIMIT),
        name="rwkv",
    )(*args, *consts)


def _out_kernel(alpha, tm, seq, fused, ya_ref, yb_ref, yc_ref, x_ref, w_ref, ln_ref, *rest):
    if fused:
        win_ref, *tok_refs = rest[0:6]
        o_ref, oa_ref, ob_ref, oc_ref, last_ref = rest[6:]
        _init_last(last_ref)
    else:
        (o_ref,) = rest

    def project(rows):
        acc = jnp.dot(ya_ref[rows, :], w_ref[0:A_WIDTH, :], preferred_element_type=F32)
        acc = acc + jnp.dot(yb_ref[rows, :], w_ref[A_WIDTH:A_WIDTH + B_WIDTH, :],
                            preferred_element_type=F32)
        return acc + jnp.dot(yc_ref[rows, :], w_ref[A_WIDTH + B_WIDTH:D_MIX, :],
                             preferred_element_type=F32)

    def norm(rows, acc, pending):
        z = alpha * x_ref[rows, :] + acc
        mu = jnp.mean(z, axis=-1, keepdims=True)
        zc = z - mu
        var = jnp.mean(zc * zc, axis=-1, keepdims=True)
        xn = zc * lax.rsqrt(var + LN_EPS) * ln_ref[0:1, :] + ln_ref[1:2, :]
        o_ref[rows, :] = xn
        if not fused:
            return None
        return _proj_rows(xn.astype(BF16), win_ref, rows.start, OUT_SUB, seq, tok_refs, last_ref,
                          oa_ref, ob_ref, oc_ref, pending)

    subs = [slice(s * OUT_SUB, (s + 1) * OUT_SUB) for s in range(tm // OUT_SUB)]
    acc_prev = project(subs[0])
    pending = None
    for s in range(len(subs)):
        acc_next = project(subs[s + 1]) if s + 1 < len(subs) else None
        pending = norm(subs[s], acc_prev, pending)
        acc_prev = acc_next
    _drain(pending)


def _out(ya, yb, yc, x2, w_all, ln_all, alpha, layer, seq, w_in_all=None, tok_args=()):
    n = x2.shape[0]
    fused = w_in_all is not None
    tm = FUSED_TM if fused else OUT_TM
    row = lambda w: pl.BlockSpec((tm, w), lambda i: (i, 0))
    in_specs = [row(A_WIDTH), row(B_WIDTH), row(C_WIDTH), row(D_MODEL),
                _layer_spec((D_MIX, D_MODEL), layer, single_buffer=True),
                _layer_spec((2, D_MODEL), layer)]
    out_specs = [row(D_MODEL)]
    out_shape = [jax.ShapeDtypeStruct((n, D_MODEL), F32)]
    args = [ya, yb, yc, x2, w_all, ln_all]
    if fused:
        in_specs.append(_layer_spec((D_IN_PAD, D_MODEL), layer + 1, single_buffer=True))
        in_specs += _tok_specs(layer + 1)
        out_specs += [row(GA_WIDTH), row(GB_WIDTH), row(PC_WIDTH)]
        out_shape += [jax.ShapeDtypeStruct((n, GA_WIDTH), BF16),
                      jax.ShapeDtypeStruct((n, GB_WIDTH), F32),
                      jax.ShapeDtypeStruct((n, PC_WIDTH), F32)]
        args += [w_in_all, *tok_args]
    return pl.pallas_call(
        functools.partial(_out_kernel, alpha, tm, seq, fused),
        grid=(n // tm,),
        in_specs=in_specs,
        out_specs=out_specs,
        out_shape=out_shape,
        scratch_shapes=[pltpu.VMEM((SUBLANES, C_SHIFT_PAD), F32)] if fused else [],
        compiler_params=pltpu.CompilerParams(dimension_semantics=("arbitrary",),
                                             vmem_limit_bytes=VMEM_LIMIT),
        name="outproj_proj" if fused else "outproj",
    )(*args)


def _block_diag_all(w):
    depth, nblk, d, _ = w.shape
    eye = jnp.eye(nblk, dtype=w.dtype)
    return (w[:, :, :, None, :] * eye[None, :, None, :, None]).reshape(depth, nblk * d, nblk * d)


def kernel(x, w_in, w_out, ln_g, ln_b, attn_sinks, conv_w, conv_b, lru_wa, lru_ba, lru_wx, lru_bx,
           lru_lambda, rwkv_mu, rwkv_w0, rwkv_w2, rwkv_a0, rwkv_a2, rwkv_kk, rwkv_ka, rwkv_rk,
           rwkv_gn_w, rwkv_gn_b, rwkv_v0, rwkv_v1, rwkv_v2):
    bsz, seq, dm = x.shape
    depth = w_in.shape[0]
    alpha = (2 * depth) ** 0.25
    n = bsz * seq
    pad_w = C_SHIFT_PAD - C_SHIFT_WIDTH

    q_scale = np.ones((w_in.shape[2],), np.float32)
    q_scale[:A_WIDTH] = HEAD_DIM ** -0.5 * LOG2E
    w_in_p = (jnp.swapaxes(w_in, 1, 2) * q_scale[None, :, None]).astype(BF16)
    w_out_b = w_out.astype(BF16)
    ln_all = jnp.stack([ln_g, ln_b], axis=1)
    lru_vec = jnp.stack([conv_b, lru_ba, lru_bx, lru_lambda], axis=1)
    wa_bd = _block_diag_all(lru_wa).astype(BF16)
    wx_bd = _block_diag_all(lru_wx).astype(BF16)
    mu_all = jnp.pad(rwkv_mu, ((0, 0), (0, pad_w)))[:, None, :]
    v0_all = jnp.pad(rwkv_v0, ((1, 0), (0, 0)))
    vec_all = jnp.stack([rwkv_w0, rwkv_a0, rwkv_kk, rwkv_ka, rwkv_rk.reshape(depth, C_WIDTH),
                         rwkv_gn_w, rwkv_gn_b, v0_all], axis=1)
    w2_all = jnp.pad(rwkv_w2, ((0, 0), (0, LANES - DECAY_RANK), (0, 0))).astype(BF16)
    a2_all = jnp.pad(rwkv_a2, ((0, 0), (DECAY_RANK, LANES - DECAY_RANK - AICL_RANK), (0, 0))).astype(BF16)
    v1_all = jnp.pad(rwkv_v1, ((1, 0), (0, 0), (0, LANES - VRES_RANK))).astype(BF16)
    v2_all = jnp.pad(rwkv_v2, ((1, 0), (0, LANES - VRES_RANK), (0, 0))).astype(BF16)
    consts = _rwkv_consts()

    tok_args = (mu_all, vec_all, w2_all, a2_all, consts[1])
    x2 = x.reshape(n, dm)
    pc_first = None
    pa, pb, pc = _proj(x2, w_in_p, tok_args, 0, seq)
    for l in range(depth):
        pc3 = pc.reshape(bsz, seq, PC_WIDTH)
        ya = _attn(pa.reshape(bsz, seq, GA_WIDTH), attn_sinks, l)
        yb = _lru(pb.reshape(bsz, seq, GB_WIDTH), conv_w, lru_vec, wa_bd, wx_bd, l)
        yc = _rwkv(pc3, pc_first, vec_all, v1_all, v2_all, consts, l)
        if l == 0:
            pc_first = pc3
        outs = _out(ya.reshape(n, A_WIDTH), yb.reshape(n, B_WIDTH), yc.reshape(n, C_WIDTH), x2,
                    w_out_b, ln_all, alpha, l, seq, w_in_p if l + 1 < depth else None, tok_args)
        x2 = outs[0]
        if l + 1 < depth:
            pa, pb, pc = outs[1:]
    return x2.reshape(bsz, seq, dm)
```

```python
import functools
import math

import jax
import jax.numpy as jnp
import numpy as np
from jax import lax
from jax.experimental import pallas as pl
from jax.experimental.pallas import tpu as pltpu

F32 = jnp.float32
BF16 = jnp.bfloat16

D_MODEL = 1024
HEAD_DIM = 64
A_Q_HEADS = 8
A_KV_HEADS = 2
A_WIDTH = A_Q_HEADS * HEAD_DIM
A_KV_WIDTH = A_KV_HEADS * HEAD_DIM
ATT_BLOCK = 128
B_WIDTH = 256
CONV_WIDTH = 4
LRU_C = 8.0
C_HEADS = 4
C_WIDTH = C_HEADS * HEAD_DIM
DECAY_RANK = 32
AICL_RANK = 32
VRES_RANK = 16
C_SHIFT_WIDTH = 3 * C_WIDTH + DECAY_RANK + AICL_RANK
GN_EPS = 64e-5
LN_EPS = 1e-5
LOG2E = math.log2(math.e)

LANES = 128
SUBLANES = 8
C_SHIFT_PAD = ((C_SHIFT_WIDTH + LANES - 1) // LANES) * LANES
GA_WIDTH = A_WIDTH + 2 * A_KV_WIDTH + A_WIDTH
GB_WIDTH = 2 * B_WIDTH
GC_WIDTH = C_SHIFT_WIDTH + C_WIDTH
D_IN_PAD = GA_WIDTH + GB_WIDTH + GC_WIDTH
D_MIX = A_WIDTH + B_WIDTH + C_WIDTH
PC_R, PC_K, PC_V, PC_KAP, PC_B, PC_LW, PC_G = range(7)
PC_WIDTH = 7 * C_WIDTH

CHUNK = 64
STACK = C_HEADS * CHUNK
RWKV_GROUP = 8
ATT_TILE = 2048
LRU_TILE = 256
LRU_STAGE_ROWS = SUBLANES * (LRU_TILE // SUBLANES + 4)
PROJ_TM = 1024
OUT_TM = 1024
FUSED_TM = 1024
OUT_SUB = 256
VMEM_LIMIT = 56 * 1024 * 1024


def _mm(a, b):
    return jnp.dot(a.astype(BF16), b.astype(BF16), preferred_element_type=F32)


def _mm_nt(a, b):
    return lax.dot_general(a.astype(BF16), b.astype(BF16), (((1,), (1,)), ((), ())),
                           preferred_element_type=F32)


def _mm_tn(a, b):
    return lax.dot_general(a.astype(BF16), b.astype(BF16), (((0,), (0,)), ((), ())),
                           preferred_element_type=F32)


def _split_bf16(x, parts):
    out = []
    rem = x
    for _ in range(parts):
        hi = rem.astype(BF16)
        out.append(hi)
        rem = rem - hi.astype(F32)
    return out


def _mm_exact_lhs(m_bf16, x, parts):
    n = x.shape[1]
    t = jnp.dot(m_bf16, jnp.concatenate(_split_bf16(x, parts), axis=1), preferred_element_type=F32)
    acc = t[:, 0:n]
    for p in range(1, parts):
        acc = acc + t[:, p * n:(p + 1) * n]
    return acc


def _sigmoid(x):
    return 1.0 / (1.0 + jnp.exp2(x * (-LOG2E)))


def _silu(x):
    return x * _sigmoid(x)


def _softplus(x):
    return jnp.maximum(x, 0.0) + jnp.log(1.0 + jnp.exp(-jnp.abs(x)))


def _layer_spec(shape, layer, single_buffer=False):
    index_map = lambda *_: (layer,) + (0,) * len(shape)
    if single_buffer:
        return pl.BlockSpec((None,) + shape, index_map, pipeline_mode=pl.Buffered(1))
    return pl.BlockSpec((None,) + shape, index_map)


def _const_spec(shape):
    return pl.BlockSpec(shape, lambda *_: (0,) * len(shape))


def _rwkv_tokenwise(cs, g, prev_row, mu, vec, w2p, a2p, ones_bd):
    w0, a0, k_k, k_a = (vec[i:i + 1, :] for i in range(4))
    row = lax.broadcasted_iota(jnp.int32, cs.shape, 0)
    prev = jnp.where(row == 0, prev_row, pltpu.roll(cs, 1, axis=0))
    xs = cs + (prev - cs) * mu
    r = xs[:, 0:C_WIDTH]
    k = xs[:, C_WIDTH:2 * C_WIDTH]
    v = xs[:, 2 * C_WIDTH:3 * C_WIDTH]
    la = xs[:, 3 * C_WIDTH:C_SHIFT_PAD]
    kk = k * k_k
    yield
    z_w = _mm(jnp.tanh(la), w2p)
    z_a = _mm(la, a2p)
    norm2 = _mm(kk * kk, ones_bd)
    yield
    lw = (-LOG2E * math.exp(-0.5)) * _sigmoid(w0 + z_w)
    a = _sigmoid(a0 + z_a)
    kap = kk * lax.rsqrt(norm2 + 1e-12)
    k2 = k * (1.0 + (a - 1.0) * k_a)
    return jnp.concatenate([r, k2, v, kap, kap * a, lw, g], axis=1), cs[cs.shape[0] - 1:, :]


def _drain(stage):
    if stage is not None:
        for _ in stage:
            pass


def _proj_rows(x16, w_ref, row0, nrows, seq, tok_refs, last_ref, oa_ref, ob_ref, oc_ref, pending):
    mu_ref, vec_ref, w2_ref, a2_ref, ones_ref = tok_refs
    nt = (((1,), (1,)), ((), ()))
    rows = pl.ds(row0, nrows)

    def advance():
        if pending is not None:
            next(pending, None)

    oa_ref[rows, :] = lax.dot_general(x16, w_ref[0:GA_WIDTH, :], nt,
                                      preferred_element_type=F32).astype(BF16)
    advance()
    ob_ref[rows, :] = lax.dot_general(x16, w_ref[GA_WIDTH:GA_WIDTH + GB_WIDTH, :], nt,
                                      preferred_element_type=F32)
    advance()
    c0 = GA_WIDTH + GB_WIDTH
    cs = lax.dot_general(x16, w_ref[c0:c0 + C_SHIFT_PAD, :], nt, preferred_element_type=F32)
    advance()
    g = lax.dot_general(x16, w_ref[c0 + C_SHIFT_WIDTH:D_IN_PAD, :], nt, preferred_element_type=F32)
    _drain(pending)
    first = pl.program_id(0) * oa_ref.shape[0] + row0

    def stage():
        prev_row = jnp.where(first % seq == 0, 0.0, last_ref[0:1, :])
        slab, last = yield from _rwkv_tokenwise(cs, g, prev_row, mu_ref[...], vec_ref[...],
                                                w2_ref[...], a2_ref[...], ones_ref[...])
        oc_ref[rows, :] = slab
        last_ref[0:1, :] = last

    return stage()


def _init_last(last_ref):
    @pl.when(pl.program_id(0) == 0)
    def _():
        last_ref[...] = jnp.zeros_like(last_ref)


def _proj_kernel(seq, x_ref, w_ref, mu_ref, vec_ref, w2_ref, a2_ref, ones_ref, oa_ref, ob_ref, oc_ref,
                 last_ref):
    _init_last(last_ref)
    pending = None
    for row0 in range(0, PROJ_TM, OUT_SUB):
        pending = _proj_rows(x_ref[pl.ds(row0, OUT_SUB), :].astype(BF16), w_ref, row0, OUT_SUB, seq,
                             (mu_ref, vec_ref, w2_ref, a2_ref, ones_ref), last_ref, oa_ref, ob_ref,
                             oc_ref, pending)
    _drain(pending)


def _tok_specs(layer):
    return [_layer_spec((1, C_SHIFT_PAD), layer), _layer_spec((8, C_WIDTH), layer),
            _layer_spec((LANES, C_WIDTH), layer), _layer_spec((LANES, C_WIDTH), layer),
            _const_spec((STACK, STACK))]


def _proj(x2, w_all, tok_args, layer, seq):
    n = x2.shape[0]
    return pl.pallas_call(
        functools.partial(_proj_kernel, seq),
        grid=(n // PROJ_TM,),
        in_specs=[pl.BlockSpec((PROJ_TM, D_MODEL), lambda i: (i, 0)),
                  _layer_spec((D_IN_PAD, D_MODEL), layer, single_buffer=True)] + _tok_specs(layer),
        out_specs=[pl.BlockSpec((PROJ_TM, GA_WIDTH), lambda i: (i, 0)),
                   pl.BlockSpec((PROJ_TM, GB_WIDTH), lambda i: (i, 0)),
                   pl.BlockSpec((PROJ_TM, PC_WIDTH), lambda i: (i, 0))],
        out_shape=[jax.ShapeDtypeStruct((n, GA_WIDTH), BF16),
                   jax.ShapeDtypeStruct((n, GB_WIDTH), F32),
                   jax.ShapeDtypeStruct((n, PC_WIDTH), F32)],
        scratch_shapes=[pltpu.VMEM((SUBLANES, C_SHIFT_PAD), F32)],
        compiler_params=pltpu.CompilerParams(dimension_semantics=("arbitrary",),
                                             vmem_limit_bytes=VMEM_LIMIT),
        name="proj",
    )(x2, w_all, *tok_args)


def _attn_kernel(layer, sink_ref, cur_ref, prev_ref, o_ref):
    n = pl.program_id(1)
    blk = ATT_BLOCK
    nsub = ATT_TILE // blk
    grp = A_Q_HEADS // A_KV_HEADS
    lo = lax.broadcasted_iota(jnp.int32, (blk, A_KV_WIDTH), 1) < HEAD_DIM

    def variants(x16):
        x = x16.astype(F32)
        xs = pltpu.roll(x, HEAD_DIM, axis=1)
        return ((jnp.where(lo, x, 0.0).astype(BF16), jnp.where(lo, 0.0, xs).astype(BF16)),
                (jnp.where(lo, xs, 0.0).astype(BF16), jnp.where(lo, 0.0, x).astype(BF16)))

    k0, v0 = A_WIDTH, A_WIDTH + A_KV_WIDTH
    g0 = A_WIDTH + 2 * A_KV_WIDTH
    kblk = [variants(prev_ref[0, :, 0:A_KV_WIDTH])]
    vblk = [variants(prev_ref[0, :, A_KV_WIDTH:2 * A_KV_WIDTH])]
    for j in range(nsub):
        kblk.append(variants(cur_ref[0, j * blk:(j + 1) * blk, k0:k0 + A_KV_WIDTH]))
        vblk.append(variants(cur_ref[0, j * blk:(j + 1) * blk, v0:v0 + A_KV_WIDTH]))

    qi = lax.broadcasted_iota(jnp.int32, (blk, 2 * blk), 0)
    kj = lax.broadcasted_iota(jnp.int32, (blk, 2 * blk), 1)
    diff = qi + blk - kj
    band = (diff >= 0) & (diff < blk)
    first = band & (kj + (n * nsub - 1) * blk >= 0)

    for j in range(nsub):
        mask = first if j == 0 else band
        rows = slice(j * blk, (j + 1) * blk)
        scores = []
        for p in range(A_Q_HEADS // 2):
            h = (2 * p) // grp
            qp = cur_ref[0, rows, p * LANES:(p + 1) * LANES]
            for half in range(2):
                kb = jnp.concatenate([kblk[j][h][half], kblk[j + 1][h][half]], axis=0)
                scores.append(lax.dot_general(qp, kb, (((1,), (1,)), ((), ())),
                                              preferred_element_type=F32))
        probs, inv = [], []
        for idx, s in enumerate(scores):
            sink2 = sink_ref[layer, idx] * LOG2E
            s = jnp.where(mask, s, -jnp.inf)
            m = jnp.maximum(jnp.max(s, axis=-1, keepdims=True), sink2)
            e = jnp.exp2(s - m)
            den = jnp.sum(e, axis=-1, keepdims=True) + jnp.exp2(sink2 - m)
            probs.append(e.astype(BF16))
            inv.append(1.0 / den)
        for p in range(A_Q_HEADS // 2):
            h = (2 * p) // grp
            acc = None
            for half in range(2):
                vb = jnp.concatenate([vblk[j][h][half], vblk[j + 1][h][half]], axis=0)
                o = jnp.dot(probs[2 * p + half], vb, preferred_element_type=F32) * inv[2 * p + half]
                acc = o if acc is None else acc + o
            g = cur_ref[0, rows, g0 + p * LANES:g0 + (p + 1) * LANES].astype(F32)
            o_ref[0, rows, p * LANES:(p + 1) * LANES] = (acc * _silu(g)).astype(BF16)


def _attn(pa, sinks, layer):
    bsz, seq, _ = pa.shape
    nsub = ATT_TILE // ATT_BLOCK
    kv_blk = A_WIDTH // (2 * A_KV_WIDTH)
    return pl.pallas_call(
        functools.partial(_attn_kernel, layer),
        grid=(bsz, seq // ATT_TILE),
        in_specs=[pl.BlockSpec(memory_space=pltpu.SMEM),
                  pl.BlockSpec((1, ATT_TILE, GA_WIDTH), lambda b, n: (b, n, 0)),
                  pl.BlockSpec((1, ATT_BLOCK, 2 * A_KV_WIDTH),
                               lambda b, n: (b, jnp.maximum(n * nsub - 1, 0), kv_blk))],
        out_specs=pl.BlockSpec((1, ATT_TILE, A_WIDTH), lambda b, n: (b, n, 0)),
        out_shape=jax.ShapeDtypeStruct((bsz, seq, A_WIDTH), BF16),
        compiler_params=pltpu.CompilerParams(dimension_semantics=("arbitrary", "arbitrary"),
                                             vmem_limit_bytes=VMEM_LIMIT),
        name="attn",
    )(sinks, pa, pa)


def _lru_kernel(pb_ref, cw_ref, vec_ref, wa_ref, wx_ref, o_ref, in_ref, stage_ref):
    seq = pb_ref.shape[1]
    tile = LRU_TILE
    nseg = SUBLANES
    nstep = tile // nseg
    conv_b = vec_ref[0:1, :]
    ba = vec_ref[1:2, :]
    bx = vec_ref[2:3, :]
    lam = vec_ref[3:4, :]
    neg_c_sp = -LRU_C * _softplus(-lam)
    cw = [cw_ref[j:j + 1, :] for j in range(CONV_WIDTH)]
    sub0 = lax.broadcasted_iota(jnp.int32, (nseg, B_WIDTH), 0) == 0

    nhalf = B_WIDTH // LANES

    pitch = nstep + 4
    assert pitch % SUBLANES == 4

    def rows_of(step):
        return pl.ds(step, nseg, stride=pitch)

    def permuted(k0, step):
        return jnp.concatenate([in_ref[k0 + k, rows_of(step), :] for k in range(nhalf)], axis=1)

    hc = jnp.zeros((1, B_WIDTH), F32)
    tail = [jnp.zeros((1, B_WIDTH), F32)] * (CONV_WIDTH - 1)
    for ti in range(seq // tile):
        t0 = ti * tile
        for k in range(2 * nhalf):
            for sgi in range(nseg):
                in_ref[k, sgi * pitch:sgi * pitch + nstep, :] = pb_ref[
                    0, t0 + sgi * nstep:t0 + (sgi + 1) * nstep, k * LANES:(k + 1) * LANES]
        xs = [permuted(0, v) for v in range(nstep)]
        wrapped = []
        for i in range(CONV_WIDTH - 1):
            src_v = xs[nstep - (CONV_WIDTH - 1) + i]
            wrapped.append(jnp.where(sub0, tail[i], pltpu.roll(src_v, 1, axis=0)))
        x_all = jnp.concatenate(xs, axis=0)
        xc = cw[CONV_WIDTH - 1] * x_all + conv_b
        for j in range(1, CONV_WIDTH):
            shifted = jnp.concatenate(wrapped[CONV_WIDTH - 1 - j:] + xs[:nstep - j], axis=0)
            xc = xc + cw[CONV_WIDTH - 1 - j] * shifted
        tail = [xs[nstep - (CONV_WIDTH - 1) + i][nseg - 1:nseg, :] for i in range(CONV_WIDTH - 1)]

        r = _sigmoid(_mm(xc, wa_ref[...]) + ba)
        ig = _sigmoid(_mm(xc, wx_ref[...]) + bx)
        a = jnp.exp(neg_c_sp * r)
        u = jnp.sqrt(1.0 - a * a) * (ig * xc)
        hl = [u[0:nseg]]
        al = [a[0:nseg]]
        for v in range(1, nstep):
            av = a[v * nseg:(v + 1) * nseg]
            hl.append(av * hl[-1] + u[v * nseg:(v + 1) * nseg])
            al.append(av * al[-1])
        carry = hc
        enter = []
        for sgi in range(nseg):
            enter.append(carry)
            carry = hl[-1][sgi:sgi + 1, :] + al[-1][sgi:sgi + 1, :] * carry
        hc = carry
        enter = jnp.concatenate(enter, axis=0)
        for v in range(nstep):
            out = (hl[v] + al[v] * enter) * _silu(permuted(nhalf, v))
            for k in range(nhalf):
                stage_ref[k, rows_of(v), :] = out[:, k * LANES:(k + 1) * LANES]
        for k in range(nhalf):
            for sgi in range(nseg):
                o_ref[0, t0 + sgi * nstep:t0 + (sgi + 1) * nstep, k * LANES:(k + 1) * LANES] = (
                    stage_ref[k, sgi * pitch:sgi * pitch + nstep, :].astype(BF16))


def _lru(pb, cw_all, vec_all, wa_all, wx_all, layer):
    bsz, seq, _ = pb.shape
    return pl.pallas_call(
        _lru_kernel,
        grid=(bsz,),
        in_specs=[pl.BlockSpec((1, seq, GB_WIDTH), lambda b: (b, 0, 0)),
                  _layer_spec((CONV_WIDTH, B_WIDTH), layer), _layer_spec((4, B_WIDTH), layer),
                  _layer_spec((B_WIDTH, B_WIDTH), layer), _layer_spec((B_WIDTH, B_WIDTH), layer)],
        out_specs=pl.BlockSpec((1, seq, B_WIDTH), lambda b: (b, 0, 0)),
        out_shape=jax.ShapeDtypeStruct((bsz, seq, B_WIDTH), BF16),
        scratch_shapes=[pltpu.VMEM((GB_WIDTH // LANES, LRU_STAGE_ROWS, LANES), F32),
                        pltpu.VMEM((B_WIDTH // LANES, LRU_STAGE_ROWS, LANES), F32)],
        compiler_params=pltpu.CompilerParams(dimension_semantics=("arbitrary",),
                                             vmem_limit_bytes=VMEM_LIMIT),
        name="lru",
    )(pb, cw_all, vec_all, wa_all, wx_all)


def _stack_heads(x):
    x16 = x.astype(BF16)
    head = lax.broadcasted_iota(jnp.int32, x16.shape, 1) // HEAD_DIM
    return jnp.concatenate([jnp.where(head == h, x16, jnp.zeros_like(x16)) for h in range(C_HEADS)],
                           axis=0)


def _unstack_heads(x):
    head = lax.broadcasted_iota(jnp.int32, (CHUNK, STACK), 1) // HEAD_DIM
    out = x[(C_HEADS - 1) * CHUNK:C_HEADS * CHUNK]
    for h in range(C_HEADS - 2, -1, -1):
        out = jnp.where(head == h, x[h * CHUNK:(h + 1) * CHUNK], out)
    return out


def _rwkv_pre(slab, vfirst, vec, v1p, v2p, tril_l, sl, le, eye_w):
    r, k2, v, kap, b, lw = (slab[:, i * C_WIDTH:(i + 1) * C_WIDTH]
                            for i in (PC_R, PC_K, PC_V, PC_KAP, PC_B, PC_LW))
    r_k = vec[4:5, :]
    v0 = vec[7:8, :]
    if vfirst is not None:
        v = v + (vfirst - v) * _sigmoid(v0 + _mm(_mm(v, v1p), v2p))
    yield
    cum = _mm_exact_lhs(tril_l, lw, 2)
    cum_end = cum[CHUNK - 1:CHUNK, :]
    e_in = jnp.exp2(cum)
    e_ex = jnp.exp2(cum - lw)
    e_neg = jnp.exp2(-cum)
    p_end = jnp.exp2(cum_end)
    e_end = p_end * e_neg

    yield
    kt = kap * e_ex
    rt = r * e_in
    kt_st = _stack_heads(kt)
    vs_st = _stack_heads(v)
    gram = _mm_nt(jnp.concatenate([kt, rt], axis=0),
                  jnp.concatenate([_stack_heads(b * e_neg), _stack_heads(k2 * e_neg)], axis=0))
    a_ab = jnp.where(sl, gram[0:CHUNK, 0:STACK], 0.0)
    a_ak = jnp.where(sl, gram[0:CHUNK, STACK:2 * STACK], 0.0)
    a_rb = jnp.where(le, gram[CHUNK:2 * CHUNK, 0:STACK], 0.0)
    a_rk = jnp.where(le, gram[CHUNK:2 * CHUNK, STACK:2 * STACK], 0.0)

    yield
    pw = -a_ab
    t_w = eye_w + pw
    av = _mm(jnp.concatenate([a_ak, a_rk], axis=0), vs_st)
    pw = _mm(pw, _stack_heads(pw))
    for _ in range(1, int(math.log2(CHUNK)) - 1):
        yield
        both = _mm(jnp.concatenate([t_w, pw], axis=0), _stack_heads(pw))
        t_w = t_w + both[0:CHUNK]
        pw = both[CHUNK:2 * CHUNK]
    yield
    t_w = t_w + _mm(t_w, _stack_heads(pw))
    yield
    tx = _mm(t_w, jnp.concatenate([kt_st, _stack_heads(av[0:CHUNK])], axis=1))
    yield
    m1 = tx[:, 0:STACK]
    c1 = tx[:, STACK:2 * STACK]
    ab = _mm(a_rb, jnp.concatenate([_stack_heads(m1), _stack_heads(c1)], axis=1))
    q_n = rt - ab[:, 0:STACK]
    d_n = av[CHUNK:2 * CHUNK] - ab[:, STACK:2 * STACK]
    bh = b * e_end
    g_w = eye_w * p_end - _unstack_heads(_mm_tn(bh, m1))
    c_w = _unstack_heads(_mm_tn(jnp.concatenate([k2 * e_end, bh], axis=0),
                                jnp.concatenate([v, -c1], axis=0)))
    bonus_arg = r * k2 * r_k
    return q_n, d_n, g_w, c_w, bonus_arg, v, slab[:, PC_G * C_WIDTH:PC_WIDTH]


RWKV_SAVED = 7


def _rwkv_kernel(use_vres, *refs):
    if use_vres:
        (pc_ref, vf_ref, vec_ref, v1_ref, v2_ref, tril_ref, bd_ref,
         sl_ref, le_ref, eye_ref, y_ref, h_ref, pre_ref, ystage_ref) = refs
    else:
        (pc_ref, vec_ref, v1_ref, v2_ref, tril_ref, bd_ref,
         sl_ref, le_ref, eye_ref, y_ref, h_ref, pre_ref, ystage_ref) = refs
        vf_ref = None
    seq = pc_ref.shape[1]
    ngroups = seq // (CHUNK * RWKV_GROUP)
    step_id = pl.program_id(0)
    last_step = pl.num_programs(0) - 1

    @pl.when(step_id == 0)
    def _():
        h_ref[...] = jnp.zeros_like(h_ref)
        pre_ref[...] = jnp.zeros_like(pre_ref)
        ystage_ref[...] = jnp.zeros_like(ystage_ref)

    def chunk_start(gi, j):
        return pl.multiple_of((gi * RWKV_GROUP + j) * CHUNK, CHUNK)

    def pre_generators(gi):
        sl = sl_ref[...] > 0.0
        le = le_ref[...] > 0.0
        gens = []
        for j in range(RWKV_GROUP):
            t0 = chunk_start(gi, j)
            slab = pc_ref[0, pl.ds(t0, CHUNK), :]
            vfirst = vf_ref[0, pl.ds(t0, CHUNK), :] if use_vres else None
            gens.append(_rwkv_pre(slab, vfirst, vec_ref[...], v1_ref[...], v2_ref[...], tril_ref[...],
                                  sl, le, eye_ref[...]))
        return gens

    def chain_tasks(gi):
        ys = [None] * RWKV_GROUP
        stats = [None] * RWKV_GROUP
        state = {}

        def step(j):
            def run():
                h_w = h_ref[...] if j == 0 else state["h"]
                q_n, d_n, g_w, c_w = (pre_ref[j * RWKV_SAVED + k] for k in range(4))
                res = _mm(jnp.concatenate([q_n, g_w], axis=0), _stack_heads(h_w))
                ys[j] = res[0:CHUNK] + d_n
                state["h"] = res[CHUNK:2 * CHUNK] + c_w
                if j == RWKV_GROUP - 1:
                    h_ref[...] = state["h"]
            return run

        def sums():
            for j in range(RWKV_GROUP):
                parts = _split_bf16(ys[j], 2) + [pre_ref[j * RWKV_SAVED + 4].astype(BF16)]
                s = jnp.dot(jnp.concatenate(parts, axis=0), bd_ref[...], preferred_element_type=F32)
                yc = ys[j] - (s[0:CHUNK] + s[CHUNK:2 * CHUNK]) * (1.0 / HEAD_DIM)
                stats[j] = (yc, s[2 * CHUNK:3 * CHUNK])

        def variance():
            for j in range(RWKV_GROUP):
                yc, bonus = stats[j]
                yv = _mm(yc * yc, bd_ref[...]) * (1.0 / HEAD_DIM)
                stats[j] = (yc, bonus, yv)

        def finish():
            gn_w = vec_ref[5:6, :]
            gn_b = vec_ref[6:7, :]
            for j in range(RWKV_GROUP):
                t0 = chunk_start(gi, j)
                yc, bonus, yv = stats[j]
                yn = yc * lax.rsqrt(yv + GN_EPS) * gn_w + gn_b
                out = (yn + bonus * pre_ref[j * RWKV_SAVED + 5]) * _silu(pre_ref[j * RWKV_SAVED + 6])
                ystage_ref[pl.ds(t0, CHUNK), :] = out.astype(BF16)

        return [step(j) for j in range(RWKV_GROUP)] + [sums, variance, finish]

    def run_group(gi, tasks):
        gens = pre_generators(gi)
        tasks = list(tasks)
        pre = [None] * RWKV_GROUP
        sweep = 0
        while any(p is None for p in pre):
            for j, gen in enumerate(gens):
                if pre[j] is None:
                    try:
                        next(gen)
                    except StopIteration as stop:
                        pre[j] = stop.value
            sweep += 1
            if tasks and sweep >= 2:
                tasks.pop(0)()
        for t in tasks:
            t()
        for j in range(RWKV_GROUP):
            for k in range(RWKV_SAVED):
                pre_ref[j * RWKV_SAVED + k] = pre[j][k]

    def flush():
        y_ref[0] = ystage_ref[...]
        h_ref[...] = jnp.zeros_like(h_ref)

    @pl.when(step_id < last_step)
    def _():
        run_group(0, chain_tasks(ngroups - 1) + [flush])

        def body(i, carry):
            run_group(i, chain_tasks(i - 1))
            return carry

        lax.fori_loop(1, ngroups, body, 0)

    @pl.when(step_id == last_step)
    def _():
        for t in chain_tasks(ngroups - 1) + [flush]:
            t()


def _rwkv_consts():
    idx = np.arange(STACK)
    same = (idx[:, None] // CHUNK) == (idx[None, :] // CHUNK)
    ci = np.arange(CHUNK)
    tril_l = ci[None, :] <= ci[:, None]
    s_loc = idx[None, :] % CHUNK
    sl = s_loc < ci[:, None]
    le = s_loc <= ci[:, None]
    eye_w = s_loc == ci[:, None]
    f32 = lambda m: jnp.asarray(m.astype(np.float32))
    return f32(tril_l).astype(BF16), f32(same).astype(BF16), f32(sl), f32(le), f32(eye_w)


def _rwkv(pc, pc_first, vec_all, v1_all, v2_all, consts, layer):
    bsz, seq, _ = pc.shape
    use_vres = pc_first is not None
    sq = (STACK, STACK)
    wide = (CHUNK, STACK)
    cur = lambda b: jnp.minimum(b, bsz - 1)
    in_specs = [pl.BlockSpec((1, seq, PC_WIDTH), lambda b: (cur(b), 0, 0))]
    if use_vres:
        in_specs.append(pl.BlockSpec((1, seq, C_WIDTH), lambda b: (cur(b), 0, PC_V)))
    in_specs += [_layer_spec((8, C_WIDTH), layer),
                 _layer_spec((C_WIDTH, LANES), layer), _layer_spec((LANES, C_WIDTH), layer),
                 _const_spec((CHUNK, CHUNK)), _const_spec(sq),
                 _const_spec(wide), _const_spec(wide), _const_spec(wide)]
    args = (pc,) + ((pc_first,) if use_vres else ()) + (vec_all, v1_all, v2_all)
    return pl.pallas_call(
        functools.partial(_rwkv_kernel, use_vres),
        grid=(bsz + 1,),
        in_specs=in_specs,
        out_specs=pl.BlockSpec((1, seq, C_WIDTH), lambda b: (jnp.maximum(b - 1, 0), 0, 0)),
        out_shape=jax.ShapeDtypeStruct((bsz, seq, C_WIDTH), BF16),
        scratch_shapes=[pltpu.VMEM(wide, F32), pltpu.VMEM((RWKV_GROUP * RWKV_SAVED,) + wide, F32),
                        pltpu.VMEM((seq, C_WIDTH), BF16)],
        compiler_params=pltpu.CompilerParams(dimension_semantics=("arbitrary",),
                                             vmem_limit_bytes=VMEM_LIMIT),
        name="rwkv",
    )(*args, *consts)


def _out_kernel(alpha, tm, seq, fused, ya_ref, yb_ref, yc_ref, x_ref, w_ref, ln_ref, *rest):
    if fused:
        win_ref, *tok_refs = rest[0:6]
        o_ref, oa_ref, ob_ref, oc_ref, last_ref = rest[6:]
        _init_last(last_ref)
    else:
        (o_ref,) = rest

    def project(rows):
        acc = jnp.dot(ya_ref[rows, :], w_ref[0:A_WIDTH, :], preferred_element_type=F32)
        acc = acc + jnp.dot(yb_ref[rows, :], w_ref[A_WIDTH:A_WIDTH + B_WIDTH, :],
                            preferred_element_type=F32)
        return acc + jnp.dot(yc_ref[rows, :], w_ref[A_WIDTH + B_WIDTH:D_MIX, :],
                             preferred_element_type=F32)

    def norm(rows, acc, pending):
        z = alpha * x_ref[rows, :] + acc
        mu = jnp.mean(z, axis=-1, keepdims=True)
        zc = z - mu
        var = jnp.mean(zc * zc, axis=-1, keepdims=True)
        xn = zc * lax.rsqrt(var + LN_EPS) * ln_ref[0:1, :] + ln_ref[1:2, :]
        o_ref[rows, :] = xn
        if not fused:
            return None
        return _proj_rows(xn.astype(BF16), win_ref, rows.start, OUT_SUB, seq, tok_refs, last_ref,
                          oa_ref, ob_ref, oc_ref, pending)

    subs = [slice(s * OUT_SUB, (s + 1) * OUT_SUB) for s in range(tm // OUT_SUB)]
    acc_prev = project(subs[0])
    pending = None
    for s in range(len(subs)):
        acc_next = project(subs[s + 1]) if s + 1 < len(subs) else None
        pending = norm(subs[s], acc_prev, pending)
        acc_prev = acc_next
    _drain(pending)


def _out(ya, yb, yc, x2, w_all, ln_all, alpha, layer, seq, w_in_all=None, tok_args=()):
    n = x2.shape[0]
    fused = w_in_all is not None
    tm = FUSED_TM if fused else OUT_TM
    row = lambda w: pl.BlockSpec((tm, w), lambda i: (i, 0))
    in_specs = [row(A_WIDTH), row(B_WIDTH), row(C_WIDTH), row(D_MODEL),
                _layer_spec((D_MIX, D_MODEL), layer, single_buffer=True),
                _layer_spec((2, D_MODEL), layer)]
    out_specs = [row(D_MODEL)]
    out_shape = [jax.ShapeDtypeStruct((n, D_MODEL), F32)]
    args = [ya, yb, yc, x2, w_all, ln_all]
    if fused:
        in_specs.append(_layer_spec((D_IN_PAD, D_MODEL), layer + 1, single_buffer=True))
        in_specs += _tok_specs(layer + 1)
        out_specs += [row(GA_WIDTH), row(GB_WIDTH), row(PC_WIDTH)]
        out_shape += [jax.ShapeDtypeStruct((n, GA_WIDTH), BF16),
                      jax.ShapeDtypeStruct((n, GB_WIDTH), F32),
                      jax.ShapeDtypeStruct((n, PC_WIDTH), F32)]
        args += [w_in_all, *tok_args]
    return pl.pallas_call(
        functools.partial(_out_kernel, alpha, tm, seq, fused),
        grid=(n // tm,),
        in_specs=in_specs,
        out_specs=out_specs,
        out_shape=out_shape,
        scratch_shapes=[pltpu.VMEM((SUBLANES, C_SHIFT_PAD), F32)] if fused else [],
        compiler_params=pltpu.CompilerParams(dimension_semantics=("arbitrary",),
                                             vmem_limit_bytes=VMEM_LIMIT),
        name="outproj_proj" if fused else "outproj",
    )(*args)


def _block_diag_all(w):
    depth, nblk, d, _ = w.shape
    eye = jnp.eye(nblk, dtype=w.dtype)
    return (w[:, :, :, None, :] * eye[None, :, None, :, None]).reshape(depth, nblk * d, nblk * d)


def kernel(x, w_in, w_out, ln_g, ln_b, attn_sinks, conv_w, conv_b, lru_wa, lru_ba, lru_wx, lru_bx,
           lru_lambda, rwkv_mu, rwkv_w0, rwkv_w2, rwkv_a0, rwkv_a2, rwkv_kk, rwkv_ka, rwkv_rk,
           rwkv_gn_w, rwkv_gn_b, rwkv_v0, rwkv_v1, rwkv_v2):
    bsz, seq, dm = x.shape
    depth = w_in.shape[0]
    alpha = (2 * depth) ** 0.25
    n = bsz * seq
    pad_w = C_SHIFT_PAD - C_SHIFT_WIDTH

    q_scale = np.ones((w_in.shape[2],), np.float32)
    q_scale[:A_WIDTH] = HEAD_DIM ** -0.5 * LOG2E
    w_in_p = (jnp.swapaxes(w_in, 1, 2) * q_scale[None, :, None]).astype(BF16)
    w_out_b = w_out.astype(BF16)
    ln_all = jnp.stack([ln_g, ln_b], axis=1)
    lru_vec = jnp.stack([conv_b, lru_ba, lru_bx, lru_lambda], axis=1)
    wa_bd = _block_diag_all(lru_wa).astype(BF16)
    wx_bd = _block_diag_all(lru_wx).astype(BF16)
    mu_all = jnp.pad(rwkv_mu, ((0, 0), (0, pad_w)))[:, None, :]
    v0_all = jnp.pad(rwkv_v0, ((1, 0), (0, 0)))
    vec_all = jnp.stack([rwkv_w0, rwkv_a0, rwkv_kk, rwkv_ka, rwkv_rk.reshape(depth, C_WIDTH),
                         rwkv_gn_w, rwkv_gn_b, v0_all], axis=1)
    w2_all = jnp.pad(rwkv_w2, ((0, 0), (0, LANES - DECAY_RANK), (0, 0))).astype(BF16)
    a2_all = jnp.pad(rwkv_a2, ((0, 0), (DECAY_RANK, LANES - DECAY_RANK - AICL_RANK), (0, 0))).astype(BF16)
    v1_all = jnp.pad(rwkv_v1, ((1, 0), (0, 0), (0, LANES - VRES_RANK))).astype(BF16)
    v2_all = jnp.pad(rwkv_v2, ((1, 0), (0, LANES - VRES_RANK), (0, 0))).astype(BF16)
    consts = _rwkv_consts()

    tok_args = (mu_all, vec_all, w2_all, a2_all, consts[1])
    x2 = x.reshape(n, dm)
    pc_first = None
    pa, pb, pc = _proj(x2, w_in_p, tok_args, 0, seq)
    for l in range(depth):
        pc3 = pc.reshape(bsz, seq, PC_WIDTH)
        ya = _attn(pa.reshape(bsz, seq, GA_WIDTH), attn_sinks, l)
        yb = _lru(pb.reshape(bsz, seq, GB_WIDTH), conv_w, lru_vec, wa_bd, wx_bd, l)
        yc = _rwkv(pc3, pc_first, vec_all, v1_all, v2_all, consts, l)
        if l == 0:
            pc_first = pc3
        outs = _out(ya.reshape(n, A_WIDTH), yb.reshape(n, B_WIDTH), yc.reshape(n, C_WIDTH), x2,
                    w_out_b, ln_all, alpha, l, seq, w_in_p if l + 1 < depth else None, tok_args)
        x2 = outs[0]
        if l + 1 < depth:
            pa, pb, pc = outs[1:]
    return x2.reshape(bsz, seq, dm)
```
